```python
import math
import jax, jax.numpy as jnp
from jax import lax
import numpy as np


D_MODEL = 1024
BATCH = 8
SEQ = 4096
DEPTH = 2

DEEPNORM_ALPHA = (2 * DEPTH) ** 0.25
DEEPNORM_BETA = (8 * DEPTH) ** -0.25
LN_EPS = 1e-5
ROPE_THETA = 500000.0
ROPE_FRACTION = 4
RWKV_HEAD_DIM = 64
RWKV_DIM = D_MODEL // 2
RWKV_HEADS = RWKV_DIM // RWKV_HEAD_DIM
DECAY_LORA = max(32, int(round(1.8 * D_MODEL ** 0.5 / 32)) * 32)
ICLR_LORA = max(32, int(round(1.8 * D_MODEL ** 0.5 / 32)) * 32)
GATE_LORA = max(32, int(round(0.6 * D_MODEL ** 0.8 / 32)) * 32)
RWKV_GN_EPS = 64e-5
RWKV_PROJ_COLS = 3 * RWKV_DIM + DECAY_LORA + ICLR_LORA + GATE_LORA
MOBA_HEAD_DIM = 128
MOBA_WIDTH = D_MODEL - RWKV_DIM
MOBA_HEADS = MOBA_WIDTH // MOBA_HEAD_DIM
MOBA_BLOCK = 256
MOBA_TOPK = 3
MOBA_Q_CHUNK = 16
AB_IN_COLS = RWKV_PROJ_COLS + 3 * MOBA_WIDTH
AB_OUT_COLS = RWKV_DIM + MOBA_WIDTH
DIL_PAIRS = ((128, 1), (512, 4), (2048, 16))
DIL_GROUPS = len(DIL_PAIRS)
DIL_HEAD_DIM = 128
DIL_HEADS = D_MODEL // DIL_HEAD_DIM
C_WIDTH = DIL_HEADS * DIL_HEAD_DIM
C_IN_COLS = DIL_GROUPS * C_WIDTH + 2 * C_WIDTH
MLP_HIDDEN = 4 * D_MODEL
NEG_INF = -1e30

kernel_name = 'hybrid_rwkv7_moba_dilated_deepnorm'


def _layer_norm(x, g, b):
    xf = x.astype(jnp.float32)
    mu = jnp.mean(xf, -1, keepdims=True)
    var = jnp.mean(jnp.square(xf - mu), -1, keepdims=True)
    return ((xf - mu) * lax.rsqrt(var + LN_EPS) * g + b).astype(x.dtype)


def _partial_rotary(x, pos):
    rot = x.shape[-1] // ROPE_FRACTION
    half = rot // 2
    inv_freq = ROPE_THETA ** (-jnp.arange(half, dtype=jnp.float32) / half)
    ang = pos.astype(jnp.float32)[:, None] * inv_freq[None, :]
    cos = jnp.cos(ang).astype(x.dtype)
    sin = jnp.sin(ang).astype(x.dtype)
    x1, x2, rest = x[..., :half], x[..., half:rot], x[..., rot:]
    return jnp.concatenate([x1 * cos - x2 * sin, x2 * cos + x1 * sin, rest], axis=-1)


def _token_shift(z):
    return jnp.pad(z, ((0, 0), (1, 0), (0, 0)))[:, :-1]


def _rwkv7_time_mix(z, shift_mix, w0, w_up, a0, a_up, g_up, k_k, k_a, r_k, lnx_g, lnx_b):
    B, T, _ = z.shape
    H, N = RWKV_HEADS, RWKV_HEAD_DIM
    f32 = jnp.float32
    z = z + (_token_shift(z) - z) * shift_mix
    o1, o2, o3 = RWKV_DIM, 2 * RWKV_DIM, 3 * RWKV_DIM
    o4 = o3 + DECAY_LORA
    o5 = o4 + ICLR_LORA
    r, k, v = z[..., :o1], z[..., o1:o2], z[..., o2:o3]
    w_d, a_d, g_d = z[..., o3:o4], z[..., o4:o5], z[..., o5:]
    w = -jax.nn.softplus(-(w0 + jnp.tanh(w_d) @ w_up)) - 0.5
    decay = jnp.exp(-jnp.exp(w.astype(f32)))
    a = jax.nn.sigmoid(a0 + a_d @ a_up).astype(f32)
    g = jax.nn.sigmoid(g_d) @ g_up
    heads = lambda t: t.astype(f32).reshape(B, T, H, N)
    kk = heads(k * k_k)
    kk = kk / jnp.maximum(jnp.linalg.norm(kk, axis=-1, keepdims=True), 1e-12)
    k = k.astype(f32) * (1.0 + (a - 1.0) * k_a)
    r, k, v, a, decay = heads(r), heads(k), heads(v), heads(a), heads(decay)

    def step(S, inp):
        r_t, w_t, k_t, v_t, a_t, b_t = inp
        sa = jnp.einsum('bhvk,bhk->bhv', S, a_t)
        S = S * w_t[:, :, None, :] + sa[..., None] * b_t[:, :, None, :] + v_t[..., None] * k_t[:, :, None, :]
        return S, jnp.einsum('bhvk,bhk->bhv', S, r_t)

    seq_first = lambda t: jnp.moveaxis(t, 1, 0)
    xs = (seq_first(r), seq_first(decay), seq_first(k), seq_first(v), seq_first(-kk), seq_first(kk * a))
    _, y = lax.scan(step, jnp.zeros((B, H, N, N), f32), xs)
    y = jnp.moveaxis(y, 0, 1)
    mu = jnp.mean(y, -1, keepdims=True)
    var = jnp.mean(jnp.square(y - mu), -1, keepdims=True)
    y = ((y - mu) * lax.rsqrt(var + RWKV_GN_EPS)).reshape(B, T, RWKV_DIM) * lnx_g + lnx_b
    bonus = jnp.sum(r * k * r_k, -1, keepdims=True) * v
    y = y + bonus.reshape(B, T, RWKV_DIM)
    return y * g


def _moba_attention(q, k, v):
    B, H, T, Dh = q.shape
    f32 = jnp.float32
    nb = -(-T // MOBA_BLOCK)
    Tp = nb * MOBA_BLOCK
    pad = ((0, 0), (0, 0), (0, Tp - T), (0, 0))
    q, k, v = jnp.pad(q, pad), jnp.pad(k, pad), jnp.pad(v, pad)
    scale = Dh ** -0.5
    kb = k.reshape(B, H, nb, MOBA_BLOCK, Dh)
    vb = v.reshape(B, H, nb, MOBA_BLOCK, Dh)
    n_sel = min(MOBA_TOPK, nb - 1)
    sel = None
    if n_sel > 0:
        k_mean = jnp.mean(kb.astype(f32), axis=3)
        gate = jnp.einsum('bhtd,bhnd->bhtn', q.astype(f32), k_mean)
        q_blk = jnp.arange(Tp) // MOBA_BLOCK
        past = jnp.arange(nb)[None, :] < q_blk[:, None]
        gate = jnp.where(past, gate, NEG_INF)
        _, sel = lax.top_k(gate, n_sel)
    bi = jnp.arange(B)[:, None, None, None]
    hi = jnp.arange(H)[None, :, None, None]

    def chunk(c):
        t0 = c * MOBA_Q_CHUNK
        i = t0 // MOBA_BLOCK
        qc = lax.dynamic_slice_in_dim(q, t0, MOBA_Q_CHUNK, axis=2)
        k_own = lax.dynamic_slice_in_dim(k, i * MOBA_BLOCK, MOBA_BLOCK, axis=2)
        v_own = lax.dynamic_slice_in_dim(v, i * MOBA_BLOCK, MOBA_BLOCK, axis=2)
        s_own = jnp.einsum('bhqd,bhkd->bhqk', qc, k_own).astype(f32) * scale
        causal = (i * MOBA_BLOCK + jnp.arange(MOBA_BLOCK))[None, :] <= (t0 + jnp.arange(MOBA_Q_CHUNK))[:, None]
        s_own = jnp.where(causal, s_own, NEG_INF)
        if n_sel == 0:
            p = jax.nn.softmax(s_own, axis=-1).astype(v.dtype)
            return jnp.einsum('bhqk,bhkd->bhqd', p, v_own)
        idx = lax.dynamic_slice_in_dim(sel, t0, MOBA_Q_CHUNK, axis=2)
        k_sel = kb[bi, hi, idx]
        v_sel = vb[bi, hi, idx]
        s_sel = jnp.einsum('bhqd,bhqskd->bhqsk', qc, k_sel).astype(f32) * scale
        valid = jnp.arange(n_sel) < i
        s_sel = jnp.where(valid[:, None], s_sel, NEG_INF)
        n_k = n_sel * MOBA_BLOCK
        s_all = jnp.concatenate([s_sel.reshape(B, H, MOBA_Q_CHUNK, n_k), s_own], axis=-1)
        p = jax.nn.softmax(s_all, axis=-1).astype(v.dtype)
        p_sel = p[..., :n_k].reshape(B, H, MOBA_Q_CHUNK, n_sel, MOBA_BLOCK)
        return (jnp.einsum('bhqsk,bhqskd->bhqd', p_sel, v_sel)
                + jnp.einsum('bhqk,bhkd->bhqd', p[..., n_k:], v_own))

    out = lax.map(chunk, jnp.arange(Tp // MOBA_Q_CHUNK))
    out = jnp.moveaxis(out, 0, 2).reshape(B, H, Tp, Dh)
    return out[:, :, :T]


def _dilated_window_branch(q, k, v, span, dilation):
    B, H, T, Dh = q.shape
    f32 = jnp.float32
    L = T // dilation
    W = span // dilation
    nblk = -(-L // W)
    Lp = nblk * W

    def strided(t):
        t = t.reshape(B, H, L, dilation, Dh).transpose(0, 1, 3, 2, 4)
        return jnp.pad(t, ((0, 0), (0, 0), (0, 0), (0, Lp - L), (0, 0)))

    qb = strided(q).reshape(B, H, dilation, nblk, W, Dh)
    kb = strided(k).reshape(B, H, dilation, nblk, W, Dh)
    vb = strided(v).reshape(B, H, dilation, nblk, W, Dh)
    prev = lambda t: jnp.pad(t, ((0, 0), (0, 0), (0, 0), (1, 0), (0, 0), (0, 0)))[:, :, :, :-1]
    k_band = jnp.concatenate([prev(kb), kb], axis=4)
    v_band = jnp.concatenate([prev(vb), vb], axis=4)
    s = jnp.einsum('bhrnqd,bhrnkd->bhrnqk', qb, k_band).astype(f32) * (Dh ** -0.5)
    qi = jnp.arange(W)[:, None]
    kj = jnp.arange(2 * W)[None, :]
    dist = W + qi - kj
    blk = jnp.arange(nblk)[:, None, None]
    mask = (dist >= 0) & (dist <= W) & ((blk > 0) | (kj >= W))
    s = jnp.where(mask, s, NEG_INF)
    lse = jax.nn.logsumexp(s, axis=-1)
    p = jnp.exp(s - lse[..., None]).astype(v.dtype)
    o = jnp.einsum('bhrnqk,bhrnkd->bhrnqd', p, v_band)
    o = o.reshape(B, H, dilation, Lp, Dh)[:, :, :, :L].transpose(0, 1, 3, 2, 4).reshape(B, H, T, Dh)
    lse = lse.reshape(B, H, dilation, Lp)[..., :L].transpose(0, 1, 3, 2).reshape(B, H, T)
    return o, lse


def _rwkv_moba_mixer(x, w_in, shift_mix, w0, w_up, a0, a_up, g_up, k_k, k_a, r_k, lnx_g, lnx_b, w_out, pos):
    B, T, _ = x.shape
    z = x @ w_in
    y_a = _rwkv7_time_mix(z[..., :RWKV_PROJ_COLS], shift_mix, w0, w_up, a0, a_up, g_up,
                          k_k, k_a, r_k, lnx_g, lnx_b)
    qkv = z[..., RWKV_PROJ_COLS:].reshape(B, T, 3, MOBA_HEADS, MOBA_HEAD_DIM).transpose(2, 0, 3, 1, 4)
    q = _partial_rotary(qkv[0], pos)
    k = _partial_rotary(qkv[1], pos)
    y_b = _moba_attention(q, k, qkv[2]).transpose(0, 2, 1, 3).reshape(B, T, MOBA_WIDTH)
    y = jnp.concatenate([y_a.astype(x.dtype), y_b.astype(x.dtype)], axis=-1)
    return (y @ w_out).astype(x.dtype)


def _dilated_mixer(x, w_in, w_out, pos):
    B, T, _ = x.shape
    z = x @ w_in
    nq = DIL_GROUPS * C_WIDTH
    q = z[..., :nq].reshape(B, T, DIL_GROUPS, DIL_HEADS, DIL_HEAD_DIM).transpose(0, 2, 3, 1, 4)
    kv = z[..., nq:].reshape(B, T, 2, DIL_HEADS, DIL_HEAD_DIM).transpose(2, 0, 3, 1, 4)
    q = _partial_rotary(q, pos)
    k = _partial_rotary(kv[0], pos)
    v = kv[1]
    outs, lses = [], []
    for g, (span, dilation) in enumerate(DIL_PAIRS):
        o, l = _dilated_window_branch(q[:, g], k, v, span, dilation)
        outs.append(o)
        lses.append(l)
    wts = jax.nn.softmax(jnp.stack(lses, axis=0), axis=0).astype(v.dtype)
    o = jnp.einsum('gbht,gbhtd->bhtd', wts, jnp.stack(outs, axis=0))
    o = o.transpose(0, 2, 1, 3).reshape(B, T, C_WIDTH)
    return (o @ w_out).astype(x.dtype)


def _sq_relu_mlp(x, w1, w2):
    return jnp.square(jax.nn.relu(x @ w1)) @ w2


def setup_inputs(seed: int = 0) -> dict:
    key = jax.random.key(seed)
    ks = iter(jax.random.split(key, 24))
    f32 = jnp.float32
    nrm = lambda shape, scale: jax.random.normal(next(ks), shape, f32) * scale
    D = D_MODEL
    NE = (DEPTH + 1) // 2
    NO = DEPTH // 2
    x = nrm((BATCH, SEQ, D), 1.0)
    ab_w_in = nrm((NE, D, AB_IN_COLS), D ** -0.5)
    ab_shift_mix = jax.random.uniform(next(ks), (NE, RWKV_PROJ_COLS), f32)
    ab_w0 = jax.random.uniform(next(ks), (NE, RWKV_DIM), f32, minval=-6.0, maxval=0.0)
    ab_w_up = nrm((NE, DECAY_LORA, RWKV_DIM), 0.5 * DECAY_LORA ** -0.5)
    ab_a0 = nrm((NE, RWKV_DIM), 0.1)
    ab_a_up = nrm((NE, ICLR_LORA, RWKV_DIM), 0.5 * ICLR_LORA ** -0.5)
    ab_g_up = nrm((NE, GATE_LORA, RWKV_DIM), GATE_LORA ** -0.5)
    ab_k_k = 0.85 + nrm((NE, RWKV_DIM), 0.02)
    ab_k_a = 1.0 + nrm((NE, RWKV_DIM), 0.02)
    ab_r_k = nrm((NE, RWKV_HEADS, RWKV_HEAD_DIM), 0.1)
    ab_lnx_g = 1.0 + nrm((NE, RWKV_DIM), 0.02)
    ab_lnx_b = nrm((NE, RWKV_DIM), 0.02)
    ab_w_out = nrm((NE, AB_OUT_COLS, D), DEEPNORM_BETA * AB_OUT_COLS ** -0.5)
    c_w_in = nrm((NO, D, C_IN_COLS), D ** -0.5)
    c_w_out = nrm((NO, C_WIDTH, D), DEEPNORM_BETA * C_WIDTH ** -0.5)
    ln1_g = 1.0 + nrm((DEPTH, D), 0.02)
    ln1_b = nrm((DEPTH, D), 0.02)
    mlp_w1 = nrm((DEPTH, D, MLP_HIDDEN), D ** -0.5)
    mlp_w2 = nrm((DEPTH, MLP_HIDDEN, D), DEEPNORM_BETA * MLP_HIDDEN ** -0.5)
    ln2_g = 1.0 + nrm((DEPTH, D), 0.02)
    ln2_b = nrm((DEPTH, D), 0.02)
    return {'x': x, 'ab_w_in': ab_w_in, 'ab_shift_mix': ab_shift_mix, 'ab_w0': ab_w0,
            'ab_w_up': ab_w_up, 'ab_a0': ab_a0, 'ab_a_up': ab_a_up, 'ab_g_up': ab_g_up,
            'ab_k_k': ab_k_k, 'ab_k_a': ab_k_a, 'ab_r_k': ab_r_k, 'ab_lnx_g': ab_lnx_g,
            'ab_lnx_b': ab_lnx_b, 'ab_w_out': ab_w_out, 'c_w_in': c_w_in, 'c_w_out': c_w_out,
            'ln1_g': ln1_g, 'ln1_b': ln1_b, 'mlp_w1': mlp_w1, 'mlp_w2': mlp_w2,
            'ln2_g': ln2_g, 'ln2_b': ln2_b}


def reference(x, ab_w_in, ab_shift_mix, ab_w0, ab_w_up, ab_a0, ab_a_up, ab_g_up, ab_k_k, ab_k_a,
              ab_r_k, ab_lnx_g, ab_lnx_b, ab_w_out, c_w_in, c_w_out, ln1_g, ln1_b, mlp_w1, mlp_w2,
              ln2_g, ln2_b):
    T = x.shape[1]
    pos = jnp.arange(T)
    for layer in range(DEPTH):
        j = layer // 2
        if layer % 2 == 0:
            h = _rwkv_moba_mixer(x, ab_w_in[j], ab_shift_mix[j], ab_w0[j], ab_w_up[j], ab_a0[j],
                                 ab_a_up[j], ab_g_up[j], ab_k_k[j], ab_k_a[j], ab_r_k[j],
                                 ab_lnx_g[j], ab_lnx_b[j], ab_w_out[j], pos)
        else:
            h = _dilated_mixer(x, c_w_in[j], c_w_out[j], pos)
        x = _layer_norm(DEEPNORM_ALPHA * x + h, ln1_g[layer], ln1_b[layer])
        x = _layer_norm(DEEPNORM_ALPHA * x + _sq_relu_mlp(x, mlp_w1[layer], mlp_w2[layer]),
                        ln2_g[layer], ln2_b[layer])
    return x
```

```python
import functools

import jax
import jax.numpy as jnp
from jax import lax
from jax.experimental import pallas as pl
from jax.experimental.pallas import tpu as pltpu

F32 = jnp.float32
BF16 = jnp.bfloat16

D_MODEL = 1024
DEPTH = 2
DEEPNORM_ALPHA = (2 * DEPTH) ** 0.25
LN_EPS = 1e-5
ROPE_THETA = 500000.0
ROPE_FRACTION = 4
RWKV_HEAD_DIM = 64
RWKV_DIM = D_MODEL // 2
RWKV_HEADS = RWKV_DIM // RWKV_HEAD_DIM
DECAY_LORA = 64
ICLR_LORA = 64
GATE_LORA = 160
RWKV_GN_EPS = 64e-5
RWKV_PROJ_COLS = 3 * RWKV_DIM + DECAY_LORA + ICLR_LORA + GATE_LORA
MOBA_HEAD_DIM = 128
MOBA_WIDTH = D_MODEL - RWKV_DIM
MOBA_HEADS = MOBA_WIDTH // MOBA_HEAD_DIM
MOBA_BLOCK = 256
MOBA_TOPK = 3
DIL_PAIRS = ((128, 1), (512, 4), (2048, 16))
DIL_GROUPS = len(DIL_PAIRS)
DIL_HEAD_DIM = 128
DIL_HEADS = D_MODEL // DIL_HEAD_DIM
C_WIDTH = DIL_HEADS * DIL_HEAD_DIM
C_IN_COLS = DIL_GROUPS * C_WIDTH + 2 * C_WIDTH
MLP_HIDDEN = 4 * D_MODEL
NEG_INF = -1e30

LANES = 128
SUBLANES = 8
VMEM_LIMIT_BYTES = 56 * 1024 * 1024

LORA_COLS = DECAY_LORA + ICLR_LORA + GATE_LORA
LORA_PAD = -(-LORA_COLS // LANES) * LANES
RWKV_Z_COLS = 3 * RWKV_DIM + LORA_PAD
RWKV_CHUNK = 64
DIL_WINDOW = 128
DIL_ROWS = 256

_NN = (((1,), (0,)), ((), ()))
_NT = (((1,), (1,)), ((), ()))
_TN = (((0,), (0,)), ((), ()))


def _bdot(a, b, dims=_NN):
    return lax.dot_general(a.astype(BF16), b.astype(BF16), dims, preferred_element_type=F32)


def _layer_norm(y, g, b):
    mu = jnp.mean(y, axis=-1, keepdims=True)
    var = jnp.mean(jnp.square(y - mu), axis=-1, keepdims=True)
    return (y - mu) * lax.rsqrt(var + LN_EPS) * g + b


def _params(*sem):
    return pltpu.CompilerParams(dimension_semantics=sem, vmem_limit_bytes=VMEM_LIMIT_BYTES)


def _rotary_tables(seq_len, head_dim, q_scale):
    rot = head_dim // ROPE_FRACTION
    half = rot // 2
    inv_freq = ROPE_THETA ** (-jnp.arange(half, dtype=F32) / half)
    ang = jnp.arange(seq_len).astype(F32)[:, None] * inv_freq[None, :]
    cos, sin = jnp.cos(ang), jnp.sin(ang)
    zeros = lambda n: jnp.zeros((seq_len, n), F32)
    c = jnp.concatenate([cos, cos, jnp.ones((seq_len, head_dim - rot), F32)], axis=1)
    s1 = jnp.concatenate([-sin, zeros(head_dim - half)], axis=1)
    s2 = jnp.concatenate([zeros(half), sin, zeros(head_dim - rot)], axis=1)
    plain = jnp.stack([c, s1, s2])
    ident = jnp.stack([jnp.ones_like(c), jnp.zeros_like(c), jnp.zeros_like(c)])
    return jnp.stack([plain * q_scale, plain, ident])


def _proj_kernel(x_ref, w_ref, *rest, rotary):
    if rotary:
        tab_ref, o_ref, xb_ref = rest
    else:
        o_ref, xb_ref = rest

    @pl.when(pl.program_id(1) == 0)
    def _():
        xb_ref[...] = x_ref[...].astype(BF16)

    acc = jnp.dot(xb_ref[...], w_ref[...], preferred_element_type=F32)
    if not rotary:
        o_ref[...] = acc.astype(o_ref.dtype)
        return
    c, s1, s2 = tab_ref[0], tab_ref[1], tab_ref[2]
    half = LANES // ROPE_FRACTION // 2
    for hb in range(acc.shape[1] // LANES):
        seg = acc[:, hb * LANES:(hb + 1) * LANES]
        seg = seg * c + pltpu.roll(seg, LANES - half, 1) * s1 + pltpu.roll(seg, half, 1) * s2
        o_ref[:, hb * LANES:(hb + 1) * LANES] = seg.astype(o_ref.dtype)


def _proj(x2d, w, *, seq_len, out_dtype, tn, tabs=None, kind_starts=()):
    m, k = x2d.shape
    n = w.shape[1]
    tm = min(1024, seq_len)
    assert m % tm == 0 and seq_len % tm == 0 and n % tn == 0 and tn % LANES == 0
    in_specs = [
        pl.BlockSpec((tm, k), lambda i, j: (i, 0)),
        pl.BlockSpec((k, tn), lambda i, j: (0, j)),
    ]
    args = [x2d, w]
    if tabs is not None:
        blocks_per_seq = seq_len // tm

        def tab_map(i, j):
            kind = sum((j >= s).astype(jnp.int32) for s in kind_starts)
            return (kind, 0, i % blocks_per_seq, 0)

        in_specs.append(pl.BlockSpec((None, 3, tm, LANES), tab_map))
        args.append(tabs)
    return pl.pallas_call(
        functools.partial(_proj_kernel, rotary=tabs is not None),
        grid=(m // tm, n // tn),
        in_specs=in_specs,
        out_specs=pl.BlockSpec((tm, tn), lambda i, j: (i, j)),
        out_shape=jax.ShapeDtypeStruct((m, n), out_dtype),
        scratch_shapes=[pltpu.VMEM((tm, k), BF16)],
        compiler_params=_params("parallel", "arbitrary"),
        name="proj_rotary" if tabs is not None else "proj",
    )(*args)


def _softplus(x):
    return jnp.maximum(x, 0.0) + jnp.log(1.0 + jnp.exp(-jnp.abs(x)))


def _rwkv_prep_kernel(z_ref, prev_ref, mix_ref, wl_ref, par_ref, hsum_ref,
                      r_o, lw_o, kp_o, v_o, na_o, b_o, g_o, *, blocks_per_seq):
    d = RWKV_DIM
    z = z_ref[...]
    first = (pl.program_id(0) % blocks_per_seq) == 0
    prev = jnp.where(first, 0.0, prev_ref[SUBLANES - 1:SUBLANES, :])
    row = lax.broadcasted_iota(jnp.int32, z.shape, 0)
    shifted = jnp.where(row == 0, prev, pltpu.roll(z, 1, 0))
    zz = z + (shifted - z) * mix_ref[...]
    r, k, v, lo = zz[:, 0:d], zz[:, d:2 * d], zz[:, 2 * d:3 * d], zz[:, 3 * d:]
    lane = lax.broadcasted_iota(jnp.int32, lo.shape, 1)
    act = jnp.where(lane < DECAY_LORA, jnp.tanh(lo),
                    jnp.where(lane < DECAY_LORA + ICLR_LORA, lo, jax.nn.sigmoid(lo)))
    lora = _bdot(act, wl_ref[...])
    w0, a0, k_k, k_a = par_ref[0:1, :], par_ref[1:2, :], par_ref[2:3, :], par_ref[3:4, :]
    w = -_softplus(-(w0 + lora[:, 0:d])) - 0.5
    a = jax.nn.sigmoid(a0 + lora[:, d:2 * d])
    kk = k * k_k
    sq = kk * kk
    sq_hi = sq.astype(BF16)
    sq_lo = (sq - sq_hi.astype(F32)).astype(BF16)
    ssq = (jnp.dot(sq_hi, hsum_ref[...], preferred_element_type=F32)
           + jnp.dot(sq_lo, hsum_ref[...], preferred_element_type=F32))
    kk = kk / jnp.maximum(jnp.sqrt(ssq), 1e-12)
    r_o[...] = r
    lw_o[...] = -jnp.exp(w)
    kp_o[...] = k * (1.0 + (a - 1.0) * k_a)
    v_o[...] = v
    na_o[...] = -kk
    b_o[...] = kk * a
    g_o[...] = lora[:, 2 * d:3 * d]


def _rwkv_prep(z, mix, wl, par, hsum, *, seq_len):
    m, zc = z.shape
    tm = min(256, seq_len)
    assert m % tm == 0 and seq_len % tm == 0
    blocks_per_seq = seq_len // tm
    rows8 = tm // SUBLANES
    out = jax.ShapeDtypeStruct((m, RWKV_DIM), F32)
    full = lambda a: pl.BlockSpec(a.shape, lambda i: (0,) * a.ndim)
    return pl.pallas_call(
        functools.partial(_rwkv_prep_kernel, blocks_per_seq=blocks_per_seq),
        grid=(m // tm,),
        in_specs=[
            pl.BlockSpec((tm, zc), lambda i: (i, 0)),
            pl.BlockSpec((SUBLANES, zc), lambda i: (jnp.maximum(i * rows8 - 1, 0), 0)),
            full(mix), full(wl), full(par), full(hsum),
        ],
        out_specs=[pl.BlockSpec((tm, RWKV_DIM), lambda i: (i, 0))] * 7,
        out_shape=[out] * 7,
        compiler_params=_params("parallel"),
        name="rwkv_prep",
    )(z, z, mix, wl, par, hsum)


def _rwkv_chunk_kernel(r_ref, lw_ref, kp_ref, v_ref, na_ref, b_ref, g_ref, par_ref, o_ref, s_ref):
    c, n = RWKV_CHUNK, RWKV_HEAD_DIM

    @pl.when(pl.program_id(1) == 0)
    def _():
        s_ref[...] = jnp.zeros_like(s_ref)

    ti = lax.broadcasted_iota(jnp.int32, (c, c), 0)
    ii = lax.broadcasted_iota(jnp.int32, (c, c), 1)
    tril_inc = ii <= ti
    tril_str = ii < ti
    eye = ii == ti

    lw = lw_ref[...]
    cum = jnp.dot(tril_inc.astype(F32), lw, precision=lax.Precision.HIGHEST,
                  preferred_element_type=F32)
    cum_end = cum[c - 1:c, :]
    e_cum = jnp.exp(cum)
    e_inv = jnp.exp(-cum)
    e_end = jnp.exp(cum_end - cum)
    p_end = jnp.exp(cum_end)
    r, kp, v, b = r_ref[...], kp_ref[...], v_ref[...], b_ref[...]
    at_all = na_ref[...] * jnp.exp(cum - lw)
    rt_all = r * e_cum
    bt_all = b * e_inv
    kt_all = kp * e_inv
    bh_all = b * e_end
    kh_all = kp * e_end
    r_k, lnx_g, lnx_b = par_ref[0:1, :], par_ref[1:2, :], par_ref[2:3, :]

    level_masks = [((ti >> (lvl + 1)) == (ii >> (lvl + 1))) & (((ti >> lvl) & 1) == 1) & (((ii >> lvl) & 1) == 0)
                   for lvl in range(c.bit_length() - 1)]
    pair_mask, level_masks = level_masks[0], level_masks[1:]

    for h in range(RWKV_HEADS):
        sl = slice(h * n, (h + 1) * n)
        at, rt, bt, kt, bh, kh, vh = (at_all[:, sl], rt_all[:, sl], bt_all[:, sl], kt_all[:, sl],
                                      bh_all[:, sl], kh_all[:, sl], v[:, sl])
        ar = jnp.concatenate([at, rt], axis=0)
        gb = _bdot(ar, bt, _NT)
        gk = _bdot(ar, kt, _NT)
        a_ab = jnp.where(tril_str, gb[:c], 0.0)
        a_rb = jnp.where(tril_inc, gb[c:], 0.0)
        a_ak = jnp.where(tril_str, gk[:c], 0.0)
        a_rk = jnp.where(tril_inc, gk[c:], 0.0)
        inv = jnp.where(eye, 1.0, jnp.where(pair_mask, a_ab, 0.0))
        for mask in level_masks:
            inv = inv + _bdot(inv, _bdot(jnp.where(mask, a_ab, 0.0), inv))
        w_mat = _bdot(inv, at)
        u_loc = _bdot(inv, _bdot(a_ak, vh))
        m_c = _bdot(w_mat, bh, _TN) + jnp.where(eye, p_end[:, sl], 0.0)
        n_c = _bdot(u_loc, bh, _TN) + _bdot(vh, kh, _TN)
        q_c = rt + _bdot(a_rb, w_mat)
        y_loc = _bdot(a_rb, u_loc) + _bdot(a_rk, vh)
        s0 = s_ref[h]
        y = _bdot(q_c, s0, _NT) + y_loc
        s_ref[h] = _bdot(s0, m_c) + n_c
        mu = jnp.mean(y, axis=-1, keepdims=True)
        var = jnp.mean(jnp.square(y - mu), axis=-1, keepdims=True)
        yn = (y - mu) * lax.rsqrt(var + RWKV_GN_EPS) * lnx_g[:, sl] + lnx_b[:, sl]
        bonus = jnp.sum(r[:, sl] * kp[:, sl] * r_k[:, sl], axis=-1, keepdims=True) * vh
        o_ref[:, sl] = ((yn + bonus) * g_ref[:, sl]).astype(o_ref.dtype)


def _rwkv_scan(r, lw, kp, v, na, b, g, par, *, batch, seq_len):
    c = RWKV_CHUNK
    assert seq_len % c == 0
    to3 = lambda a: a.reshape(batch, seq_len, RWKV_DIM)
    spec = pl.BlockSpec((None, c, RWKV_DIM), lambda bi, ci: (bi, ci, 0))
    out = pl.pallas_call(
        _rwkv_chunk_kernel,
        grid=(batch, seq_len // c),
        in_specs=[spec] * 7 + [pl.BlockSpec(par.shape, lambda bi, ci: (0, 0))],
        out_specs=spec,
        out_shape=jax.ShapeDtypeStruct((batch, seq_len, RWKV_DIM), BF16),
        scratch_shapes=[pltpu.VMEM((RWKV_HEADS, RWKV_HEAD_DIM, RWKV_HEAD_DIM), F32)],
        compiler_params=_params("parallel", "arbitrary"),
        name="rwkv_scan",
    )(to3(r), to3(lw), to3(kp), to3(v), to3(na), to3(b), to3(g), par)
    return out.reshape(batch * seq_len, RWKV_DIM)


def _moba_kernel(q_ref, k_ref, v_ref, pool_ref, o_ref, kmh_ref, kml_ref):
    blk = MOBA_BLOCK
    i = pl.program_id(2)

    @pl.when(i == 0)
    def _():
        km = jnp.dot(pool_ref[...], k_ref[...], preferred_element_type=F32)
        hi = km.astype(BF16)
        kmh_ref[...] = hi
        kml_ref[...] = (km - hi.astype(F32)).astype(BF16)

    q = q_ref[...]
    gate = (lax.dot_general(q, kmh_ref[...], _NT, preferred_element_type=F32)
            + lax.dot_general(q, kml_ref[...], _NT, preferred_element_type=F32))
    lane = lax.broadcasted_iota(jnp.int32, gate.shape, 1)
    past = lane < i
    gate = jnp.where(past, gate, NEG_INF)
    n_blocks = k_ref.shape[0] // blk
    rank = jnp.zeros(gate.shape, jnp.int32)
    for jp in range(n_blocks):
        col = gate[:, jp:jp + 1]
        rank = rank + ((col > gate) | ((col == gate) & (jp < lane))).astype(jnp.int32)
    sel = (past & (rank < MOBA_TOPK)).astype(F32)

    row = lax.broadcasted_iota(jnp.int32, (blk, blk), 0)
    colk = lax.broadcasted_iota(jnp.int32, (blk, blk), 1)
    own = pl.multiple_of(i * blk, blk)
    s = lax.dot_general(q, k_ref[pl.ds(own, blk), :], _NT, preferred_element_type=F32)
    s = jnp.where(colk <= row, s, NEG_INF)
    m = jnp.max(s, axis=-1, keepdims=True)
    p = jnp.exp(s - m)
    l = jnp.sum(p, axis=-1, keepdims=True)
    acc = jnp.dot(p.astype(BF16), v_ref[pl.ds(own, blk), :], preferred_element_type=F32)

    def body(j, carry):
        m, l, acc = carry
        start = pl.multiple_of(j * blk, blk)
        s = lax.dot_general(q, k_ref[pl.ds(start, blk), :], _NT, preferred_element_type=F32)
        chosen = jnp.sum(jnp.where(lane == j, sel, 0.0), axis=-1, keepdims=True)
        s = jnp.where(chosen > 0.0, s, NEG_INF)
        m_new = jnp.maximum(m, jnp.max(s, axis=-1, keepdims=True))
        alpha = jnp.exp(m - m_new)
        p = jnp.exp(s - m_new)
        l = alpha * l + jnp.sum(p, axis=-1, keepdims=True)
        acc = alpha * acc + jnp.dot(p.astype(BF16), v_ref[pl.ds(start, blk), :],
                                    preferred_element_type=F32)
        return m_new, l, acc

    m, l, acc = lax.fori_loop(0, i, body, (m, l, acc))
    o_ref[...] = (acc / l).astype(o_ref.dtype)


def _moba(qkv, *, batch, seq_len):
    blk, dh, nh = MOBA_BLOCK, MOBA_HEAD_DIM, MOBA_HEADS
    assert seq_len % blk == 0 and seq_len // blk <= LANES
    nb = seq_len // blk
    pool = (jnp.arange(LANES)[:, None] == (jnp.arange(seq_len) // blk)[None, :]).astype(BF16) / blk
    qkv3 = qkv.reshape(batch, seq_len, 3 * MOBA_WIDTH)
    out = pl.pallas_call(
        _moba_kernel,
        grid=(batch, nh, nb),
        in_specs=[
            pl.BlockSpec((None, blk, dh), lambda b, h, i: (b, i, h)),
            pl.BlockSpec((None, seq_len, dh), lambda b, h, i: (b, 0, nh + h)),
            pl.BlockSpec((None, seq_len, dh), lambda b, h, i: (b, 0, 2 * nh + h)),
            pl.BlockSpec((LANES, seq_len), lambda b, h, i: (0, 0)),
        ],
        out_specs=pl.BlockSpec((None, blk, dh), lambda b, h, i: (b, i, h)),
        out_shape=jax.ShapeDtypeStruct((batch, seq_len, MOBA_WIDTH), BF16),
        scratch_shapes=[pltpu.VMEM((LANES, dh), BF16), pltpu.VMEM((LANES, dh), BF16)],
        compiler_params=_params("parallel", "parallel", "arbitrary"),
        name="moba",
    )(qkv3, qkv3, qkv3, pool)
    return out.reshape(batch * seq_len, MOBA_WIDTH)


def _dilated_kernel(q_ref, k_ref, kp_ref, v_ref, vp_ref, o_ref, lse_ref):
    w, dh = DIL_WINDOW, DIL_HEAD_DIM
    tiles = q_ref.shape[0] // w
    has_prev = pl.program_id(2) > 0
    row = lax.broadcasted_iota(jnp.int32, (w, w), 0)
    col = lax.broadcasted_iota(jnp.int32, (w, w), 1)
    own_ok = col <= row
    prev_ok = col >= row
    eye = col == row
    for h in range(DIL_HEADS):
        hs = slice(h * dh, (h + 1) * dh)
        for t in range(tiles):
            ts = slice(t * w, (t + 1) * w)
            q = q_ref[ts, hs]
            if t == 0:
                k_prev, v_prev, ok = kp_ref[:, hs], vp_ref[:, hs], prev_ok & has_prev
            else:
                ps = slice((t - 1) * w, t * w)
                k_prev, v_prev, ok = k_ref[ps, hs], v_ref[ps, hs], prev_ok
            s_own = jnp.where(own_ok, lax.dot_general(q, k_ref[ts, hs], _NT, preferred_element_type=F32), NEG_INF)
            s_prev = jnp.where(ok, lax.dot_general(q, k_prev, _NT, preferred_element_type=F32), NEG_INF)
            m = jnp.maximum(jnp.max(s_own, axis=-1, keepdims=True), jnp.max(s_prev, axis=-1, keepdims=True))
            p_own = jnp.exp(s_own - m)
            p_prev = jnp.exp(s_prev - m)
            l = jnp.sum(p_own, axis=-1, keepdims=True) + jnp.sum(p_prev, axis=-1, keepdims=True)
            o = (jnp.dot(p_own.astype(BF16), v_ref[ts, hs], preferred_element_type=F32)
                 + jnp.dot(p_prev.astype(BF16), v_prev, preferred_element_type=F32)) / l
            o_ref[ts, hs] = o.astype(o_ref.dtype)
            lse = m + jnp.log(l)
            lse_ref[t, h:h + 1, :] = jnp.sum(jnp.where(eye, lse, 0.0), axis=0, keepdims=True)


def _dilated_group(qkv, group, dilation, *, batch, seq_len):
    d, w = dilation, DIL_WINDOW
    length = seq_len // d
    rows = min(DIL_ROWS, length)
    assert length % rows == 0 and rows % w == 0
    tiles = rows // w
    per_res = C_IN_COLS // C_WIDTH
    view = qkv.reshape(batch, length, d * C_IN_COLS)
    kcol, vcol = DIL_GROUPS, DIL_GROUPS + 1
    blk = lambda cb: pl.BlockSpec((None, rows, C_WIDTH), lambda b, r, i: (b, i, r * per_res + cb))
    halo = lambda cb: pl.BlockSpec(
        (None, w, C_WIDTH), lambda b, r, i: (b, jnp.maximum(i * tiles - 1, 0), r * per_res + cb))
    o, lse = pl.pallas_call(
        _dilated_kernel,
        grid=(batch, d, length // rows),
        in_specs=[blk(group), blk(kcol), halo(kcol), blk(vcol), halo(vcol)],
        out_specs=[
            pl.BlockSpec((None, rows, C_WIDTH), lambda b, r, i: (b, i, r)),
            pl.BlockSpec((None, None, tiles, DIL_HEADS, w), lambda b, r, i: (b, r, i, 0, 0)),
        ],
        out_shape=[
            jax.ShapeDtypeStruct((batch, length, d * C_WIDTH), BF16),
            jax.ShapeDtypeStruct((batch, d, length // w, DIL_HEADS, w), F32),
        ],
        compiler_params=_params("parallel", "parallel", "arbitrary"),
        name=f"dilated_d{d}",
    )(view, view, view, view, view)
    lse = lse.transpose(0, 2, 4, 1, 3).reshape(batch * seq_len, DIL_HEADS)
    return o.reshape(batch * seq_len, C_WIDTH), lse


def _outproj_ln_kernel(*refs, n_act, merge):
    acts = refs[:n_act]
    pos = n_act
    if merge:
        lse_refs = refs[pos:pos + n_act]
        pos += n_act
    w_ref, x_ref, g_ref, b_ref, o_ref = refs[pos:pos + 5]
    if merge:
        nh, dh = DIL_HEADS, DIL_HEAD_DIM
        parts = [ref[...] for ref in lse_refs]
        m = functools.reduce(jnp.maximum, parts)
        es = [jnp.exp(p - m) for p in parts]
        den = functools.reduce(jnp.add, es)
        wts = [e / den for e in es]
        heads = []
        for h in range(nh):
            hs = slice(h * dh, (h + 1) * dh)
            heads.append(functools.reduce(
                jnp.add, [wts[g][:, h:h + 1] * acts[g][:, hs].astype(F32) for g in range(n_act)]))
        hcat = jnp.concatenate(heads, axis=1)
        proj = _bdot(hcat, w_ref[...])
    else:
        proj = None
        off = 0
        for a_ref in acts:
            kw = a_ref.shape[1]
            part = jnp.dot(a_ref[...], w_ref[off:off + kw, :], preferred_element_type=F32)
            proj = part if proj is None else proj + part
            off += kw
    y = DEEPNORM_ALPHA * x_ref[...] + proj
    o_ref[...] = _layer_norm(y, g_ref[...], b_ref[...])


def _outproj_ln(acts, w, x2d, g, b, lses=()):
    m, dm = x2d.shape
    tm = 512
    assert m % tm == 0 and len(lses) in (0, len(acts))
    rowspec = lambda a: pl.BlockSpec((tm, a.shape[1]), lambda i: (i, 0))
    full = lambda a: pl.BlockSpec(a.shape, lambda i: (0, 0))
    args = list(acts) + list(lses) + [w, x2d, g, b]
    in_specs = ([rowspec(a) for a in acts] + [rowspec(a) for a in lses]
                + [full(w), rowspec(x2d), full(g), full(b)])
    return pl.pallas_call(
        functools.partial(_outproj_ln_kernel, n_act=len(acts), merge=bool(lses)),
        grid=(m // tm,),
        in_specs=in_specs,
        out_specs=pl.BlockSpec((tm, dm), lambda i: (i, 0)),
        out_shape=jax.ShapeDtypeStruct((m, dm), F32),
        compiler_params=_params("parallel"),
        name="outproj_merge_ln" if lses else "outproj_ln",
    )(*args)


def _mlp_ln_kernel(x_ref, w1_ref, w2_ref, g_ref, b_ref, o_ref, xb_ref, acc_ref):
    j = pl.program_id(1)

    @pl.when(j == 0)
    def _():
        xb_ref[...] = x_ref[...].astype(BF16)
        acc_ref[...] = jnp.zeros_like(acc_ref)

    h = jnp.dot(xb_ref[...], w1_ref[...], preferred_element_type=F32)
    h = jnp.square(jnp.maximum(h, 0.0))
    acc_ref[...] += jnp.dot(h.astype(BF16), w2_ref[...], preferred_element_type=F32)

    @pl.when(j == pl.num_programs(1) - 1)
    def _():
        y = DEEPNORM_ALPHA * x_ref[...] + acc_ref[...]
        o_ref[...] = _layer_norm(y, g_ref[...], b_ref[...])


def _mlp_ln(x2d, w1, w2, g, b):
    m, dm = x2d.shape
    hidden = w1.shape[1]
    tm, th = 1024, 512
    assert m % tm == 0 and hidden % th == 0
    return pl.pallas_call(
        _mlp_ln_kernel,
        grid=(m // tm, hidden // th),
        in_specs=[
            pl.BlockSpec((tm, dm), lambda i, j: (i, 0)),
            pl.BlockSpec((dm, th), lambda i, j: (0, j)),
            pl.BlockSpec((th, dm), lambda i, j: (j, 0)),
            pl.BlockSpec((1, dm), lambda i, j: (0, 0)),
            pl.BlockSpec((1, dm), lambda i, j: (0, 0)),
        ],
        out_specs=pl.BlockSpec((tm, dm), lambda i, j: (i, 0)),
        out_shape=jax.ShapeDtypeStruct((m, dm), F32),
        scratch_shapes=[pltpu.VMEM((tm, dm), BF16), pltpu.VMEM((tm, dm), F32)],
        compiler_params=_params("parallel", "arbitrary"),
        name="mlp_ln",
    )(x2d, w1, w2, g, b)


def _rwkv_moba_layer(x2d, w_in, shift_mix, w0, w_up, a0, a_up, g_up, k_k, k_a, r_k, lnx_g, lnx_b,
                     w_out, ln_g, ln_b, *, batch, seq_len):
    d = RWKV_DIM
    tabs = _rotary_tables(seq_len, MOBA_HEAD_DIM, MOBA_HEAD_DIM ** -0.5)
    qkv = _proj(x2d, w_in[:, RWKV_PROJ_COLS:].astype(BF16), seq_len=seq_len, out_dtype=BF16,
                tn=MOBA_WIDTH, tabs=tabs, kind_starts=(1, 2))
    y_b = _moba(qkv, batch=batch, seq_len=seq_len)
    pad = RWKV_Z_COLS - RWKV_PROJ_COLS
    w_r = jnp.pad(w_in[:, :RWKV_PROJ_COLS], ((0, 0), (0, pad))).astype(BF16)
    z = _proj(x2d, w_r, seq_len=seq_len, out_dtype=F32, tn=RWKV_Z_COLS // 5)
    mix = jnp.pad(shift_mix, (0, pad))[None, :]
    wl = jnp.zeros((LORA_PAD, 3 * d), F32)
    wl = wl.at[0:DECAY_LORA, 0:d].set(w_up)
    wl = wl.at[DECAY_LORA:DECAY_LORA + ICLR_LORA, d:2 * d].set(a_up)
    wl = wl.at[DECAY_LORA + ICLR_LORA:LORA_COLS, 2 * d:3 * d].set(g_up).astype(BF16)
    zero = jnp.zeros((d,), F32)
    par = jnp.stack([w0, a0, k_k, k_a, zero, zero, zero, zero])
    head = jnp.arange(d) // RWKV_HEAD_DIM
    hsum = (head[:, None] == head[None, :]).astype(BF16)
    r, lw, kp, v, na, b, g = _rwkv_prep(z, mix, wl, par, hsum, seq_len=seq_len)
    par2 = jnp.stack([r_k.reshape(d), lnx_g, lnx_b, zero, zero, zero, zero, zero])
    y_a = _rwkv_scan(r, lw, kp, v, na, b, g, par2, batch=batch, seq_len=seq_len)
    return _outproj_ln([y_a, y_b], w_out.astype(BF16), x2d, ln_g[None, :], ln_b[None, :])


def _dilated_layer(x2d, w_in, w_out, ln_g, ln_b, *, batch, seq_len):
    tabs = _rotary_tables(seq_len, DIL_HEAD_DIM, DIL_HEAD_DIM ** -0.5)
    tn = C_WIDTH // 2
    nq = DIL_GROUPS * C_WIDTH // tn
    qkv = _proj(x2d, w_in.astype(BF16), seq_len=seq_len, out_dtype=BF16, tn=tn, tabs=tabs,
                kind_starts=(nq, nq + C_WIDTH // tn))
    outs, lses = [], []
    for group, (span, dilation) in enumerate(DIL_PAIRS):
        assert span // dilation == DIL_WINDOW
        o, lse = _dilated_group(qkv, group, dilation, batch=batch, seq_len=seq_len)
        outs.append(o)
        lses.append(lse)
    return _outproj_ln(outs, w_out.astype(BF16), x2d, ln_g[None, :], ln_b[None, :], lses=lses)


def kernel(x, ab_w_in, ab_shift_mix, ab_w0, ab_w_up, ab_a0, ab_a_up, ab_g_up, ab_k_k, ab_k_a, ab_r_k,
           ab_lnx_g, ab_lnx_b, ab_w_out, c_w_in, c_w_out, ln1_g, ln1_b, mlp_w1, mlp_w2, ln2_g, ln2_b):
    batch, seq_len, dm = x.shape
    assert dm == D_MODEL
    h = x.reshape(batch * seq_len, dm)
    depth = ln1_g.shape[0]
    for layer in range(depth):
        j = layer // 2
        if layer % 2 == 0:
            h = _rwkv_moba_layer(h, ab_w_in[j], ab_shift_mix[j], ab_w0[j], ab_w_up[j], ab_a0[j], ab_a_up[j],
                                 ab_g_up[j], ab_k_k[j], ab_k_a[j], ab_r_k[j], ab_lnx_g[j], ab_lnx_b[j],
                                 ab_w_out[j], ln1_g[layer], ln1_b[layer], batch=batch, seq_len=seq_len)
        else:
            h = _dilated_layer(h, c_w_in[j], c_w_out[j], ln1_g[layer], ln1_b[layer],
                               batch=batch, seq_len=seq_len)
        h = _mlp_ln(h, mlp_w1[layer].astype(BF16), mlp_w2[layer].astype(BF16),
                    ln2_g[layer][None, :], ln2_b[layer][None, :])
    return h.reshape(batch, seq_len, dm)
```

```python
import functools

import jax
import jax.numpy as jnp
from jax import lax
from jax.experimental import pallas as pl
from jax.experimental.pallas import tpu as pltpu

F32 = jnp.float32
BF16 = jnp.bfloat16

D_MODEL = 1024
DEPTH = 2
DEEPNORM_ALPHA = (2 * DEPTH) ** 0.25
LN_EPS = 1e-5
ROPE_THETA = 500000.0
ROPE_FRACTION = 4
RWKV_HEAD_DIM = 64
RWKV_DIM = D_MODEL // 2
RWKV_HEADS = RWKV_DIM // RWKV_HEAD_DIM
DECAY_LORA = 64
ICLR_LORA = 64
GATE_LORA = 160
RWKV_GN_EPS = 64e-5
RWKV_PROJ_COLS = 3 * RWKV_DIM + DECAY_LORA + ICLR_LORA + GATE_LORA
MOBA_HEAD_DIM = 128
MOBA_WIDTH = D_MODEL - RWKV_DIM
MOBA_HEADS = MOBA_WIDTH // MOBA_HEAD_DIM
MOBA_BLOCK = 256
MOBA_TOPK = 3
DIL_PAIRS = ((128, 1), (512, 4), (2048, 16))
DIL_GROUPS = len(DIL_PAIRS)
DIL_HEAD_DIM = 128
DIL_HEADS = D_MODEL // DIL_HEAD_DIM
C_WIDTH = DIL_HEADS * DIL_HEAD_DIM
C_IN_COLS = DIL_GROUPS * C_WIDTH + 2 * C_WIDTH
MLP_HIDDEN = 4 * D_MODEL
NEG_INF = -1e30

LANES = 128
SUBLANES = 8
VMEM_LIMIT_BYTES = 56 * 1024 * 1024

LORA_COLS = DECAY_LORA + ICLR_LORA + GATE_LORA
LORA_PAD = -(-LORA_COLS // LANES) * LANES
RWKV_Z_COLS = 3 * RWKV_DIM + LORA_PAD
RWKV_CHUNK = 64
RWKV_SCAN_CHUNKS = 4
DIL_WINDOW = 128
DIL_ROWS = 256

_NN = (((1,), (0,)), ((), ()))
_NT = (((1,), (1,)), ((), ()))
_TN = (((0,), (0,)), ((), ()))


def _bdot(a, b, dims=_NN):
    return lax.dot_general(a.astype(BF16), b.astype(BF16), dims, preferred_element_type=F32)


def _layer_norm(y, g, b):
    mu = jnp.mean(y, axis=-1, keepdims=True)
    var = jnp.mean(jnp.square(y - mu), axis=-1, keepdims=True)
    return (y - mu) * lax.rsqrt(var + LN_EPS) * g + b


def _params(*sem):
    return pltpu.CompilerParams(dimension_semantics=sem, vmem_limit_bytes=VMEM_LIMIT_BYTES)


def _rotary_tables(seq_len, head_dim, q_scale):
    rot = head_dim // ROPE_FRACTION
    half = rot // 2
    inv_freq = ROPE_THETA ** (-jnp.arange(half, dtype=F32) / half)
    ang = jnp.arange(seq_len).astype(F32)[:, None] * inv_freq[None, :]
    cos, sin = jnp.cos(ang), jnp.sin(ang)
    zeros = lambda n: jnp.zeros((seq_len, n), F32)
    c = jnp.concatenate([cos, cos, jnp.ones((seq_len, head_dim - rot), F32)], axis=1)
    s1 = jnp.concatenate([-sin, zeros(head_dim - half)], axis=1)
    s2 = jnp.concatenate([zeros(half), sin, zeros(head_dim - rot)], axis=1)
    plain = jnp.stack([c, s1, s2])
    ident = jnp.stack([jnp.ones_like(c), jnp.zeros_like(c), jnp.zeros_like(c)])
    return jnp.stack([plain * q_scale, plain, ident])


def _proj_kernel(x_ref, w_ref, *rest, rotary):
    if rotary:
        tab_ref, o_ref, xb_ref = rest
    else:
        o_ref, xb_ref = rest

    @pl.when(pl.program_id(1) == 0)
    def _():
        xb_ref[...] = x_ref[...].astype(BF16)

    acc = jnp.dot(xb_ref[...], w_ref[...], preferred_element_type=F32)
    if not rotary:
        o_ref[...] = acc.astype(o_ref.dtype)
        return
    c, s1, s2 = tab_ref[0], tab_ref[1], tab_ref[2]
    half = LANES // ROPE_FRACTION // 2
    for hb in range(acc.shape[1] // LANES):
        seg = acc[:, hb * LANES:(hb + 1) * LANES]
        seg = seg * c + pltpu.roll(seg, LANES - half, 1) * s1 + pltpu.roll(seg, half, 1) * s2
        o_ref[:, hb * LANES:(hb + 1) * LANES] = seg.astype(o_ref.dtype)


def _proj(x2d, w, *, seq_len, out_dtype, tn, tabs=None, kind_starts=()):
    m, k = x2d.shape
    n = w.shape[1]
    tm = min(1024, seq_len)
    assert m % tm == 0 and seq_len % tm == 0 and n % tn == 0 and tn % LANES == 0
    in_specs = [
        pl.BlockSpec((tm, k), lambda i, j: (i, 0)),
        pl.BlockSpec((k, tn), lambda i, j: (0, j)),
    ]
    args = [x2d, w]
    if tabs is not None:
        blocks_per_seq = seq_len // tm

        def tab_map(i, j):
            kind = sum((j >= s).astype(jnp.int32) for s in kind_starts)
            return (kind, 0, i % blocks_per_seq, 0)

        in_specs.append(pl.BlockSpec((None, 3, tm, LANES), tab_map))
        args.append(tabs)
    return pl.pallas_call(
        functools.partial(_proj_kernel, rotary=tabs is not None),
        grid=(m // tm, n // tn),
        in_specs=in_specs,
        out_specs=pl.BlockSpec((tm, tn), lambda i, j: (i, j)),
        out_shape=jax.ShapeDtypeStruct((m, n), out_dtype),
        scratch_shapes=[pltpu.VMEM((tm, k), BF16)],
        compiler_params=_params("parallel", "arbitrary"),
        name="proj_rotary" if tabs is not None else "proj",
    )(*args)


def _softplus(x):
    return jnp.maximum(x, 0.0) + jnp.log(1.0 + jnp.exp(-jnp.abs(x)))


def _rwkv_prep_kernel(z_ref, prev_ref, mix_ref, wl_ref, par_ref, hsum_ref,
                      r_o, lw_o, kp_o, v_o, na_o, b_o, g_o, *, blocks_per_seq):
    d = RWKV_DIM
    z = z_ref[...]
    first = (pl.program_id(0) % blocks_per_seq) == 0
    prev = jnp.where(first, 0.0, prev_ref[SUBLANES - 1:SUBLANES, :])
    row = lax.broadcasted_iota(jnp.int32, z.shape, 0)
    shifted = jnp.where(row == 0, prev, pltpu.roll(z, 1, 0))
    zz = z + (shifted - z) * mix_ref[...]
    r, k, v, lo = zz[:, 0:d], zz[:, d:2 * d], zz[:, 2 * d:3 * d], zz[:, 3 * d:]
    lane = lax.broadcasted_iota(jnp.int32, lo.shape, 1)
    act = jnp.where(lane < DECAY_LORA, jnp.tanh(lo),
                    jnp.where(lane < DECAY_LORA + ICLR_LORA, lo, jax.nn.sigmoid(lo)))
    lora = _bdot(act, wl_ref[...])
    w0, a0, k_k, k_a = par_ref[0:1, :], par_ref[1:2, :], par_ref[2:3, :], par_ref[3:4, :]
    w = -_softplus(-(w0 + lora[:, 0:d])) - 0.5
    a = jax.nn.sigmoid(a0 + lora[:, d:2 * d])
    kk = k * k_k
    sq = kk * kk
    sq_hi = sq.astype(BF16)
    sq_lo = (sq - sq_hi.astype(F32)).astype(BF16)
    ssq = (jnp.dot(sq_hi, hsum_ref[...], preferred_element_type=F32)
           + jnp.dot(sq_lo, hsum_ref[...], preferred_element_type=F32))
    kk = kk / jnp.maximum(jnp.sqrt(ssq), 1e-12)
    r_o[...] = r
    lw_o[...] = -jnp.exp(w)
    kp_o[...] = k * (1.0 + (a - 1.0) * k_a)
    v_o[...] = v
    na_o[...] = -kk
    b_o[...] = kk * a
    g_o[...] = lora[:, 2 * d:3 * d]


def _rwkv_prep(z, mix, wl, par, hsum, *, seq_len):
    m, zc = z.shape
    tm = min(256, seq_len)
    assert m % tm == 0 and seq_len % tm == 0
    blocks_per_seq = seq_len // tm
    rows8 = tm // SUBLANES
    out = jax.ShapeDtypeStruct((m, RWKV_DIM), F32)
    full = lambda a: pl.BlockSpec(a.shape, lambda i: (0,) * a.ndim)
    return pl.pallas_call(
        functools.partial(_rwkv_prep_kernel, blocks_per_seq=blocks_per_seq),
        grid=(m // tm,),
        in_specs=[
            pl.BlockSpec((tm, zc), lambda i: (i, 0)),
            pl.BlockSpec((SUBLANES, zc), lambda i: (jnp.maximum(i * rows8 - 1, 0), 0)),
            full(mix), full(wl), full(par), full(hsum),
        ],
        out_specs=[pl.BlockSpec((tm, RWKV_DIM), lambda i: (i, 0))] * 7,
        out_shape=[out] * 7,
        compiler_params=_params("parallel"),
        name="rwkv_prep",
    )(z, z, mix, wl, par, hsum)


def _split_dot(x, ones_bd):
    hi = x.astype(BF16)
    lo = (x - hi.astype(F32)).astype(BF16)
    return (jnp.dot(hi, ones_bd, preferred_element_type=F32)
            + jnp.dot(lo, ones_bd, preferred_element_type=F32))


def _rwkv_chunk_kernel(r_ref, lw_ref, kp_ref, v_ref, na_ref, b_ref, g_ref, par_ref, o_ref, s_ref, *, chunks):
    c, n = RWKV_CHUNK, RWKV_HEAD_DIM
    pw = 2 * n
    assert c == n and pw == LANES
    pairs = RWKV_HEADS // 2
    units = [(ch, p) for ch in range(chunks) for p in range(pairs)]

    @pl.when(pl.program_id(1) == 0)
    def _():
        s_ref[...] = jnp.zeros_like(s_ref)

    row = lax.broadcasted_iota(jnp.int32, (c, pw), 0)
    col = lax.broadcasted_iota(jnp.int32, (c, pw), 1) & (n - 1)
    tril_inc = col <= row
    tril_str = col < row
    eye = col == row
    level_masks = [((row >> (lvl + 1)) == (col >> (lvl + 1))) & (((row >> lvl) & 1) == 1) & (((col >> lvl) & 1) == 0)
                   for lvl in range(c.bit_length() - 1)]
    r2 = lax.broadcasted_iota(jnp.int32, (pw, pw), 0)
    c2 = lax.broadcasted_iota(jnp.int32, (pw, pw), 1)
    same_head = (r2 < n) == (c2 < n)
    eye2 = r2 == c2
    ones_bd = jnp.where(same_head, 1.0, 0.0).astype(BF16)

    def stack(x):
        xb = x.astype(BF16)
        return jnp.where(same_head, jnp.concatenate([xb, xb], axis=0), jnp.zeros((), BF16))

    rows = chunks * c
    tr = lax.broadcasted_iota(jnp.int32, (rows, rows), 0)
    tc = lax.broadcasted_iota(jnp.int32, (rows, rows), 1)
    chunk_tril = ((tr >> (c.bit_length() - 1)) == (tc >> (c.bit_length() - 1))) & (tc <= tr)
    lw = lw_ref[...]
    cum = jnp.dot(chunk_tril.astype(F32), lw, precision=lax.Precision.HIGHEST,
                  preferred_element_type=F32)
    r_k, lnx_g, lnx_b = par_ref[0:1, :], par_ref[1:2, :], par_ref[2:3, :]

    at, rt, bt, kt, bh, kh, vv, p_end = {}, {}, {}, {}, {}, {}, {}, {}
    for ch in range(chunks):
        rs = slice(ch * c, (ch + 1) * c)
        cum_c = cum[rs]
        cum_end = cum_c[c - 1:c, :]
        e_inv = jnp.exp(-cum_c)
        e_end = jnp.exp(cum_end - cum_c)
        at_c = na_ref[rs, :] * jnp.exp(cum_c - lw[rs])
        rt_c = r_ref[rs, :] * jnp.exp(cum_c)
        b_c, kp_c = b_ref[rs, :], kp_ref[rs, :]
        bt_c, kt_c, bh_c, kh_c = b_c * e_inv, kp_c * e_inv, b_c * e_end, kp_c * e_end
        pe_c = jnp.exp(cum_end)
        for p in range(pairs):
            ls = slice(p * pw, (p + 1) * pw)
            u = (ch, p)
            at[u], rt[u], bt[u], kt[u], bh[u], kh[u] = at_c[:, ls], rt_c[:, ls], bt_c[:, ls], kt_c[:, ls], bh_c[:, ls], kh_c[:, ls]
            vv[u], p_end[u] = v_ref[rs, ls], pe_c[:, ls]

    ar = {u: jnp.concatenate([at[u], rt[u]], axis=0) for u in units}
    gb = {u: _bdot(ar[u], stack(bt[u]), _NT) for u in units}
    gk = {u: _bdot(ar[u], stack(kt[u]), _NT) for u in units}
    a_ab = {u: jnp.where(tril_str, gb[u][:c], 0.0) for u in units}
    a_rb = {u: jnp.where(tril_inc, gb[u][c:], 0.0) for u in units}
    a_ak = {u: jnp.where(tril_str, gk[u][:c], 0.0) for u in units}
    a_rk = {u: jnp.where(tril_inc, gk[u][c:], 0.0) for u in units}
    v_st = {u: stack(vv[u]) for u in units}
    ak_v = {u: _bdot(a_ak[u], v_st[u]) for u in units}
    inv = {u: jnp.where(eye, 1.0, jnp.where(level_masks[0], a_ab[u], 0.0)) for u in units}
    for mask in level_masks[1:]:
        e = {u: _bdot(jnp.where(mask, a_ab[u], 0.0), stack(inv[u])) for u in units}
        inv = {u: inv[u] + _bdot(inv[u], stack(e[u])) for u in units}
    w_mat = {u: _bdot(inv[u], stack(at[u])) for u in units}
    u_loc = {u: _bdot(inv[u], stack(ak_v[u])) for u in units}
    m_c = {u: jnp.where(same_head, _bdot(w_mat[u], bh[u], _TN), 0.0) + jnp.where(eye2, p_end[u], 0.0)
           for u in units}
    n_c = {u: jnp.where(same_head, _bdot(u_loc[u], bh[u], _TN) + _bdot(vv[u], kh[u], _TN), 0.0)
           for u in units}
    w_st = {u: stack(w_mat[u]) for u in units}
    ul_st = {u: stack(u_loc[u]) for u in units}
    q_c = {u: rt[u] + _bdot(a_rb[u], w_st[u]) for u in units}
    y_loc = {u: _bdot(a_rb[u], ul_st[u]) + _bdot(a_rk[u], v_st[u]) for u in units}

    state = [s_ref[p] for p in range(pairs)]
    y = {}
    for ch in range(chunks):
        for p in range(pairs):
            u = (ch, p)
            y[u] = _bdot(q_c[u], state[p], _NT) + y_loc[u]
            state[p] = _bdot(state[p], m_c[u]) + n_c[u]
    for p in range(pairs):
        s_ref[p] = state[p]

    inv_n = 1.0 / n
    mu = {u: _split_dot(y[u], ones_bd) * inv_n for u in units}
    var = {u: _split_dot(jnp.square(y[u] - mu[u]), ones_bd) * inv_n for u in units}
    for (ch, p) in units:
        u = (ch, p)
        rs, ls = slice(ch * c, (ch + 1) * c), slice(p * pw, (p + 1) * pw)
        yn = (y[u] - mu[u]) * lax.rsqrt(var[u] + RWKV_GN_EPS) * lnx_g[:, ls] + lnx_b[:, ls]
        bonus = _split_dot(r_ref[rs, ls] * kp_ref[rs, ls] * r_k[:, ls], ones_bd) * vv[u]
        o_ref[rs, ls] = ((yn + bonus) * g_ref[rs, ls]).astype(o_ref.dtype)


def _rwkv_scan(r, lw, kp, v, na, b, g, par, *, batch, seq_len):
    c = RWKV_CHUNK
    chunks = RWKV_SCAN_CHUNKS
    rows = chunks * c
    assert seq_len % rows == 0
    to3 = lambda a: a.reshape(batch, seq_len, RWKV_DIM)
    spec = pl.BlockSpec((None, rows, RWKV_DIM), lambda bi, ci: (bi, ci, 0))
    out = pl.pallas_call(
        functools.partial(_rwkv_chunk_kernel, chunks=chunks),
        grid=(batch, seq_len // rows),
        in_specs=[spec] * 7 + [pl.BlockSpec(par.shape, lambda bi, ci: (0, 0))],
        out_specs=spec,
        out_shape=jax.ShapeDtypeStruct((batch, seq_len, RWKV_DIM), BF16),
        scratch_shapes=[pltpu.VMEM((RWKV_HEADS // 2, 2 * RWKV_HEAD_DIM, 2 * RWKV_HEAD_DIM), F32)],
        compiler_params=_params("parallel", "arbitrary"),
        name="rwkv_scan",
    )(to3(r), to3(lw), to3(kp), to3(v), to3(na), to3(b), to3(g), par)
    return out.reshape(batch * seq_len, RWKV_DIM)


def _moba_kernel(q_ref, k_ref, v_ref, pool_ref, o_ref, kmh_ref, kml_ref):
    blk = MOBA_BLOCK
    i = pl.program_id(2)

    @pl.when(i == 0)
    def _():
        km = jnp.dot(pool_ref[...], k_ref[...], preferred_element_type=F32)
        hi = km.astype(BF16)
        kmh_ref[...] = hi
        kml_ref[...] = (km - hi.astype(F32)).astype(BF16)

    q = q_ref[...]
    gate = (lax.dot_general(q, kmh_ref[...], _NT, preferred_element_type=F32)
            + lax.dot_general(q, kml_ref[...], _NT, preferred_element_type=F32))
    lane = lax.broadcasted_iota(jnp.int32, gate.shape, 1)
    past = lane < i
    gate = jnp.where(past, gate, NEG_INF)
    n_blocks = k_ref.shape[0] // blk
    rank = jnp.zeros(gate.shape, jnp.int32)
    for jp in range(n_blocks):
        col = gate[:, jp:jp + 1]
        rank = rank + ((col > gate) | ((col == gate) & (jp < lane))).astype(jnp.int32)
    sel = (past & (rank < MOBA_TOPK)).astype(F32)

    row = lax.broadcasted_iota(jnp.int32, (blk, blk), 0)
    colk = lax.broadcasted_iota(jnp.int32, (blk, blk), 1)
    own = pl.multiple_of(i * blk, blk)
    s = lax.dot_general(q, k_ref[pl.ds(own, blk), :], _NT, preferred_element_type=F32)
    s = jnp.where(colk <= row, s, NEG_INF)
    m = jnp.max(s, axis=-1, keepdims=True)
    p = jnp.exp(s - m)
    l = jnp.sum(p, axis=-1, keepdims=True)
    acc = jnp.dot(p.astype(BF16), v_ref[pl.ds(own, blk), :], preferred_element_type=F32)

    def body(j, carry):
        m, l, acc = carry
        start = pl.multiple_of(j * blk, blk)
        s = lax.dot_general(q, k_ref[pl.ds(start, blk), :], _NT, preferred_element_type=F32)
        chosen = jnp.sum(jnp.where(lane == j, sel, 0.0), axis=-1, keepdims=True)
        s = jnp.where(chosen > 0.0, s, NEG_INF)
        m_new = jnp.maximum(m, jnp.max(s, axis=-1, keepdims=True))
        alpha = jnp.exp(m - m_new)
        p = jnp.exp(s - m_new)
        l = alpha * l + jnp.sum(p, axis=-1, keepdims=True)
        acc = alpha * acc + jnp.dot(p.astype(BF16), v_ref[pl.ds(start, blk), :],
                                    preferred_element_type=F32)
        return m_new, l, acc

    m, l, acc = lax.fori_loop(0, i, body, (m, l, acc))
    o_ref[...] = (acc / l).astype(o_ref.dtype)


def _moba(qkv, *, batch, seq_len):
    blk, dh, nh = MOBA_BLOCK, MOBA_HEAD_DIM, MOBA_HEADS
    assert seq_len % blk == 0 and seq_len // blk <= LANES
    nb = seq_len // blk
    pool = (jnp.arange(LANES)[:, None] == (jnp.arange(seq_len) // blk)[None, :]).astype(BF16) / blk
    qkv3 = qkv.reshape(batch, seq_len, 3 * MOBA_WIDTH)
    out = pl.pallas_call(
        _moba_kernel,
        grid=(batch, nh, nb),
        in_specs=[
            pl.BlockSpec((None, blk, dh), lambda b, h, i: (b, i, h)),
            pl.BlockSpec((None, seq_len, dh), lambda b, h, i: (b, 0, nh + h)),
            pl.BlockSpec((None, seq_len, dh), lambda b, h, i: (b, 0, 2 * nh + h)),
            pl.BlockSpec((LANES, seq_len), lambda b, h, i: (0, 0)),
        ],
        out_specs=pl.BlockSpec((None, blk, dh), lambda b, h, i: (b, i, h)),
        out_shape=jax.ShapeDtypeStruct((batch, seq_len, MOBA_WIDTH), BF16),
        scratch_shapes=[pltpu.VMEM((LANES, dh), BF16), pltpu.VMEM((LANES, dh), BF16)],
        compiler_params=_params("parallel", "parallel", "arbitrary"),
        name="moba",
    )(qkv3, qkv3, qkv3, pool)
    return out.reshape(batch * seq_len, MOBA_WIDTH)


def _dilated_kernel(q_ref, k_ref, kp_ref, v_ref, vp_ref, o_ref, lse_ref):
    w, dh = DIL_WINDOW, DIL_HEAD_DIM
    tiles = q_ref.shape[0] // w
    has_prev = pl.program_id(2) > 0
    row = lax.broadcasted_iota(jnp.int32, (w, w), 0)
    col = lax.broadcasted_iota(jnp.int32, (w, w), 1)
    own_ok = col <= row
    prev_ok = col >= row
    eye = col == row
    for h in range(DIL_HEADS):
        hs = slice(h * dh, (h + 1) * dh)
        for t in range(tiles):
            ts = slice(t * w, (t + 1) * w)
            q = q_ref[ts, hs]
            if t == 0:
                k_prev, v_prev, ok = kp_ref[:, hs], vp_ref[:, hs], prev_ok & has_prev
            else:
                ps = slice((t - 1) * w, t * w)
                k_prev, v_prev, ok = k_ref[ps, hs], v_ref[ps, hs], prev_ok
            s_own = jnp.where(own_ok, lax.dot_general(q, k_ref[ts, hs], _NT, preferred_element_type=F32), NEG_INF)
            s_prev = jnp.where(ok, lax.dot_general(q, k_prev, _NT, preferred_element_type=F32), NEG_INF)
            m = jnp.maximum(jnp.max(s_own, axis=-1, keepdims=True), jnp.max(s_prev, axis=-1, keepdims=True))
            p_own = jnp.exp(s_own - m)
            p_prev = jnp.exp(s_prev - m)
            l = jnp.sum(p_own, axis=-1, keepdims=True) + jnp.sum(p_prev, axis=-1, keepdims=True)
            o = (jnp.dot(p_own.astype(BF16), v_ref[ts, hs], preferred_element_type=F32)
                 + jnp.dot(p_prev.astype(BF16), v_prev, preferred_element_type=F32)) / l
            o_ref[ts, hs] = o.astype(o_ref.dtype)
            lse = m + jnp.log(l)
            lse_ref[t, h:h + 1, :] = jnp.sum(jnp.where(eye, lse, 0.0), axis=0, keepdims=True)


def _dilated_group(qkv, group, dilation, *, batch, seq_len):
    d, w = dilation, DIL_WINDOW
    length = seq_len // d
    rows = min(DIL_ROWS, length)
    assert length % rows == 0 and rows % w == 0
    tiles = rows // w
    per_res = C_IN_COLS // C_WIDTH
    view = qkv.reshape(batch, length, d * C_IN_COLS)
    kcol, vcol = DIL_GROUPS, DIL_GROUPS + 1
    blk = lambda cb: pl.BlockSpec((None, rows, C_WIDTH), lambda b, r, i: (b, i, r * per_res + cb))
    halo = lambda cb: pl.BlockSpec(
        (None, w, C_WIDTH), lambda b, r, i: (b, jnp.maximum(i * tiles - 1, 0), r * per_res + cb))
    o, lse = pl.pallas_call(
        _dilated_kernel,
        grid=(batch, d, length // rows),
        in_specs=[blk(group), blk(kcol), halo(kcol), blk(vcol), halo(vcol)],
        out_specs=[
            pl.BlockSpec((None, rows, C_WIDTH), lambda b, r, i: (b, i, r)),
            pl.BlockSpec((None, None, tiles, DIL_HEADS, w), lambda b, r, i: (b, r, i, 0, 0)),
        ],
        out_shape=[
            jax.ShapeDtypeStruct((batch, length, d * C_WIDTH), BF16),
            jax.ShapeDtypeStruct((batch, d, length // w, DIL_HEADS, w), F32),
        ],
        compiler_params=_params("parallel", "parallel", "arbitrary"),
        name=f"dilated_d{d}",
    )(view, view, view, view, view)
    lse = lse.transpose(0, 2, 4, 1, 3).reshape(batch * seq_len, DIL_HEADS)
    return o.reshape(batch * seq_len, C_WIDTH), lse


def _outproj_ln_kernel(*refs, n_act, merge):
    acts = refs[:n_act]
    pos = n_act
    if merge:
        lse_refs = refs[pos:pos + n_act]
        pos += n_act
    w_ref, x_ref, g_ref, b_ref, o_ref = refs[pos:pos + 5]
    if merge:
        nh, dh = DIL_HEADS, DIL_HEAD_DIM
        parts = [ref[...] for ref in lse_refs]
        m = functools.reduce(jnp.maximum, parts)
        es = [jnp.exp(p - m) for p in parts]
        den = functools.reduce(jnp.add, es)
        wts = [e / den for e in es]
        heads = []
        for h in range(nh):
            hs = slice(h * dh, (h + 1) * dh)
            heads.append(functools.reduce(
                jnp.add, [wts[g][:, h:h + 1] * acts[g][:, hs].astype(F32) for g in range(n_act)]))
        hcat = jnp.concatenate(heads, axis=1)
        proj = _bdot(hcat, w_ref[...])
    else:
        proj = None
        off = 0
        for a_ref in acts:
            kw = a_ref.shape[1]
            part = jnp.dot(a_ref[...], w_ref[off:off + kw, :], preferred_element_type=F32)
            proj = part if proj is None else proj + part
            off += kw
    y = DEEPNORM_ALPHA * x_ref[...] + proj
    o_ref[...] = _layer_norm(y, g_ref[...], b_ref[...])


def _outproj_ln(acts, w, x2d, g, b, lses=()):
    m, dm = x2d.shape
    tm = 512
    assert m % tm == 0 and len(lses) in (0, len(acts))
    rowspec = lambda a: pl.BlockSpec((tm, a.shape[1]), lambda i: (i, 0))
    full = lambda a: pl.BlockSpec(a.shape, lambda i: (0, 0))
    args = list(acts) + list(lses) + [w, x2d, g, b]
    in_specs = ([rowspec(a) for a in acts] + [rowspec(a) for a in lses]
                + [full(w), rowspec(x2d), full(g), full(b)])
    return pl.pallas_call(
        functools.partial(_outproj_ln_kernel, n_act=len(acts), merge=bool(lses)),
        grid=(m // tm,),
        in_specs=in_specs,
        out_specs=pl.BlockSpec((tm, dm), lambda i: (i, 0)),
        out_shape=jax.ShapeDtypeStruct((m, dm), F32),
        compiler_params=_params("parallel"),
        name="outproj_merge_ln" if lses else "outproj_ln",
    )(*args)


def _mlp_ln_kernel(x_ref, w1_ref, w2_ref, g_ref, b_ref, o_ref, xb_ref, acc_ref):
    j = pl.program_id(1)

    @pl.when(j == 0)
    def _():
        xb_ref[...] = x_ref[...].astype(BF16)
        acc_ref[...] = jnp.zeros_like(acc_ref)

    h = jnp.dot(xb_ref[...], w1_ref[...], preferred_element_type=F32)
    h = jnp.square(jnp.maximum(h, 0.0))
    acc_ref[...] += jnp.dot(h.astype(BF16), w2_ref[...], preferred_element_type=F32)

    @pl.when(j == pl.num_programs(1) - 1)
    def _():
        y = DEEPNORM_ALPHA * x_ref[...] + acc_ref[...]
        o_ref[...] = _layer_norm(y, g_ref[...], b_ref[...])


def _mlp_ln(x2d, w1, w2, g, b):
    m, dm = x2d.shape
    hidden = w1.shape[1]
    tm, th = 1024, 512
    assert m % tm == 0 and hidden % th == 0
    return pl.pallas_call(
        _mlp_ln_kernel,
        grid=(m // tm, hidden // th),
        in_specs=[
            pl.BlockSpec((tm, dm), lambda i, j: (i, 0)),
            pl.BlockSpec((dm, th), lambda i, j: (0, j)),
            pl.BlockSpec((th, dm), lambda i, j: (j, 0)),
            pl.BlockSpec((1, dm), lambda i, j: (0, 0)),
            pl.BlockSpec((1, dm), lambda i, j: (0, 0)),
        ],
        out_specs=pl.BlockSpec((tm, dm), lambda i, j: (i, 0)),
        out_shape=jax.ShapeDtypeStruct((m, dm), F32),
        scratch_shapes=[pltpu.VMEM((tm, dm), BF16), pltpu.VMEM((tm, dm), F32)],
        compiler_params=_params("parallel", "arbitrary"),
        name="mlp_ln",
    )(x2d, w1, w2, g, b)


def _rwkv_moba_layer(x2d, w_in, shift_mix, w0, w_up, a0, a_up, g_up, k_k, k_a, r_k, lnx_g, lnx_b,
                     w_out, ln_g, ln_b, *, batch, seq_len):
    d = RWKV_DIM
    tabs = _rotary_tables(seq_len, MOBA_HEAD_DIM, MOBA_HEAD_DIM ** -0.5)
    qkv = _proj(x2d, w_in[:, RWKV_PROJ_COLS:].astype(BF16), seq_len=seq_len, out_dtype=BF16,
                tn=MOBA_WIDTH, tabs=tabs, kind_starts=(1, 2))
    y_b = _moba(qkv, batch=batch, seq_len=seq_len)
    pad = RWKV_Z_COLS - RWKV_PROJ_COLS
    w_r = jnp.pad(w_in[:, :RWKV_PROJ_COLS], ((0, 0), (0, pad))).astype(BF16)
    z = _proj(x2d, w_r, seq_len=seq_len, out_dtype=F32, tn=RWKV_Z_COLS // 5)
    mix = jnp.pad(shift_mix, (0, pad))[None, :]
    wl = jnp.zeros((LORA_PAD, 3 * d), F32)
    wl = wl.at[0:DECAY_LORA, 0:d].set(w_up)
    wl = wl.at[DECAY_LORA:DECAY_LORA + ICLR_LORA, d:2 * d].set(a_up)
    wl = wl.at[DECAY_LORA + ICLR_LORA:LORA_COLS, 2 * d:3 * d].set(g_up).astype(BF16)
    zero = jnp.zeros((d,), F32)
    par = jnp.stack([w0, a0, k_k, k_a, zero, zero, zero, zero])
    head = jnp.arange(d) // RWKV_HEAD_DIM
    hsum = (head[:, None] == head[None, :]).astype(BF16)
    r, lw, kp, v, na, b, g = _rwkv_prep(z, mix, wl, par, hsum, seq_len=seq_len)
    par2 = jnp.stack([r_k.reshape(d), lnx_g, lnx_b, zero, zero, zero, zero, zero])
    y_a = _rwkv_scan(r, lw, kp, v, na, b, g, par2, batch=batch, seq_len=seq_len)
    return _outproj_ln([y_a, y_b], w_out.astype(BF16), x2d, ln_g[None, :], ln_b[None, :])


def _dilated_layer(x2d, w_in, w_out, ln_g, ln_b, *, batch, seq_len):
    tabs = _rotary_tables(seq_len, DIL_HEAD_DIM, DIL_HEAD_DIM ** -0.5)
    tn = C_WIDTH // 2
    nq = DIL_GROUPS * C_WIDTH // tn
    qkv = _proj(x2d, w_in.astype(BF16), seq_len=seq_len, out_dtype=BF16, tn=tn, tabs=tabs,
                kind_starts=(nq, nq + C_WIDTH // tn))
    outs, lses = [], []
    for group, (span, dilation) in enumerate(DIL_PAIRS):
        assert span // dilation == DIL_WINDOW
        o, lse = _dilated_group(qkv, group, dilation, batch=batch, seq_len=seq_len)
        outs.append(o)
        lses.append(lse)
    return _outproj_ln(outs, w_out.astype(BF16), x2d, ln_g[None, :], ln_b[None, :], lses=lses)


def kernel(x, ab_w_in, ab_shift_mix, ab_w0, ab_w_up, ab_a0, ab_a_up, ab_g_up, ab_k_k, ab_k_a, ab_r_k,
           ab_lnx_g, ab_lnx_b, ab_w_out, c_w_in, c_w_out, ln1_g, ln1_b, mlp_w1, mlp_w2, ln2_g, ln2_b):
    batch, seq_len, dm = x.shape
    assert dm == D_MODEL
    h = x.reshape(batch * seq_len, dm)
    depth = ln1_g.shape[0]
    for layer in range(depth):
        j = layer // 2
        if layer % 2 == 0:
            h = _rwkv_moba_layer(h, ab_w_in[j], ab_shift_mix[j], ab_w0[j], ab_w_up[j], ab_a0[j], ab_a_up[j],
                                 ab_g_up[j], ab_k_k[j], ab_k_a[j], ab_r_k[j], ab_lnx_g[j], ab_lnx_b[j],
                                 ab_w_out[j], ln1_g[layer], ln1_b[layer], batch=batch, seq_len=seq_len)
        else:
            h = _dilated_layer(h, c_w_in[j], c_w_out[j], ln1_g[layer], ln1_b[layer],
                               batch=batch, seq_len=seq_len)
        h = _mlp_ln(h, mlp_w1[layer].astype(BF16), mlp_w2[layer].astype(BF16),
                    ln2_g[layer][None, :], ln2_b[layer][None, :])
    return h.reshape(batch, seq_len, dm)
```

```python
import functools

import jax
import jax.numpy as jnp
from jax import lax
from jax.experimental import pallas as pl
from jax.experimental.pallas import tpu as pltpu

F32 = jnp.float32
BF16 = jnp.bfloat16

D_MODEL = 1024
DEPTH = 2
DEEPNORM_ALPHA = (2 * DEPTH) ** 0.25
LN_EPS = 1e-5
ROPE_THETA = 500000.0
ROPE_FRACTION = 4
RWKV_HEAD_DIM = 64
RWKV_DIM = D_MODEL // 2
RWKV_HEADS = RWKV_DIM // RWKV_HEAD_DIM
DECAY_LORA = 64
ICLR_LORA = 64
GATE_LORA = 160
RWKV_GN_EPS = 64e-5
RWKV_PROJ_COLS = 3 * RWKV_DIM + DECAY_LORA + ICLR_LORA + GATE_LORA
MOBA_HEAD_DIM = 128
MOBA_WIDTH = D_MODEL - RWKV_DIM
MOBA_HEADS = MOBA_WIDTH // MOBA_HEAD_DIM
MOBA_BLOCK = 256
MOBA_TOPK = 3
DIL_PAIRS = ((128, 1), (512, 4), (2048, 16))
DIL_GROUPS = len(DIL_PAIRS)
DIL_HEAD_DIM = 128
DIL_HEADS = D_MODEL // DIL_HEAD_DIM
C_WIDTH = DIL_HEADS * DIL_HEAD_DIM
C_IN_COLS = DIL_GROUPS * C_WIDTH + 2 * C_WIDTH
MLP_HIDDEN = 4 * D_MODEL
NEG_INF = -1e30

LANES = 128
SUBLANES = 8
VMEM_LIMIT_BYTES = 56 * 1024 * 1024

LORA_COLS = DECAY_LORA + ICLR_LORA + GATE_LORA
LORA_PAD = -(-LORA_COLS // LANES) * LANES
RWKV_Z_COLS = 3 * RWKV_DIM + LORA_PAD
RWKV_CHUNK = 64
RWKV_SCAN_CHUNKS = 4
DIL_WINDOW = 128
DIL_ROWS = 256
DIL_PERM_ROWS = 256
OUTPROJ_ROWS = 512

_NN = (((1,), (0,)), ((), ()))
_NT = (((1,), (1,)), ((), ()))
_TN = (((0,), (0,)), ((), ()))


def _bdot(a, b, dims=_NN):
    return lax.dot_general(a.astype(BF16), b.astype(BF16), dims, preferred_element_type=F32)


def _layer_norm(y, g, b):
    mu = jnp.mean(y, axis=-1, keepdims=True)
    var = jnp.mean(jnp.square(y - mu), axis=-1, keepdims=True)
    return (y - mu) * lax.rsqrt(var + LN_EPS) * g + b


def _params(*sem):
    return pltpu.CompilerParams(dimension_semantics=sem, vmem_limit_bytes=VMEM_LIMIT_BYTES)


def _rotary_tables(seq_len, head_dim, q_scale):
    rot = head_dim // ROPE_FRACTION
    half = rot // 2
    inv_freq = ROPE_THETA ** (-jnp.arange(half, dtype=F32) / half)
    ang = jnp.arange(seq_len).astype(F32)[:, None] * inv_freq[None, :]
    cos, sin = jnp.cos(ang), jnp.sin(ang)
    zeros = lambda n: jnp.zeros((seq_len, n), F32)
    c = jnp.concatenate([cos, cos, jnp.ones((seq_len, head_dim - rot), F32)], axis=1)
    s1 = jnp.concatenate([-sin, zeros(head_dim - half)], axis=1)
    s2 = jnp.concatenate([zeros(half), sin, zeros(head_dim - rot)], axis=1)
    plain = jnp.stack([c, s1, s2])
    ident = jnp.stack([jnp.ones_like(c), jnp.zeros_like(c), jnp.zeros_like(c)])
    return jnp.stack([plain * q_scale, plain, ident])


def _proj_kernel(x_ref, w_ref, *rest, rotary):
    if rotary:
        tab_ref, o_ref, xb_ref = rest
    else:
        o_ref, xb_ref = rest

    @pl.when(pl.program_id(1) == 0)
    def _():
        xb_ref[...] = x_ref[...].astype(BF16)

    acc = jnp.dot(xb_ref[...], w_ref[...], preferred_element_type=F32)
    if not rotary:
        o_ref[...] = acc.astype(o_ref.dtype)
        return
    c, s1, s2 = tab_ref[0], tab_ref[1], tab_ref[2]
    half = LANES // ROPE_FRACTION // 2
    for hb in range(acc.shape[1] // LANES):
        seg = acc[:, hb * LANES:(hb + 1) * LANES]
        seg = seg * c + pltpu.roll(seg, LANES - half, 1) * s1 + pltpu.roll(seg, half, 1) * s2
        o_ref[:, hb * LANES:(hb + 1) * LANES] = seg.astype(o_ref.dtype)


def _proj(x2d, w, *, seq_len, out_dtype, tn, tabs=None, kind_starts=()):
    m, k = x2d.shape
    n = w.shape[1]
    tm = min(1024, seq_len)
    assert m % tm == 0 and seq_len % tm == 0 and n % tn == 0 and tn % LANES == 0
    in_specs = [
        pl.BlockSpec((tm, k), lambda i, j: (i, 0)),
        pl.BlockSpec((k, tn), lambda i, j: (0, j)),
    ]
    args = [x2d, w]
    if tabs is not None:
        blocks_per_seq = seq_len // tm

        def tab_map(i, j):
            kind = sum((j >= s).astype(jnp.int32) for s in kind_starts)
            return (kind, 0, i % blocks_per_seq, 0)

        in_specs.append(pl.BlockSpec((None, 3, tm, LANES), tab_map))
        args.append(tabs)
    return pl.pallas_call(
        functools.partial(_proj_kernel, rotary=tabs is not None),
        grid=(m // tm, n // tn),
        in_specs=in_specs,
        out_specs=pl.BlockSpec((tm, tn), lambda i, j: (i, j)),
        out_shape=jax.ShapeDtypeStruct((m, n), out_dtype),
        scratch_shapes=[pltpu.VMEM((tm, k), BF16)],
        compiler_params=_params("parallel", "arbitrary"),
        name="proj_rotary" if tabs is not None else "proj",
    )(*args)


def _softplus(x):
    return jnp.maximum(x, 0.0) + jnp.log(1.0 + jnp.exp(-jnp.abs(x)))


def _rwkv_prep_kernel(z_ref, prev_ref, mix_ref, wl_ref, par_ref, hsum_ref,
                      r_o, lw_o, kp_o, v_o, na_o, b_o, g_o, *, blocks_per_seq):
    d = RWKV_DIM
    z = z_ref[...]
    first = (pl.program_id(0) % blocks_per_seq) == 0
    prev = jnp.where(first, 0.0, prev_ref[SUBLANES - 1:SUBLANES, :])
    row = lax.broadcasted_iota(jnp.int32, z.shape, 0)
    shifted = jnp.where(row == 0, prev, pltpu.roll(z, 1, 0))
    zz = z + (shifted - z) * mix_ref[...]
    r, k, v, lo = zz[:, 0:d], zz[:, d:2 * d], zz[:, 2 * d:3 * d], zz[:, 3 * d:]
    lane = lax.broadcasted_iota(jnp.int32, lo.shape, 1)
    act = jnp.where(lane < DECAY_LORA, jnp.tanh(lo),
                    jnp.where(lane < DECAY_LORA + ICLR_LORA, lo, jax.nn.sigmoid(lo)))
    lora = _bdot(act, wl_ref[...])
    w0, a0, k_k, k_a = par_ref[0:1, :], par_ref[1:2, :], par_ref[2:3, :], par_ref[3:4, :]
    w = -_softplus(-(w0 + lora[:, 0:d])) - 0.5
    a = jax.nn.sigmoid(a0 + lora[:, d:2 * d])
    kk = k * k_k
    sq = kk * kk
    sq_hi = sq.astype(BF16)
    sq_lo = (sq - sq_hi.astype(F32)).astype(BF16)
    ssq = (jnp.dot(sq_hi, hsum_ref[...], preferred_element_type=F32)
           + jnp.dot(sq_lo, hsum_ref[...], preferred_element_type=F32))
    kk = kk / jnp.maximum(jnp.sqrt(ssq), 1e-12)
    r_o[...] = r
    lw_o[...] = -jnp.exp(w)
    kp_o[...] = k * (1.0 + (a - 1.0) * k_a)
    v_o[...] = v
    na_o[...] = -kk
    b_o[...] = kk * a
    g_o[...] = lora[:, 2 * d:3 * d]


def _rwkv_prep(z, mix, wl, par, hsum, *, seq_len):
    m, zc = z.shape
    tm = min(256, seq_len)
    assert m % tm == 0 and seq_len % tm == 0
    blocks_per_seq = seq_len // tm
    rows8 = tm // SUBLANES
    out = jax.ShapeDtypeStruct((m, RWKV_DIM), F32)
    full = lambda a: pl.BlockSpec(a.shape, lambda i: (0,) * a.ndim)
    return pl.pallas_call(
        functools.partial(_rwkv_prep_kernel, blocks_per_seq=blocks_per_seq),
        grid=(m // tm,),
        in_specs=[
            pl.BlockSpec((tm, zc), lambda i: (i, 0)),
            pl.BlockSpec((SUBLANES, zc), lambda i: (jnp.maximum(i * rows8 - 1, 0), 0)),
            full(mix), full(wl), full(par), full(hsum),
        ],
        out_specs=[pl.BlockSpec((tm, RWKV_DIM), lambda i: (i, 0))] * 7,
        out_shape=[out] * 7,
        compiler_params=_params("parallel"),
        name="rwkv_prep",
    )(z, z, mix, wl, par, hsum)


def _split_dot(x, ones_bd):
    hi = x.astype(BF16)
    lo = (x - hi.astype(F32)).astype(BF16)
    return (jnp.dot(hi, ones_bd, preferred_element_type=F32)
            + jnp.dot(lo, ones_bd, preferred_element_type=F32))


def _rwkv_chunk_kernel(r_ref, lw_ref, kp_ref, v_ref, na_ref, b_ref, g_ref, par_ref, o_ref, s_ref, *, chunks):
    c, n = RWKV_CHUNK, RWKV_HEAD_DIM
    pw = 2 * n
    assert c == n and pw == LANES
    pairs = RWKV_HEADS // 2
    units = [(ch, p) for ch in range(chunks) for p in range(pairs)]

    @pl.when(pl.program_id(1) == 0)
    def _():
        s_ref[...] = jnp.zeros_like(s_ref)

    row = lax.broadcasted_iota(jnp.int32, (c, pw), 0)
    col = lax.broadcasted_iota(jnp.int32, (c, pw), 1) & (n - 1)
    tril_inc = col <= row
    tril_str = col < row
    eye = col == row
    level_masks = [((row >> (lvl + 1)) == (col >> (lvl + 1))) & (((row >> lvl) & 1) == 1) & (((col >> lvl) & 1) == 0)
                   for lvl in range(c.bit_length() - 1)]
    r2 = lax.broadcasted_iota(jnp.int32, (pw, pw), 0)
    c2 = lax.broadcasted_iota(jnp.int32, (pw, pw), 1)
    same_head = (r2 < n) == (c2 < n)
    eye2 = r2 == c2
    ones_bd = jnp.where(same_head, 1.0, 0.0).astype(BF16)

    def stack(x):
        xb = x.astype(BF16)
        return jnp.where(same_head, jnp.concatenate([xb, xb], axis=0), jnp.zeros((), BF16))

    rows = chunks * c
    tr = lax.broadcasted_iota(jnp.int32, (rows, rows), 0)
    tc = lax.broadcasted_iota(jnp.int32, (rows, rows), 1)
    chunk_tril = ((tr >> (c.bit_length() - 1)) == (tc >> (c.bit_length() - 1))) & (tc <= tr)
    lw = lw_ref[...]
    cum = jnp.dot(chunk_tril.astype(F32), lw, precision=lax.Precision.HIGHEST,
                  preferred_element_type=F32)
    r_k, lnx_g, lnx_b = par_ref[0:1, :], par_ref[1:2, :], par_ref[2:3, :]

    at, rt, bt, kt, bh, kh, vv, p_end = {}, {}, {}, {}, {}, {}, {}, {}
    for ch in range(chunks):
        rs = slice(ch * c, (ch + 1) * c)
        cum_c = cum[rs]
        cum_end = cum_c[c - 1:c, :]
        e_inv = jnp.exp(-cum_c)
        e_end = jnp.exp(cum_end - cum_c)
        at_c = na_ref[rs, :] * jnp.exp(cum_c - lw[rs])
        rt_c = r_ref[rs, :] * jnp.exp(cum_c)
        b_c, kp_c = b_ref[rs, :], kp_ref[rs, :]
        bt_c, kt_c, bh_c, kh_c = b_c * e_inv, kp_c * e_inv, b_c * e_end, kp_c * e_end
        pe_c = jnp.exp(cum_end)
        for p in range(pairs):
            ls = slice(p * pw, (p + 1) * pw)
            u = (ch, p)
            at[u], rt[u], bt[u], kt[u], bh[u], kh[u] = at_c[:, ls], rt_c[:, ls], bt_c[:, ls], kt_c[:, ls], bh_c[:, ls], kh_c[:, ls]
            vv[u], p_end[u] = v_ref[rs, ls], pe_c[:, ls]

    ar = {u: jnp.concatenate([at[u], rt[u]], axis=0) for u in units}
    gb = {u: _bdot(ar[u], stack(bt[u]), _NT) for u in units}
    gk = {u: _bdot(ar[u], stack(kt[u]), _NT) for u in units}
    a_ab = {u: jnp.where(tril_str, gb[u][:c], 0.0) for u in units}
    a_rb = {u: jnp.where(tril_inc, gb[u][c:], 0.0) for u in units}
    a_ak = {u: jnp.where(tril_str, gk[u][:c], 0.0) for u in units}
    a_rk = {u: jnp.where(tril_inc, gk[u][c:], 0.0) for u in units}
    v_st = {u: stack(vv[u]) for u in units}
    ak_v = {u: _bdot(a_ak[u], v_st[u]) for u in units}
    inv = {u: jnp.where(eye, 1.0, jnp.where(level_masks[0], a_ab[u], 0.0)) for u in units}
    for mask in level_masks[1:]:
        e = {u: _bdot(jnp.where(mask, a_ab[u], 0.0), stack(inv[u])) for u in units}
        inv = {u: inv[u] + _bdot(inv[u], stack(e[u])) for u in units}
    w_mat = {u: _bdot(inv[u], stack(at[u])) for u in units}
    u_loc = {u: _bdot(inv[u], stack(ak_v[u])) for u in units}
    m_c = {u: jnp.where(same_head, _bdot(w_mat[u], bh[u], _TN), 0.0) + jnp.where(eye2, p_end[u], 0.0)
           for u in units}
    n_c = {u: jnp.where(same_head, _bdot(u_loc[u], bh[u], _TN) + _bdot(vv[u], kh[u], _TN), 0.0)
           for u in units}
    w_st = {u: stack(w_mat[u]) for u in units}
    ul_st = {u: stack(u_loc[u]) for u in units}
    q_c = {u: rt[u] + _bdot(a_rb[u], w_st[u]) for u in units}
    y_loc = {u: _bdot(a_rb[u], ul_st[u]) + _bdot(a_rk[u], v_st[u]) for u in units}

    state = [s_ref[p] for p in range(pairs)]
    y = {}
    for ch in range(chunks):
        for p in range(pairs):
            u = (ch, p)
            y[u] = _bdot(q_c[u], state[p], _NT) + y_loc[u]
            state[p] = _bdot(state[p], m_c[u]) + n_c[u]
    for p in range(pairs):
        s_ref[p] = state[p]

    inv_n = 1.0 / n
    mu = {u: _split_dot(y[u], ones_bd) * inv_n for u in units}
    var = {u: _split_dot(jnp.square(y[u] - mu[u]), ones_bd) * inv_n for u in units}
    for (ch, p) in units:
        u = (ch, p)
        rs, ls = slice(ch * c, (ch + 1) * c), slice(p * pw, (p + 1) * pw)
        yn = (y[u] - mu[u]) * lax.rsqrt(var[u] + RWKV_GN_EPS) * lnx_g[:, ls] + lnx_b[:, ls]
        bonus = _split_dot(r_ref[rs, ls] * kp_ref[rs, ls] * r_k[:, ls], ones_bd) * vv[u]
        o_ref[rs, ls] = ((yn + bonus) * g_ref[rs, ls]).astype(o_ref.dtype)


def _rwkv_scan(r, lw, kp, v, na, b, g, par, *, batch, seq_len):
    c = RWKV_CHUNK
    chunks = RWKV_SCAN_CHUNKS
    rows = chunks * c
    assert seq_len % rows == 0
    to3 = lambda a: a.reshape(batch, seq_len, RWKV_DIM)
    spec = pl.BlockSpec((None, rows, RWKV_DIM), lambda bi, ci: (bi, ci, 0))
    out = pl.pallas_call(
        functools.partial(_rwkv_chunk_kernel, chunks=chunks),
        grid=(batch, seq_len // rows),
        in_specs=[spec] * 7 + [pl.BlockSpec(par.shape, lambda bi, ci: (0, 0))],
        out_specs=spec,
        out_shape=jax.ShapeDtypeStruct((batch, seq_len, RWKV_DIM), BF16),
        scratch_shapes=[pltpu.VMEM((RWKV_HEADS // 2, 2 * RWKV_HEAD_DIM, 2 * RWKV_HEAD_DIM), F32)],
        compiler_params=_params("parallel", "arbitrary"),
        name="rwkv_scan",
    )(to3(r), to3(lw), to3(kp), to3(v), to3(na), to3(b), to3(g), par)
    return out.reshape(batch * seq_len, RWKV_DIM)


def _moba_kernel(q_ref, k_ref, v_ref, pool_ref, o_ref, kext_ref, s_ref):
    blk, dh = MOBA_BLOCK, MOBA_HEAD_DIM
    t_len = k_ref.shape[0]
    nb = t_len // blk
    nbp = -(-nb // SUBLANES) * SUBLANES
    shift = blk.bit_length() - 1
    key_tile = 2 * blk

    lane = lax.broadcasted_iota(jnp.int32, (blk, LANES), 1)
    for j in range(nb):
        rs = slice(j * blk, (j + 1) * blk)
        kext_ref[rs, 0:dh] = k_ref[rs, :]
        kext_ref[rs, dh:dh + LANES] = jnp.where(lane == j, 1.0, 0.0).astype(BF16)

    km = jnp.dot(pool_ref[...], k_ref[...], preferred_element_type=F32)[0:nbp]
    km_hi = km.astype(BF16)
    km_lo = (km - km_hi.astype(F32)).astype(BF16)
    q_all = q_ref[...]
    gate = (lax.dot_general(km_hi, q_all, _NT, preferred_element_type=F32)
            + lax.dot_general(km_lo, q_all, _NT, preferred_element_type=F32))
    bidx = lax.broadcasted_iota(jnp.int32, gate.shape, 0)
    qblk = lax.broadcasted_iota(jnp.int32, gate.shape, 1) >> shift
    past = bidx < qblk
    gate = jnp.where(past, gate, NEG_INF)
    rank = jnp.zeros(gate.shape, jnp.int32)
    for jp in range(nb):
        other = gate[jp:jp + 1, :]
        rank = rank + ((other > gate) | ((other == gate) & (jp < bidx))).astype(jnp.int32)
    allowed = (past & (rank < MOBA_TOPK)) | (bidx == qblk)
    bias_t = jnp.where(allowed, 0.0, NEG_INF).astype(BF16)
    bias_t = jnp.concatenate([bias_t, jnp.zeros((LANES - nbp, t_len), BF16)], axis=0)
    r2 = lax.broadcasted_iota(jnp.int32, (LANES, LANES), 0)
    c2 = lax.broadcasted_iota(jnp.int32, (LANES, LANES), 1)
    eye = jnp.where(r2 == c2, 1.0, 0.0).astype(BF16)
    row = lax.broadcasted_iota(jnp.int32, (blk, blk), 0)
    col = lax.broadcasted_iota(jnp.int32, (blk, blk), 1)
    causal = col <= row
    colmax = lambda x: functools.reduce(jnp.maximum, [x[:, a:a + LANES] for a in range(0, x.shape[1], LANES)])
    colsum = lambda x: functools.reduce(jnp.add, [x[:, a:a + LANES] for a in range(0, x.shape[1], LANES)])

    for i in range(nb):
        qs = slice(i * blk, (i + 1) * blk)
        bias = lax.dot_general(bias_t[:, qs], eye, _TN, preferred_element_type=F32)
        q_ext = jnp.concatenate([q_ref[qs, :], bias.astype(BF16)], axis=1)
        nk = (i + 1) * blk
        tiles = [(a, min(a + key_tile, nk)) for a in range(0, nk, key_tile)]
        m_lane = None
        for a, e in tiles:
            s = lax.dot_general(q_ext, kext_ref[a:e, :], _NT, preferred_element_type=F32)
            if e == nk:
                s_own = jnp.where(causal, s[:, e - a - blk:], NEG_INF)
                s = s_own if e - a == blk else jnp.concatenate([s[:, :e - a - blk], s_own], axis=1)
            s_ref[:, a:e] = s
            part = colmax(s)
            m_lane = part if m_lane is None else jnp.maximum(m_lane, part)
        m = jnp.max(m_lane, axis=-1, keepdims=True)
        l_lane = None
        acc = None
        for a, e in tiles:
            p = jnp.exp(s_ref[:, a:e] - m)
            part = colsum(p)
            l_lane = part if l_lane is None else l_lane + part
            pv = jnp.dot(p.astype(BF16), v_ref[a:e, :], preferred_element_type=F32)
            acc = pv if acc is None else acc + pv
        l = jnp.sum(l_lane, axis=-1, keepdims=True)
        o_ref[qs, :] = (acc / l).astype(o_ref.dtype)


def _moba(qkv, *, batch, seq_len):
    blk, dh, nh = MOBA_BLOCK, MOBA_HEAD_DIM, MOBA_HEADS
    assert seq_len % blk == 0 and seq_len // blk <= LANES and dh == LANES
    pool = (jnp.arange(LANES)[:, None] == (jnp.arange(seq_len) // blk)[None, :]).astype(BF16) / blk
    qkv3 = qkv.reshape(batch, seq_len, 3 * MOBA_WIDTH)
    head = lambda off: pl.BlockSpec((None, seq_len, dh), lambda b, h: (b, 0, off + h))
    out = pl.pallas_call(
        _moba_kernel,
        grid=(batch, nh),
        in_specs=[head(0), head(nh), head(2 * nh), pl.BlockSpec((LANES, seq_len), lambda b, h: (0, 0))],
        out_specs=head(0),
        out_shape=jax.ShapeDtypeStruct((batch, seq_len, MOBA_WIDTH), BF16),
        scratch_shapes=[pltpu.VMEM((seq_len, dh + LANES), BF16), pltpu.VMEM((blk, seq_len), F32)],
        compiler_params=_params("parallel", "parallel"),
        name="moba",
    )(qkv3, qkv3, qkv3, pool)
    return out.reshape(batch * seq_len, MOBA_WIDTH)


def _residue_perm(d):
    p = DIL_PERM_ROWS
    idx = jnp.arange(p)
    nat = (idx % (p // d)) * d + idx // (p // d)
    return (nat[:, None] == idx[None, :]).astype(BF16)


def _deinterleave_kernel(x_ref, p_ref, o_ref):
    g = DIL_PERM_ROWS
    d = o_ref.shape[0]
    w = g // d
    for gi in range(x_ref.shape[0] // g):
        y = jnp.dot(p_ref[...], x_ref[gi * g:(gi + 1) * g, :], preferred_element_type=F32).astype(o_ref.dtype)
        for r in range(d):
            o_ref[r, gi * w:(gi + 1) * w, :] = y[r * w:(r + 1) * w, :]


def _deinterleave(qkv3, group, d):
    batch, seq_len, _ = qkv3.shape
    rows = 2 * DIL_PERM_ROWS
    assert seq_len % rows == 0 and DIL_PERM_ROWS % (d * 2 * SUBLANES) == 0
    kcol = DIL_GROUPS
    return pl.pallas_call(
        _deinterleave_kernel,
        grid=(batch, seq_len // rows, 3),
        in_specs=[
            pl.BlockSpec((None, rows, C_WIDTH), lambda b, i, j: (b, i, jnp.where(j == 0, group, j + kcol - 1))),
            pl.BlockSpec((DIL_PERM_ROWS, DIL_PERM_ROWS), lambda b, i, j: (0, 0)),
        ],
        out_specs=pl.BlockSpec((None, d, rows // d, C_WIDTH), lambda b, i, j: (b, 0, i, j)),
        out_shape=jax.ShapeDtypeStruct((batch, d, seq_len // d, 3 * C_WIDTH), BF16),
        compiler_params=_params("parallel", "parallel", "parallel"),
        name=f"deinterleave_d{d}",
    )(qkv3, _residue_perm(d))


def _dilated_kernel(q_ref, k_ref, kp_ref, v_ref, vp_ref, o_ref, lse_ref):
    w, dh = DIL_WINDOW, DIL_HEAD_DIM
    tiles = q_ref.shape[0] // w
    has_prev = pl.program_id(2) > 0
    row = lax.broadcasted_iota(jnp.int32, (w, 2 * w), 0)
    col = lax.broadcasted_iota(jnp.int32, (w, 2 * w), 1)
    band = (col >= row) & (col <= row + w)
    first_band = band & (has_prev | (col >= w))
    er = lax.broadcasted_iota(jnp.int32, (w, w), 0)
    ec = lax.broadcasted_iota(jnp.int32, (w, w), 1)
    eye = er == ec
    units = [(h, t) for h in range(DIL_HEADS) for t in range(tiles)]

    def banded(ref, halo_ref, h, t):
        hs = slice(h * dh, (h + 1) * dh)
        if t == 0:
            return jnp.concatenate([halo_ref[:, hs], ref[0:w, hs]], axis=0)
        return ref[(t - 1) * w:(t + 1) * w, hs]

    s = {}
    for h, t in units:
        q = q_ref[t * w:(t + 1) * w, h * dh:(h + 1) * dh]
        sc = lax.dot_general(q, banded(k_ref, kp_ref, h, t), _NT, preferred_element_type=F32)
        s[h, t] = jnp.where(first_band if t == 0 else band, sc, NEG_INF)
    m = {u: jnp.max(jnp.maximum(s[u][:, :w], s[u][:, w:]), axis=-1, keepdims=True) for u in units}
    p = {u: jnp.exp(s[u] - m[u]) for u in units}
    l = {u: jnp.sum(p[u][:, :w] + p[u][:, w:], axis=-1, keepdims=True) for u in units}
    o = {(h, t): jnp.dot(p[h, t].astype(BF16), banded(v_ref, vp_ref, h, t), preferred_element_type=F32)
         for h, t in units}
    for h, t in units:
        u = (h, t)
        o_ref[t * w:(t + 1) * w, h * dh:(h + 1) * dh] = (o[u] / l[u]).astype(o_ref.dtype)
        lse = m[u] + jnp.log(l[u])
        lse_ref[t, h:h + 1, :] = jnp.sum(jnp.where(eye, lse, 0.0), axis=0, keepdims=True)


def _dilated_group(src, cols, *, seq_len):
    batch, d, length, _ = src.shape
    w = DIL_WINDOW
    rows = min(DIL_ROWS, length)
    assert length % rows == 0 and rows % w == 0 and length * d == seq_len
    tiles = rows // w
    qc, kc, vc = cols
    blk = lambda cb: pl.BlockSpec((None, None, rows, C_WIDTH), lambda b, r, i: (b, r, i, cb))
    halo = lambda cb: pl.BlockSpec(
        (None, None, w, C_WIDTH), lambda b, r, i: (b, r, jnp.maximum(i * tiles - 1, 0), cb))
    o, lse = pl.pallas_call(
        _dilated_kernel,
        grid=(batch, d, length // rows),
        in_specs=[blk(qc), blk(kc), halo(kc), blk(vc), halo(vc)],
        out_specs=[
            pl.BlockSpec((None, None, rows, C_WIDTH), lambda b, r, i: (b, r, i, 0)),
            pl.BlockSpec((None, None, tiles, DIL_HEADS, w), lambda b, r, i: (b, r, i, 0, 0)),
        ],
        out_shape=[
            jax.ShapeDtypeStruct((batch, d, length, C_WIDTH), BF16),
            jax.ShapeDtypeStruct((batch, d, length // w, DIL_HEADS, w), F32),
        ],
        compiler_params=_params("parallel", "parallel", "arbitrary"),
        name=f"dilated_d{d}",
    )(src, src, src, src, src)
    lse = lse.transpose(0, 2, 4, 1, 3).reshape(batch, seq_len, DIL_HEADS)
    return o, lse


def _outproj_ln_kernel(*refs, n_act):
    acts = refs[:n_act]
    w_ref, x_ref, g_ref, b_ref, o_ref = refs[n_act:]
    proj = None
    off = 0
    for a_ref in acts:
        kw = a_ref.shape[1]
        part = jnp.dot(a_ref[...], w_ref[off:off + kw, :], preferred_element_type=F32)
        proj = part if proj is None else proj + part
        off += kw
    y = DEEPNORM_ALPHA * x_ref[...] + proj
    o_ref[...] = _layer_norm(y, g_ref[...], b_ref[...])


def _outproj_ln(acts, w, x2d, g, b):
    m, dm = x2d.shape
    tm = OUTPROJ_ROWS
    assert m % tm == 0
    rowspec = lambda a: pl.BlockSpec((tm, a.shape[1]), lambda i: (i, 0))
    full = lambda a: pl.BlockSpec(a.shape, lambda i: (0, 0))
    return pl.pallas_call(
        functools.partial(_outproj_ln_kernel, n_act=len(acts)),
        grid=(m // tm,),
        in_specs=[rowspec(a) for a in acts] + [full(w), rowspec(x2d), full(g), full(b)],
        out_specs=pl.BlockSpec((tm, dm), lambda i: (i, 0)),
        out_shape=jax.ShapeDtypeStruct((m, dm), F32),
        compiler_params=_params("parallel"),
        name="outproj_ln",
    )(*acts, w, x2d, g, b)


def _merge_outproj_ln_kernel(*refs, dilations):
    n = len(dilations)
    o_refs, lse_refs, p_refs = refs[:n], refs[n:2 * n], refs[2 * n:3 * n]
    w_ref, x_ref, g_ref, b_ref, out_ref = refs[3 * n:]
    nh, dh, grp = DIL_HEADS, DIL_HEAD_DIM, DIL_PERM_ROWS
    tm = x_ref.shape[0]
    branch = []
    for o_ref, p_ref, d in zip(o_refs, p_refs, dilations):
        if d == 1:
            branch.append(o_ref[0].astype(F32))
            continue
        w = grp // d
        nat = []
        for gi in range(tm // grp):
            perm = jnp.concatenate([o_ref[r, gi * w:(gi + 1) * w, :] for r in range(d)], axis=0)
            nat.append(lax.dot_general(p_ref[...], perm, _TN, preferred_element_type=F32))
        branch.append(jnp.concatenate(nat, axis=0))
    parts = [ref[...] for ref in lse_refs]
    m = functools.reduce(jnp.maximum, parts)
    es = [jnp.exp(p - m) for p in parts]
    den = functools.reduce(jnp.add, es)
    wts = [e / den for e in es]
    heads = []
    for h in range(nh):
        hs = slice(h * dh, (h + 1) * dh)
        heads.append(functools.reduce(jnp.add, [wts[g][:, h:h + 1] * branch[g][:, hs] for g in range(n)]))
    proj = _bdot(jnp.concatenate(heads, axis=1), w_ref[...])
    y = DEEPNORM_ALPHA * x_ref[...] + proj
    out_ref[...] = _layer_norm(y, g_ref[...], b_ref[...])


def _merge_outproj_ln(outs, lses, dilations, w, x2d, g, b, *, batch, seq_len):
    dm = x2d.shape[1]
    tm = OUTPROJ_ROWS
    assert seq_len % tm == 0 and tm % DIL_PERM_ROWS == 0
    x3 = x2d.reshape(batch, seq_len, dm)
    perms = [_residue_perm(d) for d in dilations]
    full = lambda a: pl.BlockSpec(a.shape, lambda bi, i: (0,) * a.ndim)
    rows3 = lambda a: pl.BlockSpec((None, tm, a.shape[2]), lambda bi, i: (bi, i, 0))
    in_specs = ([pl.BlockSpec((None, d, tm // d, C_WIDTH), lambda bi, i: (bi, 0, i, 0)) for d in dilations]
                + [rows3(a) for a in lses] + [full(p) for p in perms]
                + [full(w), rows3(x3), full(g), full(b)])
    out = pl.pallas_call(
        functools.partial(_merge_outproj_ln_kernel, dilations=tuple(dilations)),
        grid=(batch, seq_len // tm),
        in_specs=in_specs,
        out_specs=pl.BlockSpec((None, tm, dm), lambda bi, i: (bi, i, 0)),
        out_shape=jax.ShapeDtypeStruct((batch, seq_len, dm), F32),
        compiler_params=_params("parallel", "parallel"),
        name="merge_outproj_ln",
    )(*outs, *lses, *perms, w, x3, g, b)
    return out.reshape(batch * seq_len, dm)


def _mlp_ln_kernel(x_ref, w1_ref, w2_ref, g_ref, b_ref, o_ref, xb_ref, acc_ref):
    j = pl.program_id(1)

    @pl.when(j == 0)
    def _():
        xb_ref[...] = x_ref[...].astype(BF16)
        acc_ref[...] = jnp.zeros_like(acc_ref)

    h = jnp.dot(xb_ref[...], w1_ref[...], preferred_element_type=F32)
    h = jnp.square(jnp.maximum(h, 0.0))
    acc_ref[...] += jnp.dot(h.astype(BF16), w2_ref[...], preferred_element_type=F32)

    @pl.when(j == pl.num_programs(1) - 1)
    def _():
        y = DEEPNORM_ALPHA * x_ref[...] + acc_ref[...]
        o_ref[...] = _layer_norm(y, g_ref[...], b_ref[...])


def _mlp_ln(x2d, w1, w2, g, b):
    m, dm = x2d.shape
    hidden = w1.shape[1]
    tm, th = 1024, 512
    assert m % tm == 0 and hidden % th == 0
    return pl.pallas_call(
        _mlp_ln_kernel,
        grid=(m // tm, hidden // th),
        in_specs=[
            pl.BlockSpec((tm, dm), lambda i, j: (i, 0)),
            pl.BlockSpec((dm, th), lambda i, j: (0, j)),
            pl.BlockSpec((th, dm), lambda i, j: (j, 0)),
            pl.BlockSpec((1, dm), lambda i, j: (0, 0)),
            pl.BlockSpec((1, dm), lambda i, j: (0, 0)),
        ],
        out_specs=pl.BlockSpec((tm, dm), lambda i, j: (i, 0)),
        out_shape=jax.ShapeDtypeStruct((m, dm), F32),
        scratch_shapes=[pltpu.VMEM((tm, dm), BF16), pltpu.VMEM((tm, dm), F32)],
        compiler_params=_params("parallel", "arbitrary"),
        name="mlp_ln",
    )(x2d, w1, w2, g, b)


def _rwkv_moba_layer(x2d, w_in, shift_mix, w0, w_up, a0, a_up, g_up, k_k, k_a, r_k, lnx_g, lnx_b,
                     w_out, ln_g, ln_b, *, batch, seq_len):
    d = RWKV_DIM
    tabs = _rotary_tables(seq_len, MOBA_HEAD_DIM, MOBA_HEAD_DIM ** -0.5)
    qkv = _proj(x2d, w_in[:, RWKV_PROJ_COLS:].astype(BF16), seq_len=seq_len, out_dtype=BF16,
                tn=MOBA_WIDTH, tabs=tabs, kind_starts=(1, 2))
    y_b = _moba(qkv, batch=batch, seq_len=seq_len)
    pad = RWKV_Z_COLS - RWKV_PROJ_COLS
    w_r = jnp.pad(w_in[:, :RWKV_PROJ_COLS], ((0, 0), (0, pad))).astype(BF16)
    z = _proj(x2d, w_r, seq_len=seq_len, out_dtype=F32, tn=RWKV_Z_COLS // 5)
    mix = jnp.pad(shift_mix, (0, pad))[None, :]
    wl = jnp.zeros((LORA_PAD, 3 * d), F32)
    wl = wl.at[0:DECAY_LORA, 0:d].set(w_up)
    wl = wl.at[DECAY_LORA:DECAY_LORA + ICLR_LORA, d:2 * d].set(a_up)
    wl = wl.at[DECAY_LORA + ICLR_LORA:LORA_COLS, 2 * d:3 * d].set(g_up).astype(BF16)
    zero = jnp.zeros((d,), F32)
    par = jnp.stack([w0, a0, k_k, k_a, zero, zero, zero, zero])
    head = jnp.arange(d) // RWKV_HEAD_DIM
    hsum = (head[:, None] == head[None, :]).astype(BF16)
    r, lw, kp, v, na, b, g = _rwkv_prep(z, mix, wl, par, hsum, seq_len=seq_len)
    par2 = jnp.stack([r_k.reshape(d), lnx_g, lnx_b, zero, zero, zero, zero, zero])
    y_a = _rwkv_scan(r, lw, kp, v, na, b, g, par2, batch=batch, seq_len=seq_len)
    return _outproj_ln([y_a, y_b], w_out.astype(BF16), x2d, ln_g[None, :], ln_b[None, :])


def _dilated_layer(x2d, w_in, w_out, ln_g, ln_b, *, batch, seq_len):
    tabs = _rotary_tables(seq_len, DIL_HEAD_DIM, DIL_HEAD_DIM ** -0.5)
    tn = C_WIDTH // 2
    nq = DIL_GROUPS * C_WIDTH // tn
    qkv = _proj(x2d, w_in.astype(BF16), seq_len=seq_len, out_dtype=BF16, tn=tn, tabs=tabs,
                kind_starts=(nq, nq + C_WIDTH // tn))
    qkv3 = qkv.reshape(batch, seq_len, C_IN_COLS)
    kcol = DIL_GROUPS
    outs, lses, dilations = [], [], []
    for group, (span, dilation) in enumerate(DIL_PAIRS):
        assert span // dilation == DIL_WINDOW
        if dilation == 1:
            src, cols = qkv3[:, None], (group, kcol, kcol + 1)
        else:
            src, cols = _deinterleave(qkv3, group, dilation), (0, 1, 2)
        o, lse = _dilated_group(src, cols, seq_len=seq_len)
        outs.append(o)
        lses.append(lse)
        dilations.append(dilation)
    return _merge_outproj_ln(outs, lses, dilations, w_out.astype(BF16), x2d, ln_g[None, :], ln_b[None, :],
                             batch=batch, seq_len=seq_len)


def kernel(x, ab_w_in, ab_shift_mix, ab_w0, ab_w_up, ab_a0, ab_a_up, ab_g_up, ab_k_k, ab_k_a, ab_r_k,
           ab_lnx_g, ab_lnx_b, ab_w_out, c_w_in, c_w_out, ln1_g, ln1_b, mlp_w1, mlp_w2, ln2_g, ln2_b):
    batch, seq_len, dm = x.shape
    assert dm == D_MODEL
    h = x.reshape(batch * seq_len, dm)
    depth = ln1_g.shape[0]
    for layer in range(depth):
        j = layer // 2
        if layer % 2 == 0:
            h = _rwkv_moba_layer(h, ab_w_in[j], ab_shift_mix[j], ab_w0[j], ab_w_up[j], ab_a0[j], ab_a_up[j],
                                 ab_g_up[j], ab_k_k[j], ab_k_a[j], ab_r_k[j], ab_lnx_g[j], ab_lnx_b[j],
                                 ab_w_out[j], ln1_g[layer], ln1_b[layer], batch=batch, seq_len=seq_len)
        else:
            h = _dilated_layer(h, c_w_in[j], c_w_out[j], ln1_g[layer], ln1_b[layer],
                               batch=batch, seq_len=seq_len)
        h = _mlp_ln(h, mlp_w1[layer].astype(BF16), mlp_w2[layer].astype(BF16),
                    ln2_g[layer][None, :], ln2_b[layer][None, :])
    return h.reshape(batch, seq_len, dm)
```

```python
import functools

import jax
import jax.numpy as jnp
from jax import lax
from jax.experimental import pallas as pl
from jax.experimental.pallas import tpu as pltpu

F32 = jnp.float32
BF16 = jnp.bfloat16

D_MODEL = 1024
DEPTH = 2
DEEPNORM_ALPHA = (2 * DEPTH) ** 0.25
LN_EPS = 1e-5
ROPE_THETA = 500000.0
ROPE_FRACTION = 4
RWKV_HEAD_DIM = 64
RWKV_DIM = D_MODEL // 2
RWKV_HEADS = RWKV_DIM // RWKV_HEAD_DIM
DECAY_LORA = 64
ICLR_LORA = 64
GATE_LORA = 160
RWKV_GN_EPS = 64e-5
RWKV_PROJ_COLS = 3 * RWKV_DIM + DECAY_LORA + ICLR_LORA + GATE_LORA
MOBA_HEAD_DIM = 128
MOBA_WIDTH = D_MODEL - RWKV_DIM
MOBA_HEADS = MOBA_WIDTH // MOBA_HEAD_DIM
MOBA_BLOCK = 256
MOBA_TOPK = 3
DIL_PAIRS = ((128, 1), (512, 4), (2048, 16))
DIL_GROUPS = len(DIL_PAIRS)
DIL_HEAD_DIM = 128
DIL_HEADS = D_MODEL // DIL_HEAD_DIM
C_WIDTH = DIL_HEADS * DIL_HEAD_DIM
C_IN_COLS = DIL_GROUPS * C_WIDTH + 2 * C_WIDTH
MLP_HIDDEN = 4 * D_MODEL
NEG_INF = -1e30

LANES = 128
SUBLANES = 8
VMEM_LIMIT_BYTES = 56 * 1024 * 1024

LORA_COLS = DECAY_LORA + ICLR_LORA + GATE_LORA
LORA_PAD = -(-LORA_COLS // LANES) * LANES
RWKV_Z_COLS = 3 * RWKV_DIM + LORA_PAD
RWKV_CHUNK = 64
RWKV_SCAN_CHUNKS = 4
DIL_WINDOW = 128
DIL_ROWS = 256
DIL_PERM_ROWS = 256
OUTPROJ_ROWS = 512
PROJ_ROW_CHUNK = 256
MLP_ROWS = 512
MLP_HIDDEN_CHUNK = 1024

_NN = (((1,), (0,)), ((), ()))
_NT = (((1,), (1,)), ((), ()))
_TN = (((0,), (0,)), ((), ()))


def _bdot(a, b, dims=_NN):
    return lax.dot_general(a.astype(BF16), b.astype(BF16), dims, preferred_element_type=F32)


def _layer_norm(y, g, b):
    mu = jnp.mean(y, axis=-1, keepdims=True)
    var = jnp.mean(jnp.square(y - mu), axis=-1, keepdims=True)
    return (y - mu) * lax.rsqrt(var + LN_EPS) * g + b


def _params(*sem):
    return pltpu.CompilerParams(dimension_semantics=sem, vmem_limit_bytes=VMEM_LIMIT_BYTES)


ROT_DIMS = MOBA_HEAD_DIM // ROPE_FRACTION
ROT_HALF = ROT_DIMS // 2


def _rotary_weight(w, n_heads):
    rest_a = LANES // 2 - ROT_HALF
    heads = []
    for h in range(n_heads):
        head = w[:, h * LANES:(h + 1) * LANES]
        rot, rest = head[:, :ROT_DIMS], head[:, ROT_DIMS:]
        heads += [rot[:, :ROT_HALF], rest[:, :rest_a], rot[:, ROT_HALF:], rest[:, rest_a:]]
    return jnp.concatenate(heads, axis=1)


def _rotary_tables(seq_len, q_scale):
    inv_freq = ROPE_THETA ** (-jnp.arange(ROT_HALF, dtype=F32) / ROT_HALF)
    ang = jnp.arange(seq_len).astype(F32)[:, None] * inv_freq[None, :]
    cos, sin = jnp.cos(ang), jnp.sin(ang)
    pad = lambda a, fill: jnp.concatenate([a, jnp.full((seq_len, LANES // 2 - ROT_HALF), fill, F32)], axis=1)
    c = jnp.concatenate([pad(cos, 1.0), pad(cos, 1.0)], axis=1)
    s = jnp.concatenate([pad(-sin, 0.0), pad(sin, 0.0)], axis=1)
    one = jnp.ones_like(c)
    return jnp.stack([jnp.stack([c * q_scale, s * q_scale]), jnp.stack([c, s]),
                      jnp.stack([one, jnp.zeros_like(c)])])


def _proj_kernel(x_ref, w_ref, *rest, rotary):
    if rotary:
        tab_ref, o_ref, xb_ref = rest
    else:
        o_ref, xb_ref = rest

    @pl.when(pl.program_id(1) == 0)
    def _():
        xb_ref[...] = x_ref[...].astype(BF16)

    tm = x_ref.shape[0]
    chunk = min(PROJ_ROW_CHUNK, tm)
    for r0 in range(0, tm, chunk):
        rs = slice(r0, r0 + chunk)
        acc = jnp.dot(xb_ref[rs, :], w_ref[...], preferred_element_type=F32)
        if not rotary:
            o_ref[rs, :] = acc.astype(o_ref.dtype)
            continue
        t_z, t_partner = tab_ref[0, rs, :], tab_ref[1, rs, :]
        for hb in range(acc.shape[1] // LANES):
            z = acc[:, hb * LANES:(hb + 1) * LANES]
            o_ref[rs, hb * LANES:(hb + 1) * LANES] = (
                z * t_z + pltpu.roll(z, LANES // 2, 1) * t_partner).astype(o_ref.dtype)


def _proj(x2d, w, *, seq_len, out_dtype, tn, tabs=None, kind_starts=()):
    m, k = x2d.shape
    tm = min(1024, seq_len)
    n_blocks = w.shape[1] // tn
    assert m % tm == 0 and seq_len % tm == 0 and w.shape[1] % tn == 0 and tn % LANES == 0
    in_specs = [
        pl.BlockSpec((tm, k), lambda i, j: (i, 0)),
        pl.BlockSpec((k, tn), lambda i, j: (0, j)),
    ]
    args = [x2d, w]
    if tabs is not None:
        blocks_per_seq = seq_len // tm

        def tab_map(i, j):
            kind = sum((j >= s).astype(jnp.int32) for s in kind_starts)
            return (kind, 0, i % blocks_per_seq, 0)

        in_specs.append(pl.BlockSpec((None, 2, tm, LANES), tab_map))
        args.append(tabs)
    return pl.pallas_call(
        functools.partial(_proj_kernel, rotary=tabs is not None),
        grid=(m // tm, n_blocks),
        in_specs=in_specs,
        out_specs=pl.BlockSpec((tm, tn), lambda i, j: (i, j)),
        out_shape=jax.ShapeDtypeStruct((m, n_blocks * tn), out_dtype),
        scratch_shapes=[pltpu.VMEM((tm, k), BF16)],
        compiler_params=_params("parallel", "arbitrary"),
        name="proj_rotary" if tabs is not None else "proj",
    )(*args)


def _softplus(x):
    return jnp.maximum(x, 0.0) + jnp.log(1.0 + jnp.exp(-jnp.abs(x)))


def _split_dot(x, ones_bd):
    hi = x.astype(BF16)
    lo = (x - hi.astype(F32)).astype(BF16)
    return (jnp.dot(hi, ones_bd, preferred_element_type=F32)
            + jnp.dot(lo, ones_bd, preferred_element_type=F32))


def _rwkv_prep(z, prev, mix, wl, par, ones_bd):
    d = RWKV_DIM
    pw = 2 * RWKV_HEAD_DIM
    row = lax.broadcasted_iota(jnp.int32, z.shape, 0)
    shifted = jnp.where(row == 0, prev, pltpu.roll(z, 1, 0))
    zz = z + (shifted - z) * mix
    r, k, v, lo = zz[:, 0:d], zz[:, d:2 * d], zz[:, 2 * d:3 * d], zz[:, 3 * d:]
    lane = lax.broadcasted_iota(jnp.int32, lo.shape, 1)
    act = jnp.where(lane < DECAY_LORA, jnp.tanh(lo),
                    jnp.where(lane < DECAY_LORA + ICLR_LORA, lo, jax.nn.sigmoid(lo)))
    lora = _bdot(act, wl)
    w0, a0, k_k, k_a = par[0:1, :], par[1:2, :], par[2:3, :], par[3:4, :]
    w = -_softplus(-(w0 + lora[:, 0:d])) - 0.5
    a = jax.nn.sigmoid(a0 + lora[:, d:2 * d])
    kk = k * k_k
    sq = kk * kk
    ssq = jnp.concatenate([_split_dot(sq[:, p0:p0 + pw], ones_bd) for p0 in range(0, d, pw)], axis=1)
    kk = kk / jnp.maximum(jnp.sqrt(ssq), 1e-12)
    return r, -jnp.exp(w), k * (1.0 + (a - 1.0) * k_a), v, -kk, kk * a, lora[:, 2 * d:3 * d]


def _rwkv_chunk_kernel(z_ref, mix_ref, wl_ref, par_ref, o_ref, s_ref, prev_ref, *, chunks):
    c, n = RWKV_CHUNK, RWKV_HEAD_DIM
    pw = 2 * n
    assert c == n and pw == LANES
    pairs = RWKV_HEADS // 2
    units = [(ch, p) for ch in range(chunks) for p in range(pairs)]

    first = pl.program_id(1) == 0

    @pl.when(first)
    def _():
        s_ref[...] = jnp.zeros_like(s_ref)

    row = lax.broadcasted_iota(jnp.int32, (c, pw), 0)
    col = lax.broadcasted_iota(jnp.int32, (c, pw), 1) & (n - 1)
    tril_inc = col <= row
    tril_str = col < row
    eye = col == row
    level_masks = [((row >> (lvl + 1)) == (col >> (lvl + 1))) & (((row >> lvl) & 1) == 1) & (((col >> lvl) & 1) == 0)
                   for lvl in range(c.bit_length() - 1)]
    r2 = lax.broadcasted_iota(jnp.int32, (pw, pw), 0)
    c2 = lax.broadcasted_iota(jnp.int32, (pw, pw), 1)
    same_head = (r2 < n) == (c2 < n)
    eye2 = r2 == c2
    ones_bd = jnp.where(same_head, 1.0, 0.0).astype(BF16)

    def stack(x):
        xb = x.astype(BF16)
        return jnp.where(same_head, jnp.concatenate([xb, xb], axis=0), jnp.zeros((), BF16))

    rows = chunks * c
    z = z_ref[...]
    prev = jnp.where(first, 0.0, prev_ref[SUBLANES - 1:SUBLANES, :])
    r_all, lw, kp_all, v_all, na_all, b_all, g_all = _rwkv_prep(
        z, prev, mix_ref[...], wl_ref[...], par_ref[...], ones_bd)
    prev_ref[...] = z[rows - SUBLANES:, :]

    tr = lax.broadcasted_iota(jnp.int32, (rows, rows), 0)
    tc = lax.broadcasted_iota(jnp.int32, (rows, rows), 1)
    chunk_tril = ((tr >> (c.bit_length() - 1)) == (tc >> (c.bit_length() - 1))) & (tc <= tr)
    cum = jnp.dot(chunk_tril.astype(F32), lw, precision=lax.Precision.HIGHEST,
                  preferred_element_type=F32)
    r_k, lnx_g, lnx_b = par_ref[4:5, :], par_ref[5:6, :], par_ref[6:7, :]

    at, rt, bt, kt, bh, kh, vv, p_end = {}, {}, {}, {}, {}, {}, {}, {}
    for ch in range(chunks):
        rs = slice(ch * c, (ch + 1) * c)
        cum_c = cum[rs]
        cum_end = cum_c[c - 1:c, :]
        e_inv = jnp.exp(-cum_c)
        e_end = jnp.exp(cum_end - cum_c)
        at_c = na_all[rs] * jnp.exp(cum_c - lw[rs])
        rt_c = r_all[rs] * jnp.exp(cum_c)
        b_c, kp_c = b_all[rs], kp_all[rs]
        bt_c, kt_c, bh_c, kh_c = b_c * e_inv, kp_c * e_inv, b_c * e_end, kp_c * e_end
        pe_c = jnp.exp(cum_end)
        for p in range(pairs):
            ls = slice(p * pw, (p + 1) * pw)
            u = (ch, p)
            at[u], rt[u], bt[u], kt[u], bh[u], kh[u] = at_c[:, ls], rt_c[:, ls], bt_c[:, ls], kt_c[:, ls], bh_c[:, ls], kh_c[:, ls]
            vv[u], p_end[u] = v_all[rs, ls], pe_c[:, ls]

    ar = {u: jnp.concatenate([at[u], rt[u]], axis=0) for u in units}
    gb = {u: _bdot(ar[u], stack(bt[u]), _NT) for u in units}
    gk = {u: _bdot(ar[u], stack(kt[u]), _NT) for u in units}
    a_ab = {u: jnp.where(tril_str, gb[u][:c], 0.0) for u in units}
    a_rb = {u: jnp.where(tril_inc, gb[u][c:], 0.0) for u in units}
    a_ak = {u: jnp.where(tril_str, gk[u][:c], 0.0) for u in units}
    a_rk = {u: jnp.where(tril_inc, gk[u][c:], 0.0) for u in units}
    v_st = {u: stack(vv[u]) for u in units}
    ak_v = {u: _bdot(a_ak[u], v_st[u]) for u in units}
    inv = {u: jnp.where(eye, 1.0, jnp.where(level_masks[0], a_ab[u], 0.0)) for u in units}
    for mask in level_masks[1:]:
        e = {u: _bdot(jnp.where(mask, a_ab[u], 0.0), stack(inv[u])) for u in units}
        inv = {u: inv[u] + _bdot(inv[u], stack(e[u])) for u in units}
    w_mat = {u: _bdot(inv[u], stack(at[u])) for u in units}
    u_loc = {u: _bdot(inv[u], stack(ak_v[u])) for u in units}
    m_c = {u: jnp.where(same_head, _bdot(w_mat[u], bh[u], _TN), 0.0) + jnp.where(eye2, p_end[u], 0.0)
           for u in units}
    n_c = {u: jnp.where(same_head, _bdot(u_loc[u], bh[u], _TN) + _bdot(vv[u], kh[u], _TN), 0.0)
           for u in units}
    w_st = {u: stack(w_mat[u]) for u in units}
    ul_st = {u: stack(u_loc[u]) for u in units}
    q_c = {u: rt[u] + _bdot(a_rb[u], w_st[u]) for u in units}
    y_loc = {u: _bdot(a_rb[u], ul_st[u]) + _bdot(a_rk[u], v_st[u]) for u in units}

    state = [s_ref[p] for p in range(pairs)]
    y = {}
    for ch in range(chunks):
        for p in range(pairs):
            u = (ch, p)
            y[u] = _bdot(q_c[u], state[p], _NT) + y_loc[u]
            state[p] = _bdot(state[p], m_c[u]) + n_c[u]
    for p in range(pairs):
        s_ref[p] = state[p]

    inv_n = 1.0 / n
    mu = {u: _split_dot(y[u], ones_bd) * inv_n for u in units}
    var = {u: _split_dot(jnp.square(y[u] - mu[u]), ones_bd) * inv_n for u in units}
    for (ch, p) in units:
        u = (ch, p)
        rs, ls = slice(ch * c, (ch + 1) * c), slice(p * pw, (p + 1) * pw)
        yn = (y[u] - mu[u]) * lax.rsqrt(var[u] + RWKV_GN_EPS) * lnx_g[:, ls] + lnx_b[:, ls]
        bonus = _split_dot(r_all[rs, ls] * kp_all[rs, ls] * r_k[:, ls], ones_bd) * vv[u]
        o_ref[rs, ls] = ((yn + bonus) * g_all[rs, ls]).astype(o_ref.dtype)


def _rwkv_scan(z, mix, wl, par, *, batch, seq_len):
    c = RWKV_CHUNK
    chunks = RWKV_SCAN_CHUNKS
    rows = chunks * c
    zc = z.shape[1]
    assert seq_len % rows == 0
    full = lambda a: pl.BlockSpec(a.shape, lambda bi, ci: (0, 0))
    out = pl.pallas_call(
        functools.partial(_rwkv_chunk_kernel, chunks=chunks),
        grid=(batch, seq_len // rows),
        in_specs=[pl.BlockSpec((None, rows, zc), lambda bi, ci: (bi, ci, 0)), full(mix), full(wl), full(par)],
        out_specs=pl.BlockSpec((None, rows, RWKV_DIM), lambda bi, ci: (bi, ci, 0)),
        out_shape=jax.ShapeDtypeStruct((batch, seq_len, RWKV_DIM), BF16),
        scratch_shapes=[pltpu.VMEM((RWKV_HEADS // 2, 2 * RWKV_HEAD_DIM, 2 * RWKV_HEAD_DIM), F32),
                        pltpu.VMEM((SUBLANES, zc), F32)],
        compiler_params=_params("parallel", "arbitrary"),
        name="rwkv_scan",
    )(z.reshape(batch, seq_len, zc), mix, wl, par)
    return out.reshape(batch * seq_len, RWKV_DIM)


def _moba_kernel(q_ref, k_ref, v_ref, pool_ref, o_ref, kext_ref, s_ref):
    blk, dh = MOBA_BLOCK, MOBA_HEAD_DIM
    t_len = k_ref.shape[0]
    nb = t_len // blk
    nbp = -(-nb // SUBLANES) * SUBLANES
    shift = blk.bit_length() - 1
    key_tile = 2 * blk

    lane = lax.broadcasted_iota(jnp.int32, (blk, LANES), 1)
    for j in range(nb):
        rs = slice(j * blk, (j + 1) * blk)
        kext_ref[rs, 0:dh] = k_ref[rs, :]
        kext_ref[rs, dh:dh + LANES] = jnp.where(lane == j, 1.0, 0.0).astype(BF16)

    km = jnp.dot(pool_ref[...], k_ref[...], preferred_element_type=F32)[0:nbp]
    km_hi = km.astype(BF16)
    km_lo = (km - km_hi.astype(F32)).astype(BF16)
    q_all = q_ref[...]
    gate = (lax.dot_general(km_hi, q_all, _NT, preferred_element_type=F32)
            + lax.dot_general(km_lo, q_all, _NT, preferred_element_type=F32))
    bidx = lax.broadcasted_iota(jnp.int32, gate.shape, 0)
    qblk = lax.broadcasted_iota(jnp.int32, gate.shape, 1) >> shift
    past = bidx < qblk
    gate = jnp.where(past, gate, NEG_INF)
    rank = jnp.zeros(gate.shape, jnp.int32)
    for jp in range(nb):
        other = gate[jp:jp + 1, :]
        rank = rank + ((other > gate) | ((other == gate) & (jp < bidx))).astype(jnp.int32)
    allowed = (past & (rank < MOBA_TOPK)) | (bidx == qblk)
    bias_t = jnp.where(allowed, 0.0, NEG_INF).astype(BF16)
    bias_t = jnp.concatenate([bias_t, jnp.zeros((LANES - nbp, t_len), BF16)], axis=0)
    r2 = lax.broadcasted_iota(jnp.int32, (LANES, LANES), 0)
    c2 = lax.broadcasted_iota(jnp.int32, (LANES, LANES), 1)
    eye = jnp.where(r2 == c2, 1.0, 0.0).astype(BF16)
    row = lax.broadcasted_iota(jnp.int32, (blk, blk), 0)
    col = lax.broadcasted_iota(jnp.int32, (blk, blk), 1)
    causal = col <= row
    colmax = lambda x: functools.reduce(jnp.maximum, [x[:, a:a + LANES] for a in range(0, x.shape[1], LANES)])
    colsum = lambda x: functools.reduce(jnp.add, [x[:, a:a + LANES] for a in range(0, x.shape[1], LANES)])

    for i in range(nb):
        qs = slice(i * blk, (i + 1) * blk)
        bias = lax.dot_general(bias_t[:, qs], eye, _TN, preferred_element_type=F32)
        q_ext = jnp.concatenate([q_ref[qs, :], bias.astype(BF16)], axis=1)
        nk = (i + 1) * blk
        tiles = [(a, min(a + key_tile, nk)) for a in range(0, nk, key_tile)]
        m_lane = None
        for a, e in tiles:
            s = lax.dot_general(q_ext, kext_ref[a:e, :], _NT, preferred_element_type=F32)
            if e == nk:
                s_own = jnp.where(causal, s[:, e - a - blk:], NEG_INF)
                s = s_own if e - a == blk else jnp.concatenate([s[:, :e - a - blk], s_own], axis=1)
            s_ref[:, a:e] = s
            part = colmax(s)
            m_lane = part if m_lane is None else jnp.maximum(m_lane, part)
        m = jnp.max(m_lane, axis=-1, keepdims=True)
        l_lane = None
        acc = None
        for a, e in tiles:
            p = jnp.exp(s_ref[:, a:e] - m)
            part = colsum(p)
            l_lane = part if l_lane is None else l_lane + part
            pv = jnp.dot(p.astype(BF16), v_ref[a:e, :], preferred_element_type=F32)
            acc = pv if acc is None else acc + pv
        l = jnp.sum(l_lane, axis=-1, keepdims=True)
        o_ref[qs, :] = (acc / l).astype(o_ref.dtype)


def _moba(qkv, *, batch, seq_len):
    blk, dh, nh = MOBA_BLOCK, MOBA_HEAD_DIM, MOBA_HEADS
    assert seq_len % blk == 0 and seq_len // blk <= LANES and dh == LANES
    pool = (jnp.arange(LANES)[:, None] == (jnp.arange(seq_len) // blk)[None, :]).astype(BF16) / blk
    qkv3 = qkv.reshape(batch, seq_len, 3 * MOBA_WIDTH)
    head = lambda off: pl.BlockSpec((None, seq_len, dh), lambda b, h: (b, 0, off + h))
    out = pl.pallas_call(
        _moba_kernel,
        grid=(batch, nh),
        in_specs=[head(0), head(nh), head(2 * nh), pl.BlockSpec((LANES, seq_len), lambda b, h: (0, 0))],
        out_specs=head(0),
        out_shape=jax.ShapeDtypeStruct((batch, seq_len, MOBA_WIDTH), BF16),
        scratch_shapes=[pltpu.VMEM((seq_len, dh + LANES), BF16), pltpu.VMEM((blk, seq_len), F32)],
        compiler_params=_params("parallel", "parallel"),
        name="moba",
    )(qkv3, qkv3, qkv3, pool)
    return out.reshape(batch * seq_len, MOBA_WIDTH)


def _residue_perm(d):
    p = DIL_PERM_ROWS
    idx = jnp.arange(p)
    nat = (idx % (p // d)) * d + idx // (p // d)
    return (nat[:, None] == idx[None, :]).astype(BF16)


def _deinterleave_kernel(x_ref, p_ref, o_ref):
    g = DIL_PERM_ROWS
    d = o_ref.shape[0]
    w = g // d
    for gi in range(x_ref.shape[0] // g):
        y = jnp.dot(p_ref[...], x_ref[gi * g:(gi + 1) * g, :], preferred_element_type=F32).astype(o_ref.dtype)
        for r in range(d):
            o_ref[r, gi * w:(gi + 1) * w, :] = y[r * w:(r + 1) * w, :]


def _deinterleave(qkv3, group, d):
    batch, seq_len, _ = qkv3.shape
    rows = 2 * DIL_PERM_ROWS
    assert seq_len % rows == 0 and DIL_PERM_ROWS % (d * 2 * SUBLANES) == 0
    kcol = DIL_GROUPS
    return pl.pallas_call(
        _deinterleave_kernel,
        grid=(batch, seq_len // rows, 3),
        in_specs=[
            pl.BlockSpec((None, rows, C_WIDTH), lambda b, i, j: (b, i, jnp.where(j == 0, group, j + kcol - 1))),
            pl.BlockSpec((DIL_PERM_ROWS, DIL_PERM_ROWS), lambda b, i, j: (0, 0)),
        ],
        out_specs=pl.BlockSpec((None, d, rows // d, C_WIDTH), lambda b, i, j: (b, 0, i, j)),
        out_shape=jax.ShapeDtypeStruct((batch, d, seq_len // d, 3 * C_WIDTH), BF16),
        compiler_params=_params("parallel", "parallel", "parallel"),
        name=f"deinterleave_d{d}",
    )(qkv3, _residue_perm(d))


def _dilated_kernel(q_ref, k_ref, kp_ref, v_ref, vp_ref, o_ref, lse_ref):
    w, dh = DIL_WINDOW, DIL_HEAD_DIM
    tiles = q_ref.shape[0] // w
    has_prev = pl.program_id(2) > 0
    row = lax.broadcasted_iota(jnp.int32, (w, 2 * w), 0)
    col = lax.broadcasted_iota(jnp.int32, (w, 2 * w), 1)
    band = (col >= row) & (col <= row + w)
    first_band = band & (has_prev | (col >= w))
    er = lax.broadcasted_iota(jnp.int32, (w, w), 0)
    ec = lax.broadcasted_iota(jnp.int32, (w, w), 1)
    eye = er == ec
    units = [(h, t) for h in range(DIL_HEADS) for t in range(tiles)]

    def banded(ref, halo_ref, h, t):
        hs = slice(h * dh, (h + 1) * dh)
        if t == 0:
            return jnp.concatenate([halo_ref[:, hs], ref[0:w, hs]], axis=0)
        return ref[(t - 1) * w:(t + 1) * w, hs]

    s = {}
    for h, t in units:
        q = q_ref[t * w:(t + 1) * w, h * dh:(h + 1) * dh]
        sc = lax.dot_general(q, banded(k_ref, kp_ref, h, t), _NT, preferred_element_type=F32)
        s[h, t] = jnp.where(first_band if t == 0 else band, sc, NEG_INF)
    m = {u: jnp.max(jnp.maximum(s[u][:, :w], s[u][:, w:]), axis=-1, keepdims=True) for u in units}
    p = {u: jnp.exp(s[u] - m[u]) for u in units}
    l = {u: jnp.sum(p[u][:, :w] + p[u][:, w:], axis=-1, keepdims=True) for u in units}
    o = {(h, t): jnp.dot(p[h, t].astype(BF16), banded(v_ref, vp_ref, h, t), preferred_element_type=F32)
         for h, t in units}
    for h, t in units:
        u = (h, t)
        o_ref[t * w:(t + 1) * w, h * dh:(h + 1) * dh] = (o[u] / l[u]).astype(o_ref.dtype)
        lse = m[u] + jnp.log(l[u])
        lse_ref[t, h:h + 1, :] = jnp.sum(jnp.where(eye, lse, 0.0), axis=0, keepdims=True)


def _dilated_group(src, cols, *, seq_len):
    batch, d, length, _ = src.shape
    w = DIL_WINDOW
    rows = min(DIL_ROWS, length)
    assert length % rows == 0 and rows % w == 0 and length * d == seq_len
    tiles = rows // w
    qc, kc, vc = cols
    blk = lambda cb: pl.BlockSpec((None, None, rows, C_WIDTH), lambda b, r, i: (b, r, i, cb))
    halo = lambda cb: pl.BlockSpec(
        (None, None, w, C_WIDTH), lambda b, r, i: (b, r, jnp.maximum(i * tiles - 1, 0), cb))
    o, lse = pl.pallas_call(
        _dilated_kernel,
        grid=(batch, d, length // rows),
        in_specs=[blk(qc), blk(kc), halo(kc), blk(vc), halo(vc)],
        out_specs=[
            pl.BlockSpec((None, None, rows, C_WIDTH), lambda b, r, i: (b, r, i, 0)),
            pl.BlockSpec((None, None, tiles, DIL_HEADS, w), lambda b, r, i: (b, r, i, 0, 0)),
        ],
        out_shape=[
            jax.ShapeDtypeStruct((batch, d, length, C_WIDTH), BF16),
            jax.ShapeDtypeStruct((batch, d, length // w, DIL_HEADS, w), F32),
        ],
        compiler_params=_params("parallel", "parallel", "arbitrary"),
        name=f"dilated_d{d}",
    )(src, src, src, src, src)
    lse = lse.transpose(0, 2, 4, 1, 3).reshape(batch, seq_len, DIL_HEADS)
    return o, lse


def _outproj_ln_kernel(*refs, n_act):
    acts = refs[:n_act]
    w_ref, x_ref, g_ref, b_ref, o_ref = refs[n_act:]
    proj = None
    off = 0
    for a_ref in acts:
        kw = a_ref.shape[1]
        part = jnp.dot(a_ref[...], w_ref[off:off + kw, :], preferred_element_type=F32)
        proj = part if proj is None else proj + part
        off += kw
    y = DEEPNORM_ALPHA * x_ref[...] + proj
    o_ref[...] = _layer_norm(y, g_ref[...], b_ref[...])


def _outproj_ln(acts, w, x2d, g, b):
    m, dm = x2d.shape
    tm = OUTPROJ_ROWS
    assert m % tm == 0
    rowspec = lambda a: pl.BlockSpec((tm, a.shape[1]), lambda i: (i, 0))
    full = lambda a: pl.BlockSpec(a.shape, lambda i: (0, 0))
    return pl.pallas_call(
        functools.partial(_outproj_ln_kernel, n_act=len(acts)),
        grid=(m // tm,),
        in_specs=[rowspec(a) for a in acts] + [full(w), rowspec(x2d), full(g), full(b)],
        out_specs=pl.BlockSpec((tm, dm), lambda i: (i, 0)),
        out_shape=jax.ShapeDtypeStruct((m, dm), F32),
        compiler_params=_params("parallel"),
        name="outproj_ln",
    )(*acts, w, x2d, g, b)


def _merge_outproj_ln_kernel(*refs, dilations):
    n = len(dilations)
    o_refs, lse_refs, p_refs = refs[:n], refs[n:2 * n], refs[2 * n:3 * n]
    w_ref, x_ref, g_ref, b_ref, out_ref = refs[3 * n:]
    nh, dh, grp = DIL_HEADS, DIL_HEAD_DIM, DIL_PERM_ROWS
    tm = x_ref.shape[0]
    branch = []
    for o_ref, p_ref, d in zip(o_refs, p_refs, dilations):
        if d == 1:
            branch.append(o_ref[0].astype(F32))
            continue
        w = grp // d
        nat = []
        for gi in range(tm // grp):
            perm = jnp.concatenate([o_ref[r, gi * w:(gi + 1) * w, :] for r in range(d)], axis=0)
            nat.append(lax.dot_general(p_ref[...], perm, _TN, preferred_element_type=F32))
        branch.append(jnp.concatenate(nat, axis=0))
    parts = [ref[...] for ref in lse_refs]
    m = functools.reduce(jnp.maximum, parts)
    es = [jnp.exp(p - m) for p in parts]
    den = functools.reduce(jnp.add, es)
    wts = [e / den for e in es]
    heads = []
    for h in range(nh):
        hs = slice(h * dh, (h + 1) * dh)
        heads.append(functools.reduce(jnp.add, [wts[g][:, h:h + 1] * branch[g][:, hs] for g in range(n)]))
    proj = _bdot(jnp.concatenate(heads, axis=1), w_ref[...])
    y = DEEPNORM_ALPHA * x_ref[...] + proj
    out_ref[...] = _layer_norm(y, g_ref[...], b_ref[...])


def _merge_outproj_ln(outs, lses, dilations, w, x2d, g, b, *, batch, seq_len):
    dm = x2d.shape[1]
    tm = OUTPROJ_ROWS
    assert seq_len % tm == 0 and tm % DIL_PERM_ROWS == 0
    x3 = x2d.reshape(batch, seq_len, dm)
    perms = [_residue_perm(d) for d in dilations]
    full = lambda a: pl.BlockSpec(a.shape, lambda bi, i: (0,) * a.ndim)
    rows3 = lambda a: pl.BlockSpec((None, tm, a.shape[2]), lambda bi, i: (bi, i, 0))
    in_specs = ([pl.BlockSpec((None, d, tm // d, C_WIDTH), lambda bi, i: (bi, 0, i, 0)) for d in dilations]
                + [rows3(a) for a in lses] + [full(p) for p in perms]
                + [full(w), rows3(x3), full(g), full(b)])
    out = pl.pallas_call(
        functools.partial(_merge_outproj_ln_kernel, dilations=tuple(dilations)),
        grid=(batch, seq_len // tm),
        in_specs=in_specs,
        out_specs=pl.BlockSpec((None, tm, dm), lambda bi, i: (bi, i, 0)),
        out_shape=jax.ShapeDtypeStruct((batch, seq_len, dm), F32),
        compiler_params=_params("parallel", "parallel"),
        name="merge_outproj_ln",
    )(*outs, *lses, *perms, w, x3, g, b)
    return out.reshape(batch * seq_len, dm)


def _mlp_ln_kernel(x_ref, w1_ref, w2_ref, g_ref, b_ref, o_ref):
    x = x_ref[...]
    xb = x.astype(BF16)
    hidden = w1_ref.shape[1]
    hs = []
    for c0 in range(0, hidden, MLP_HIDDEN_CHUNK):
        h = jnp.dot(xb, w1_ref[:, c0:c0 + MLP_HIDDEN_CHUNK], preferred_element_type=F32)
        hs.append(jnp.square(jnp.maximum(h, 0.0)).astype(BF16))
    y = jnp.dot(jnp.concatenate(hs, axis=1), w2_ref[...], preferred_element_type=F32)
    o_ref[...] = _layer_norm(DEEPNORM_ALPHA * x + y, g_ref[...], b_ref[...])


def _mlp_ln(x2d, w1, w2, g, b):
    m, dm = x2d.shape
    hidden = w1.shape[1]
    tm = MLP_ROWS
    assert m % tm == 0 and hidden % MLP_HIDDEN_CHUNK == 0
    const = lambda a: pl.BlockSpec(a.shape, lambda i: (0, 0), pipeline_mode=pl.Buffered(1))
    return pl.pallas_call(
        _mlp_ln_kernel,
        grid=(m // tm,),
        in_specs=[pl.BlockSpec((tm, dm), lambda i: (i, 0)), const(w1), const(w2), const(g), const(b)],
        out_specs=pl.BlockSpec((tm, dm), lambda i: (i, 0)),
        out_shape=jax.ShapeDtypeStruct((m, dm), F32),
        compiler_params=_params("parallel"),
        name="mlp_ln",
    )(x2d, w1, w2, g, b)


def _rwkv_moba_layer(x2d, w_in, shift_mix, w0, w_up, a0, a_up, g_up, k_k, k_a, r_k, lnx_g, lnx_b,
                     w_out, ln_g, ln_b, *, batch, seq_len):
    d = RWKV_DIM
    tabs = _rotary_tables(seq_len, MOBA_HEAD_DIM ** -0.5)
    w_m = w_in[:, RWKV_PROJ_COLS:]
    w_m = jnp.concatenate([_rotary_weight(w_m[:, :2 * MOBA_WIDTH], 2 * MOBA_HEADS),
                           w_m[:, 2 * MOBA_WIDTH:]], axis=1)
    qkv = _proj(x2d, w_m.astype(BF16), seq_len=seq_len, out_dtype=BF16, tn=MOBA_WIDTH, tabs=tabs,
                kind_starts=(1, 2))
    y_b = _moba(qkv, batch=batch, seq_len=seq_len)
    pad = RWKV_Z_COLS - RWKV_PROJ_COLS
    w_r = jnp.pad(w_in[:, :RWKV_PROJ_COLS], ((0, 0), (0, pad))).astype(BF16)
    z = _proj(x2d, w_r, seq_len=seq_len, out_dtype=F32, tn=RWKV_Z_COLS // 5)
    mix = jnp.pad(shift_mix, (0, pad))[None, :]
    wl = jnp.zeros((LORA_PAD, 3 * d), F32)
    wl = wl.at[0:DECAY_LORA, 0:d].set(w_up)
    wl = wl.at[DECAY_LORA:DECAY_LORA + ICLR_LORA, d:2 * d].set(a_up)
    wl = wl.at[DECAY_LORA + ICLR_LORA:LORA_COLS, 2 * d:3 * d].set(g_up).astype(BF16)
    par = jnp.stack([w0, a0, k_k, k_a, r_k.reshape(d), lnx_g, lnx_b, jnp.zeros((d,), F32)])
    y_a = _rwkv_scan(z, mix, wl, par, batch=batch, seq_len=seq_len)
    return _outproj_ln([y_a, y_b], w_out.astype(BF16), x2d, ln_g[None, :], ln_b[None, :])


def _dilated_layer(x2d, w_in, w_out, ln_g, ln_b, *, batch, seq_len):
    tabs = _rotary_tables(seq_len, DIL_HEAD_DIM ** -0.5)
    nq = DIL_GROUPS * C_WIDTH
    tn = C_WIDTH // 2
    w_c = jnp.concatenate([_rotary_weight(w_in[:, :nq + C_WIDTH], (DIL_GROUPS + 1) * DIL_HEADS),
                           w_in[:, nq + C_WIDTH:]], axis=1)
    qkv = _proj(x2d, w_c.astype(BF16), seq_len=seq_len, out_dtype=BF16, tn=tn, tabs=tabs,
                kind_starts=(nq // tn, (nq + C_WIDTH) // tn))
    qkv3 = qkv.reshape(batch, seq_len, C_IN_COLS)
    kcol = DIL_GROUPS
    outs, lses, dilations = [], [], []
    for group, (span, dilation) in enumerate(DIL_PAIRS):
        assert span // dilation == DIL_WINDOW
        if dilation == 1:
            src, cols = qkv3[:, None], (group, kcol, kcol + 1)
        else:
            src, cols = _deinterleave(qkv3, group, dilation), (0, 1, 2)
        o, lse = _dilated_group(src, cols, seq_len=seq_len)
        outs.append(o)
        lses.append(lse)
        dilations.append(dilation)
    return _merge_outproj_ln(outs, lses, dilations, w_out.astype(BF16), x2d, ln_g[None, :], ln_b[None, :],
                             batch=batch, seq_len=seq_len)


def kernel(x, ab_w_in, ab_shift_mix, ab_w0, ab_w_up, ab_a0, ab_a_up, ab_g_up, ab_k_k, ab_k_a, ab_r_k,
           ab_lnx_g, ab_lnx_b, ab_w_out, c_w_in, c_w_out, ln1_g, ln1_b, mlp_w1, mlp_w2, ln2_g, ln2_b):
    batch, seq_len, dm = x.shape
    assert dm == D_MODEL
    h = x.reshape(batch * seq_len, dm)
    depth = ln1_g.shape[0]
    for layer in range(depth):
        j = layer // 2
        if layer % 2 == 0:
            h = _rwkv_moba_layer(h, ab_w_in[j], ab_shift_mix[j], ab_w0[j], ab_w_up[j], ab_a0[j], ab_a_up[j],
                                 ab_g_up[j], ab_k_k[j], ab_k_a[j], ab_r_k[j], ab_lnx_g[j], ab_lnx_b[j],
                                 ab_w_out[j], ln1_g[layer], ln1_b[layer], batch=batch, seq_len=seq_len)
        else:
            h = _dilated_layer(h, c_w_in[j], c_w_out[j], ln1_g[layer], ln1_b[layer],
                               batch=batch, seq_len=seq_len)
        h = _mlp_ln(h, mlp_w1[layer].astype(BF16), mlp_w2[layer].astype(BF16),
                    ln2_g[layer][None, :], ln2_b[layer][None, :])
    return h.reshape(batch, seq_len, dm)
```

```python
import functools

import jax
import jax.numpy as jnp
from jax import lax
from jax.experimental import pallas as pl
from jax.experimental.pallas import tpu as pltpu

F32 = jnp.float32
BF16 = jnp.bfloat16

D_MODEL = 1024
DEPTH = 2
DEEPNORM_ALPHA = (2 * DEPTH) ** 0.25
LN_EPS = 1e-5
ROPE_THETA = 500000.0
ROPE_FRACTION = 4
RWKV_HEAD_DIM = 64
RWKV_DIM = D_MODEL // 2
RWKV_HEADS = RWKV_DIM // RWKV_HEAD_DIM
DECAY_LORA = 64
ICLR_LORA = 64
GATE_LORA = 160
RWKV_GN_EPS = 64e-5
RWKV_PROJ_COLS = 3 * RWKV_DIM + DECAY_LORA + ICLR_LORA + GATE_LORA
MOBA_HEAD_DIM = 128
MOBA_WIDTH = D_MODEL - RWKV_DIM
MOBA_HEADS = MOBA_WIDTH // MOBA_HEAD_DIM
MOBA_BLOCK = 256
MOBA_TOPK = 3
DIL_PAIRS = ((128, 1), (512, 4), (2048, 16))
DIL_GROUPS = len(DIL_PAIRS)
DIL_HEAD_DIM = 128
DIL_HEADS = D_MODEL // DIL_HEAD_DIM
C_WIDTH = DIL_HEADS * DIL_HEAD_DIM
C_IN_COLS = DIL_GROUPS * C_WIDTH + 2 * C_WIDTH
MLP_HIDDEN = 4 * D_MODEL
NEG_INF = -1e30

LANES = 128
SUBLANES = 8
VMEM_LIMIT_BYTES = 56 * 1024 * 1024

LORA_COLS = DECAY_LORA + ICLR_LORA + GATE_LORA
LORA_PAD = -(-LORA_COLS // LANES) * LANES
RWKV_Z_COLS = 3 * RWKV_DIM + LORA_PAD
RWKV_CHUNK = 64
RWKV_SCAN_CHUNKS = 4
DIL_WINDOW = 128
DIL_ROWS = 256
DIL_PERM_ROWS = 256
OUTPROJ_ROWS = 512
PROJ_ROW_CHUNK = 256
MLP_ROWS = 512
MLP_HIDDEN_CHUNK = 1024

_NN = (((1,), (0,)), ((), ()))
_NT = (((1,), (1,)), ((), ()))
_TN = (((0,), (0,)), ((), ()))


def _bdot(a, b, dims=_NN):
    return lax.dot_general(a.astype(BF16), b.astype(BF16), dims, preferred_element_type=F32)


def _layer_norm(y, g, b):
    mu = jnp.mean(y, axis=-1, keepdims=True)
    var = jnp.mean(jnp.square(y - mu), axis=-1, keepdims=True)
    return (y - mu) * lax.rsqrt(var + LN_EPS) * g + b


def _params(*sem):
    return pltpu.CompilerParams(dimension_semantics=sem, vmem_limit_bytes=VMEM_LIMIT_BYTES)


ROT_DIMS = MOBA_HEAD_DIM // ROPE_FRACTION
ROT_HALF = ROT_DIMS // 2


def _rotary_weight(w, n_heads):
    rest_a = LANES // 2 - ROT_HALF
    heads = []
    for h in range(n_heads):
        head = w[:, h * LANES:(h + 1) * LANES]
        rot, rest = head[:, :ROT_DIMS], head[:, ROT_DIMS:]
        heads += [rot[:, :ROT_HALF], rest[:, :rest_a], rot[:, ROT_HALF:], rest[:, rest_a:]]
    return jnp.concatenate(heads, axis=1)


def _rotary_tables(seq_len, q_scale):
    inv_freq = ROPE_THETA ** (-jnp.arange(ROT_HALF, dtype=F32) / ROT_HALF)
    ang = jnp.arange(seq_len).astype(F32)[:, None] * inv_freq[None, :]
    cos, sin = jnp.cos(ang), jnp.sin(ang)
    pad = lambda a, fill: jnp.concatenate([a, jnp.full((seq_len, LANES // 2 - ROT_HALF), fill, F32)], axis=1)
    c = jnp.concatenate([pad(cos, 1.0), pad(cos, 1.0)], axis=1)
    s = jnp.concatenate([pad(-sin, 0.0), pad(sin, 0.0)], axis=1)
    one = jnp.ones_like(c)
    return jnp.stack([jnp.stack([c * q_scale, s * q_scale]), jnp.stack([c, s]),
                      jnp.stack([one, jnp.zeros_like(c)])])


def _proj_kernel(x_ref, w_ref, *rest, rotary):
    if rotary:
        tab_ref, o_ref, xb_ref = rest
    else:
        o_ref, xb_ref = rest

    @pl.when(pl.program_id(1) == 0)
    def _():
        xb_ref[...] = x_ref[...].astype(BF16)

    tm = x_ref.shape[0]
    chunk = min(PROJ_ROW_CHUNK, tm)
    for r0 in range(0, tm, chunk):
        rs = slice(r0, r0 + chunk)
        acc = jnp.dot(xb_ref[rs, :], w_ref[...], preferred_element_type=F32)
        if not rotary:
            o_ref[rs, :] = acc.astype(o_ref.dtype)
            continue
        t_z, t_partner = tab_ref[0, rs, :], tab_ref[1, rs, :]
        for hb in range(acc.shape[1] // LANES):
            z = acc[:, hb * LANES:(hb + 1) * LANES]
            o_ref[rs, hb * LANES:(hb + 1) * LANES] = (
                z * t_z + pltpu.roll(z, LANES // 2, 1) * t_partner).astype(o_ref.dtype)


def _proj(x2d, w, *, seq_len, out_dtype, tn, tabs=None, kind_starts=()):
    m, k = x2d.shape
    tm = min(1024, seq_len)
    n_blocks = w.shape[1] // tn
    assert m % tm == 0 and seq_len % tm == 0 and w.shape[1] % tn == 0 and tn % LANES == 0
    in_specs = [
        pl.BlockSpec((tm, k), lambda i, j: (i, 0)),
        pl.BlockSpec((k, tn), lambda i, j: (0, j)),
    ]
    args = [x2d, w]
    if tabs is not None:
        blocks_per_seq = seq_len // tm

        def tab_map(i, j):
            kind = sum((j >= s).astype(jnp.int32) for s in kind_starts)
            return (kind, 0, i % blocks_per_seq, 0)

        in_specs.append(pl.BlockSpec((None, 2, tm, LANES), tab_map))
        args.append(tabs)
    return pl.pallas_call(
        functools.partial(_proj_kernel, rotary=tabs is not None),
        grid=(m // tm, n_blocks),
        in_specs=in_specs,
        out_specs=pl.BlockSpec((tm, tn), lambda i, j: (i, j)),
        out_shape=jax.ShapeDtypeStruct((m, n_blocks * tn), out_dtype),
        scratch_shapes=[pltpu.VMEM((tm, k), BF16)],
        compiler_params=_params("parallel", "arbitrary"),
        name="proj_rotary" if tabs is not None else "proj",
    )(*args)


def _softplus(x):
    return jnp.maximum(x, 0.0) + jnp.log(1.0 + jnp.exp(-jnp.abs(x)))


def _split_dot(x, ones_bd):
    hi = x.astype(BF16)
    lo = (x - hi.astype(F32)).astype(BF16)
    return (jnp.dot(hi, ones_bd, preferred_element_type=F32)
            + jnp.dot(lo, ones_bd, preferred_element_type=F32))


def _rwkv_prep(z, prev, mix, wl, par, ones_bd):
    d = RWKV_DIM
    pw = 2 * RWKV_HEAD_DIM
    row = lax.broadcasted_iota(jnp.int32, z.shape, 0)
    shifted = jnp.where(row == 0, prev, pltpu.roll(z, 1, 0))
    zz = z + (shifted - z) * mix
    r, k, v, lo = zz[:, 0:d], zz[:, d:2 * d], zz[:, 2 * d:3 * d], zz[:, 3 * d:]
    lane = lax.broadcasted_iota(jnp.int32, lo.shape, 1)
    act = jnp.where(lane < DECAY_LORA, jnp.tanh(lo),
                    jnp.where(lane < DECAY_LORA + ICLR_LORA, lo, jax.nn.sigmoid(lo)))
    lora = _bdot(act, wl)
    w0, a0, k_k, k_a = par[0:1, :], par[1:2, :], par[2:3, :], par[3:4, :]
    w = -_softplus(-(w0 + lora[:, 0:d])) - 0.5
    a = jax.nn.sigmoid(a0 + lora[:, d:2 * d])
    kk = k * k_k
    sq = kk * kk
    ssq = jnp.concatenate([_split_dot(sq[:, p0:p0 + pw], ones_bd) for p0 in range(0, d, pw)], axis=1)
    kk = kk / jnp.maximum(jnp.sqrt(ssq), 1e-12)
    return r, -jnp.exp(w), k * (1.0 + (a - 1.0) * k_a), v, -kk, kk * a, lora[:, 2 * d:3 * d]


def _rwkv_chunk_kernel(x_ref, wr_ref, mix_ref, wl_ref, par_ref, o_ref, s_ref, prev_ref, *, chunks):
    c, n = RWKV_CHUNK, RWKV_HEAD_DIM
    pw = 2 * n
    assert c == n and pw == LANES
    pairs = RWKV_HEADS // 2
    units = [(ch, p) for ch in range(chunks) for p in range(pairs)]

    first = pl.program_id(1) == 0

    @pl.when(first)
    def _():
        s_ref[...] = jnp.zeros_like(s_ref)

    row = lax.broadcasted_iota(jnp.int32, (c, pw), 0)
    col = lax.broadcasted_iota(jnp.int32, (c, pw), 1) & (n - 1)
    tril_inc = col <= row
    tril_str = col < row
    eye = col == row
    level_masks = [((row >> (lvl + 1)) == (col >> (lvl + 1))) & (((row >> lvl) & 1) == 1) & (((col >> lvl) & 1) == 0)
                   for lvl in range(c.bit_length() - 1)]
    r2 = lax.broadcasted_iota(jnp.int32, (pw, pw), 0)
    c2 = lax.broadcasted_iota(jnp.int32, (pw, pw), 1)
    same_head = (r2 < n) == (c2 < n)
    eye2 = r2 == c2
    ones_bd = jnp.where(same_head, 1.0, 0.0).astype(BF16)

    def stack(x):
        xb = x.astype(BF16)
        return jnp.where(same_head, jnp.concatenate([xb, xb], axis=0), jnp.zeros((), BF16))

    rows = chunks * c
    z = jnp.dot(x_ref[...].astype(BF16), wr_ref[...], preferred_element_type=F32)
    prev = jnp.where(first, 0.0, prev_ref[SUBLANES - 1:SUBLANES, :])
    r_all, lw, kp_all, v_all, na_all, b_all, g_all = _rwkv_prep(
        z, prev, mix_ref[...], wl_ref[...], par_ref[...], ones_bd)
    prev_ref[...] = z[rows - SUBLANES:, :]

    tr = lax.broadcasted_iota(jnp.int32, (rows, rows), 0)
    tc = lax.broadcasted_iota(jnp.int32, (rows, rows), 1)
    chunk_tril = ((tr >> (c.bit_length() - 1)) == (tc >> (c.bit_length() - 1))) & (tc <= tr)
    chunk_tril = jnp.where(chunk_tril, 1.0, 0.0).astype(BF16)
    lw_hi = lw.astype(BF16)
    lw_r1 = lw - lw_hi.astype(F32)
    lw_mid = lw_r1.astype(BF16)
    lw_lo = (lw_r1 - lw_mid.astype(F32)).astype(BF16)
    cum = (jnp.dot(chunk_tril, lw_hi, preferred_element_type=F32)
           + jnp.dot(chunk_tril, lw_mid, preferred_element_type=F32)
           + jnp.dot(chunk_tril, lw_lo, preferred_element_type=F32))
    r_k, lnx_g, lnx_b = par_ref[4:5, :], par_ref[5:6, :], par_ref[6:7, :]

    at, rt, bt, kt, bh, kh, vv, p_end = {}, {}, {}, {}, {}, {}, {}, {}
    for ch in range(chunks):
        rs = slice(ch * c, (ch + 1) * c)
        cum_c = cum[rs]
        cum_end = cum_c[c - 1:c, :]
        e_inv = jnp.exp(-cum_c)
        e_end = jnp.exp(cum_end - cum_c)
        at_c = na_all[rs] * jnp.exp(cum_c - lw[rs])
        rt_c = r_all[rs] * jnp.exp(cum_c)
        b_c, kp_c = b_all[rs], kp_all[rs]
        bt_c, kt_c, bh_c, kh_c = b_c * e_inv, kp_c * e_inv, b_c * e_end, kp_c * e_end
        pe_c = jnp.exp(cum_end)
        for p in range(pairs):
            ls = slice(p * pw, (p + 1) * pw)
            u = (ch, p)
            at[u], rt[u], bt[u], kt[u], bh[u], kh[u] = at_c[:, ls], rt_c[:, ls], bt_c[:, ls], kt_c[:, ls], bh_c[:, ls], kh_c[:, ls]
            vv[u], p_end[u] = v_all[rs, ls], pe_c[:, ls]

    ar = {u: jnp.concatenate([at[u], rt[u]], axis=0) for u in units}
    g = {u: _bdot(ar[u], jnp.concatenate([stack(bt[u]), stack(kt[u])], axis=0), _NT)
         for u in units}
    a_ab = {u: jnp.where(tril_str, g[u][:c, :pw], 0.0) for u in units}
    a_rb = {u: jnp.where(tril_inc, g[u][c:, :pw], 0.0) for u in units}
    a_ak = {u: jnp.where(tril_str, g[u][:c, pw:], 0.0) for u in units}
    a_rk = {u: jnp.where(tril_inc, g[u][c:, pw:], 0.0) for u in units}
    v_st = {u: stack(vv[u]) for u in units}
    ak_v = {u: _bdot(a_ak[u], v_st[u]) for u in units}
    inv = {u: jnp.where(eye, 1.0, jnp.where(level_masks[0], a_ab[u], 0.0)) for u in units}
    for mask in level_masks[1:]:
        e = {u: _bdot(jnp.where(mask, a_ab[u], 0.0), stack(inv[u])) for u in units}
        inv = {u: inv[u] + _bdot(inv[u], stack(e[u])) for u in units}
    wu = {u: _bdot(inv[u], jnp.concatenate([stack(at[u]), stack(ak_v[u])], axis=1)) for u in units}
    w_mat = {u: wu[u][:, :pw] for u in units}
    u_loc = {u: wu[u][:, pw:] for u in units}
    m_c = {u: jnp.where(same_head, _bdot(w_mat[u], bh[u], _TN), 0.0) + jnp.where(eye2, p_end[u], 0.0)
           for u in units}
    n_c = {u: jnp.where(same_head, _bdot(u_loc[u], bh[u], _TN) + _bdot(vv[u], kh[u], _TN), 0.0)
           for u in units}
    qy = {u: _bdot(a_rb[u], jnp.concatenate([stack(w_mat[u]), stack(u_loc[u])], axis=1)) for u in units}
    q_c = {u: rt[u] + qy[u][:, :pw] for u in units}
    y_loc = {u: qy[u][:, pw:] + _bdot(a_rk[u], v_st[u]) for u in units}

    state = [s_ref[p] for p in range(pairs)]
    y = {}
    for ch in range(chunks):
        for p in range(pairs):
            u = (ch, p)
            y[u] = _bdot(q_c[u], state[p], _NT) + y_loc[u]
            state[p] = _bdot(state[p], m_c[u]) + n_c[u]
    for p in range(pairs):
        s_ref[p] = state[p]

    inv_n = 1.0 / n
    mu = {u: _split_dot(y[u], ones_bd) * inv_n for u in units}
    var = {u: _split_dot(jnp.square(y[u] - mu[u]), ones_bd) * inv_n for u in units}
    for (ch, p) in units:
        u = (ch, p)
        rs, ls = slice(ch * c, (ch + 1) * c), slice(p * pw, (p + 1) * pw)
        yn = (y[u] - mu[u]) * lax.rsqrt(var[u] + RWKV_GN_EPS) * lnx_g[:, ls] + lnx_b[:, ls]
        bonus = _split_dot(r_all[rs, ls] * kp_all[rs, ls] * r_k[:, ls], ones_bd) * vv[u]
        o_ref[rs, ls] = ((yn + bonus) * g_all[rs, ls]).astype(o_ref.dtype)


def _rwkv_scan(x2d, w_r, mix, wl, par, *, batch, seq_len):
    c = RWKV_CHUNK
    chunks = RWKV_SCAN_CHUNKS
    rows = chunks * c
    dm, zc = w_r.shape
    assert seq_len % rows == 0
    const = lambda a: pl.BlockSpec(a.shape, lambda bi, ci: (0, 0), pipeline_mode=pl.Buffered(1))
    out = pl.pallas_call(
        functools.partial(_rwkv_chunk_kernel, chunks=chunks),
        grid=(batch, seq_len // rows),
        in_specs=[pl.BlockSpec((None, rows, dm), lambda bi, ci: (bi, ci, 0)),
                  const(w_r), const(mix), const(wl), const(par)],
        out_specs=pl.BlockSpec((None, rows, RWKV_DIM), lambda bi, ci: (bi, ci, 0)),
        out_shape=jax.ShapeDtypeStruct((batch, seq_len, RWKV_DIM), BF16),
        scratch_shapes=[pltpu.VMEM((RWKV_HEADS // 2, 2 * RWKV_HEAD_DIM, 2 * RWKV_HEAD_DIM), F32),
                        pltpu.VMEM((SUBLANES, zc), F32)],
        compiler_params=_params("parallel", "arbitrary"),
        name="rwkv_scan",
    )(x2d.reshape(batch, seq_len, dm), w_r, mix, wl, par)
    return out.reshape(batch * seq_len, RWKV_DIM)


def _moba_kernel(q_ref, k_ref, v_ref, pool_ref, o_ref, kext_ref, s_ref):
    blk, dh = MOBA_BLOCK, MOBA_HEAD_DIM
    t_len = k_ref.shape[0]
    nb = t_len // blk
    nbp = -(-nb // SUBLANES) * SUBLANES
    shift = blk.bit_length() - 1
    key_tile = 2 * blk

    lane = lax.broadcasted_iota(jnp.int32, (blk, LANES), 1)
    for j in range(nb):
        rs = slice(j * blk, (j + 1) * blk)
        kext_ref[rs, 0:dh] = k_ref[rs, :]
        kext_ref[rs, dh:dh + LANES] = jnp.where(lane == j, 1.0, 0.0).astype(BF16)

    km = jnp.dot(pool_ref[...], k_ref[...], preferred_element_type=F32)[0:nbp]
    km_hi = km.astype(BF16)
    km_lo = (km - km_hi.astype(F32)).astype(BF16)
    q_all = q_ref[...]
    gate = (lax.dot_general(km_hi, q_all, _NT, preferred_element_type=F32)
            + lax.dot_general(km_lo, q_all, _NT, preferred_element_type=F32))
    bidx = lax.broadcasted_iota(jnp.int32, gate.shape, 0)
    qblk = lax.broadcasted_iota(jnp.int32, gate.shape, 1) >> shift
    past = bidx < qblk
    gate = jnp.where(past, gate, NEG_INF)
    rank = jnp.zeros(gate.shape, jnp.int32)
    for jp in range(nb):
        other = gate[jp:jp + 1, :]
        rank = rank + ((other > gate) | ((other == gate) & (jp < bidx))).astype(jnp.int32)
    allowed = (past & (rank < MOBA_TOPK)) | (bidx == qblk)
    bias_t = jnp.where(allowed, 0.0, NEG_INF).astype(BF16)
    bias_t = jnp.concatenate([bias_t, jnp.zeros((LANES - nbp, t_len), BF16)], axis=0)
    r2 = lax.broadcasted_iota(jnp.int32, (LANES, LANES), 0)
    c2 = lax.broadcasted_iota(jnp.int32, (LANES, LANES), 1)
    eye = jnp.where(r2 == c2, 1.0, 0.0).astype(BF16)
    row = lax.broadcasted_iota(jnp.int32, (blk, blk), 0)
    col = lax.broadcasted_iota(jnp.int32, (blk, blk), 1)
    causal = col <= row
    colmax = lambda x: functools.reduce(jnp.maximum, [x[:, a:a + LANES] for a in range(0, x.shape[1], LANES)])
    colsum = lambda x: functools.reduce(jnp.add, [x[:, a:a + LANES] for a in range(0, x.shape[1], LANES)])

    for i in range(nb):
        qs = slice(i * blk, (i + 1) * blk)
        bias = lax.dot_general(bias_t[:, qs], eye, _TN, preferred_element_type=F32)
        q_ext = jnp.concatenate([q_ref[qs, :], bias.astype(BF16)], axis=1)
        nk = (i + 1) * blk
        tiles = [(a, min(a + key_tile, nk)) for a in range(0, nk, key_tile)]
        m_lane = None
        for a, e in tiles:
            s = lax.dot_general(q_ext, kext_ref[a:e, :], _NT, preferred_element_type=F32)
            if e == nk:
                s_own = jnp.where(causal, s[:, e - a - blk:], NEG_INF)
                s = s_own if e - a == blk else jnp.concatenate([s[:, :e - a - blk], s_own], axis=1)
            s_ref[:, a:e] = s
            part = colmax(s)
            m_lane = part if m_lane is None else jnp.maximum(m_lane, part)
        m = jnp.max(m_lane, axis=-1, keepdims=True)
        l_lane = None
        acc = None
        for a, e in tiles:
            p = jnp.exp(s_ref[:, a:e] - m)
            part = colsum(p)
            l_lane = part if l_lane is None else l_lane + part
            pv = jnp.dot(p.astype(BF16), v_ref[a:e, :], preferred_element_type=F32)
            acc = pv if acc is None else acc + pv
        l = jnp.sum(l_lane, axis=-1, keepdims=True)
        o_ref[qs, :] = (acc / l).astype(o_ref.dtype)


def _moba(qkv, *, batch, seq_len):
    blk, dh, nh = MOBA_BLOCK, MOBA_HEAD_DIM, MOBA_HEADS
    assert seq_len % blk == 0 and seq_len // blk <= LANES and dh == LANES
    pool = (jnp.arange(LANES)[:, None] == (jnp.arange(seq_len) // blk)[None, :]).astype(BF16) / blk
    qkv3 = qkv.reshape(batch, seq_len, 3 * MOBA_WIDTH)
    head = lambda off: pl.BlockSpec((None, seq_len, dh), lambda b, h: (b, 0, off + h))
    out = pl.pallas_call(
        _moba_kernel,
        grid=(batch, nh),
        in_specs=[head(0), head(nh), head(2 * nh), pl.BlockSpec((LANES, seq_len), lambda b, h: (0, 0))],
        out_specs=head(0),
        out_shape=jax.ShapeDtypeStruct((batch, seq_len, MOBA_WIDTH), BF16),
        scratch_shapes=[pltpu.VMEM((seq_len, dh + LANES), BF16), pltpu.VMEM((blk, seq_len), F32)],
        compiler_params=_params("parallel", "parallel"),
        name="moba",
    )(qkv3, qkv3, qkv3, pool)
    return out.reshape(batch * seq_len, MOBA_WIDTH)


def _residue_perm(d):
    p = DIL_PERM_ROWS
    idx = jnp.arange(p)
    nat = (idx % (p // d)) * d + idx // (p // d)
    return (nat[:, None] == idx[None, :]).astype(BF16)


def _dilated_proj_kernel(x_ref, w_ref, tab_ref, *rest, dilations):
    n = len(dilations)
    p_refs, o_refs, xb_ref = rest[:n], rest[n:2 * n], rest[2 * n]
    j = pl.program_id(1)

    @pl.when(j == 0)
    def _():
        xb_ref[...] = x_ref[...].astype(BF16)

    tm = x_ref.shape[0]
    grp = DIL_PERM_ROWS
    nat_ref = o_refs[0]
    for r0 in range(0, tm, grp):
        rs = slice(r0, r0 + grp)
        acc = jnp.dot(xb_ref[rs, :], w_ref[...], preferred_element_type=F32)
        t_z, t_partner = tab_ref[0, rs, :], tab_ref[1, rs, :]
        for hb in range(acc.shape[1] // LANES):
            z = acc[:, hb * LANES:(hb + 1) * LANES]
            nat_ref[0, rs, hb * LANES:(hb + 1) * LANES] = (
                z * t_z + pltpu.roll(z, LANES // 2, 1) * t_partner).astype(nat_ref.dtype)

    for g in range(1, n):
        d = dilations[g]
        w = grp // d

        @pl.when((j == g) | (j >= n))
        def _(g=g, d=d, w=w):
            for gi in range(tm // grp):
                y = jnp.dot(p_refs[g][...], nat_ref[0, gi * grp:(gi + 1) * grp, :],
                            preferred_element_type=F32).astype(o_refs[g].dtype)
                for r in range(d):
                    o_refs[g][r, gi * w:(gi + 1) * w, :] = y[r * w:(r + 1) * w, :]


def _dilated_proj(x2d, w, tabs, dilations, *, batch, seq_len):
    m, k = x2d.shape
    n = len(dilations)
    tm = min(1024, seq_len)
    assert dilations[0] == 1 and w.shape[1] == (n + 2) * C_WIDTH
    assert m % tm == 0 and seq_len % tm == 0 and tm % DIL_PERM_ROWS == 0
    assert all(DIL_PERM_ROWS % (d * 2 * SUBLANES) == 0 for d in dilations[1:])
    bps = seq_len // tm
    perms = [_residue_perm(d) for d in dilations]

    def tab_map(i, j):
        return ((j >= n).astype(jnp.int32) + (j > n).astype(jnp.int32), 0, i % bps, 0)

    def out_map(g):
        return lambda i, j: (i // bps, 0, i % bps, (j > g).astype(jnp.int32) + (j > n).astype(jnp.int32))

    outs = pl.pallas_call(
        functools.partial(_dilated_proj_kernel, dilations=tuple(dilations)),
        grid=(m // tm, n + 2),
        in_specs=[pl.BlockSpec((tm, k), lambda i, j: (i, 0)),
                  pl.BlockSpec((k, C_WIDTH), lambda i, j: (0, j)),
                  pl.BlockSpec((None, 2, tm, LANES), tab_map)]
                 + [pl.BlockSpec(p.shape, lambda i, j: (0, 0)) for p in perms],
        out_specs=[pl.BlockSpec((None, d, tm // d, C_WIDTH), out_map(g)) for g, d in enumerate(dilations)],
        out_shape=[jax.ShapeDtypeStruct((batch, d, seq_len // d, 3 * C_WIDTH), BF16) for d in dilations],
        scratch_shapes=[pltpu.VMEM((tm, k), BF16)],
        compiler_params=_params("parallel", "arbitrary"),
        name="dilated_proj",
    )(x2d, w, tabs, *perms)
    return outs


def _dilated_kernel(q_ref, k_ref, kp_ref, v_ref, vp_ref, o_ref, lse_ref):
    w, dh = DIL_WINDOW, DIL_HEAD_DIM
    tiles = q_ref.shape[0] // w
    has_prev = pl.program_id(2) > 0
    row = lax.broadcasted_iota(jnp.int32, (w, 2 * w), 0)
    col = lax.broadcasted_iota(jnp.int32, (w, 2 * w), 1)
    band = (col >= row) & (col <= row + w)
    first_band = band & (has_prev | (col >= w))
    er = lax.broadcasted_iota(jnp.int32, (w, w), 0)
    ec = lax.broadcasted_iota(jnp.int32, (w, w), 1)
    eye = er == ec
    units = [(h, t) for h in range(DIL_HEADS) for t in range(tiles)]

    def banded(ref, halo_ref, h, t):
        hs = slice(h * dh, (h + 1) * dh)
        if t == 0:
            return jnp.concatenate([halo_ref[:, hs], ref[0:w, hs]], axis=0)
        return ref[(t - 1) * w:(t + 1) * w, hs]

    s = {}
    for h, t in units:
        q = q_ref[t * w:(t + 1) * w, h * dh:(h + 1) * dh]
        sc = lax.dot_general(q, banded(k_ref, kp_ref, h, t), _NT, preferred_element_type=F32)
        s[h, t] = jnp.where(first_band if t == 0 else band, sc, NEG_INF)
    m = {u: jnp.max(jnp.maximum(s[u][:, :w], s[u][:, w:]), axis=-1, keepdims=True) for u in units}
    p = {u: jnp.exp(s[u] - m[u]) for u in units}
    l = {u: jnp.sum(p[u][:, :w] + p[u][:, w:], axis=-1, keepdims=True) for u in units}
    o = {(h, t): jnp.dot(p[h, t].astype(BF16), banded(v_ref, vp_ref, h, t), preferred_element_type=F32)
         for h, t in units}
    for h, t in units:
        u = (h, t)
        o_ref[t * w:(t + 1) * w, h * dh:(h + 1) * dh] = (o[u] / l[u]).astype(o_ref.dtype)
        lse = m[u] + jnp.log(l[u])
        lse_ref[t, h:h + 1, :] = jnp.sum(jnp.where(eye, lse, 0.0), axis=0, keepdims=True)


def _dilated_group(src, cols, *, seq_len):
    batch, d, length, _ = src.shape
    w = DIL_WINDOW
    rows = min(DIL_ROWS, length)
    assert length % rows == 0 and rows % w == 0 and length * d == seq_len
    tiles = rows // w
    qc, kc, vc = cols
    blk = lambda cb: pl.BlockSpec((None, None, rows, C_WIDTH), lambda b, r, i: (b, r, i, cb))
    halo = lambda cb: pl.BlockSpec(
        (None, None, w, C_WIDTH), lambda b, r, i: (b, r, jnp.maximum(i * tiles - 1, 0), cb))
    o, lse = pl.pallas_call(
        _dilated_kernel,
        grid=(batch, d, length // rows),
        in_specs=[blk(qc), blk(kc), halo(kc), blk(vc), halo(vc)],
        out_specs=[
            pl.BlockSpec((None, None, rows, C_WIDTH), lambda b, r, i: (b, r, i, 0)),
            pl.BlockSpec((None, None, tiles, DIL_HEADS, w), lambda b, r, i: (b, r, i, 0, 0)),
        ],
        out_shape=[
            jax.ShapeDtypeStruct((batch, d, length, C_WIDTH), BF16),
            jax.ShapeDtypeStruct((batch, d, length // w, DIL_HEADS, w), F32),
        ],
        compiler_params=_params("parallel", "parallel", "arbitrary"),
        name=f"dilated_d{d}",
    )(src, src, src, src, src)
    lse = lse.transpose(0, 2, 4, 1, 3).reshape(batch, seq_len, DIL_HEADS)
    return o, lse


def _outproj_ln_kernel(*refs, n_act):
    acts = refs[:n_act]
    w_ref, x_ref, g_ref, b_ref, o_ref = refs[n_act:]
    proj = None
    off = 0
    for a_ref in acts:
        kw = a_ref.shape[1]
        part = jnp.dot(a_ref[...], w_ref[off:off + kw, :], preferred_element_type=F32)
        proj = part if proj is None else proj + part
        off += kw
    y = DEEPNORM_ALPHA * x_ref[...] + proj
    o_ref[...] = _layer_norm(y, g_ref[...], b_ref[...])


def _outproj_ln(acts, w, x2d, g, b):
    m, dm = x2d.shape
    tm = OUTPROJ_ROWS
    assert m % tm == 0
    rowspec = lambda a: pl.BlockSpec((tm, a.shape[1]), lambda i: (i, 0))
    full = lambda a: pl.BlockSpec(a.shape, lambda i: (0, 0))
    return pl.pallas_call(
        functools.partial(_outproj_ln_kernel, n_act=len(acts)),
        grid=(m // tm,),
        in_specs=[rowspec(a) for a in acts] + [full(w), rowspec(x2d), full(g), full(b)],
        out_specs=pl.BlockSpec((tm, dm), lambda i: (i, 0)),
        out_shape=jax.ShapeDtypeStruct((m, dm), F32),
        compiler_params=_params("parallel"),
        name="outproj_ln",
    )(*acts, w, x2d, g, b)


def _merge_outproj_ln_kernel(*refs, dilations):
    n = len(dilations)
    o_refs, lse_refs, p_refs = refs[:n], refs[n:2 * n], refs[2 * n:3 * n]
    w_ref, x_ref, g_ref, b_ref, out_ref = refs[3 * n:]
    nh, dh, grp = DIL_HEADS, DIL_HEAD_DIM, DIL_PERM_ROWS
    tm = x_ref.shape[0]
    branch = []
    for o_ref, p_ref, d in zip(o_refs, p_refs, dilations):
        if d == 1:
            branch.append(o_ref[0].astype(F32))
            continue
        w = grp // d
        nat = []
        for gi in range(tm // grp):
            perm = jnp.concatenate([o_ref[r, gi * w:(gi + 1) * w, :] for r in range(d)], axis=0)
            nat.append(lax.dot_general(p_ref[...], perm, _TN, preferred_element_type=F32))
        branch.append(jnp.concatenate(nat, axis=0))
    parts = [ref[...] for ref in lse_refs]
    m = functools.reduce(jnp.maximum, parts)
    es = [jnp.exp(p - m) for p in parts]
    den = functools.reduce(jnp.add, es)
    wts = [e / den for e in es]
    heads = []
    for h in range(nh):
        hs = slice(h * dh, (h + 1) * dh)
        heads.append(functools.reduce(jnp.add, [wts[g][:, h:h + 1] * branch[g][:, hs] for g in range(n)]))
    proj = _bdot(jnp.concatenate(heads, axis=1), w_ref[...])
    y = DEEPNORM_ALPHA * x_ref[...] + proj
    out_ref[...] = _layer_norm(y, g_ref[...], b_ref[...])


def _merge_outproj_ln(outs, lses, dilations, w, x2d, g, b, *, batch, seq_len):
    dm = x2d.shape[1]
    tm = OUTPROJ_ROWS
    assert seq_len % tm == 0 and tm % DIL_PERM_ROWS == 0
    x3 = x2d.reshape(batch, seq_len, dm)
    perms = [_residue_perm(d) for d in dilations]
    full = lambda a: pl.BlockSpec(a.shape, lambda bi, i: (0,) * a.ndim)
    rows3 = lambda a: pl.BlockSpec((None, tm, a.shape[2]), lambda bi, i: (bi, i, 0))
    in_specs = ([pl.BlockSpec((None, d, tm // d, C_WIDTH), lambda bi, i: (bi, 0, i, 0)) for d in dilations]
                + [rows3(a) for a in lses] + [full(p) for p in perms]
                + [full(w), rows3(x3), full(g), full(b)])
    out = pl.pallas_call(
        functools.partial(_merge_outproj_ln_kernel, dilations=tuple(dilations)),
        grid=(batch, seq_len // tm),
        in_specs=in_specs,
        out_specs=pl.BlockSpec((None, tm, dm), lambda bi, i: (bi, i, 0)),
        out_shape=jax.ShapeDtypeStruct((batch, seq_len, dm), F32),
        compiler_params=_params("parallel", "parallel"),
        name="merge_outproj_ln",
    )(*outs, *lses, *perms, w, x3, g, b)
    return out.reshape(batch * seq_len, dm)


def _mlp_ln_kernel(x_ref, w1_ref, w2_ref, g_ref, b_ref, o_ref):
    x = x_ref[...]
    xb = x.astype(BF16)
    hidden = w1_ref.shape[1]
    hs = []
    for c0 in range(0, hidden, MLP_HIDDEN_CHUNK):
        h = jnp.dot(xb, w1_ref[:, c0:c0 + MLP_HIDDEN_CHUNK], preferred_element_type=F32)
        hs.append(jnp.square(jnp.maximum(h, 0.0)).astype(BF16))
    y = jnp.dot(jnp.concatenate(hs, axis=1), w2_ref[...], preferred_element_type=F32)
    o_ref[...] = _layer_norm(DEEPNORM_ALPHA * x + y, g_ref[...], b_ref[...])


def _mlp_ln(x2d, w1, w2, g, b):
    m, dm = x2d.shape
    hidden = w1.shape[1]
    tm = MLP_ROWS
    assert m % tm == 0 and hidden % MLP_HIDDEN_CHUNK == 0
    const = lambda a: pl.BlockSpec(a.shape, lambda i: (0, 0), pipeline_mode=pl.Buffered(1))
    return pl.pallas_call(
        _mlp_ln_kernel,
        grid=(m // tm,),
        in_specs=[pl.BlockSpec((tm, dm), lambda i: (i, 0)), const(w1), const(w2), const(g), const(b)],
        out_specs=pl.BlockSpec((tm, dm), lambda i: (i, 0)),
        out_shape=jax.ShapeDtypeStruct((m, dm), F32),
        compiler_params=_params("parallel"),
        name="mlp_ln",
    )(x2d, w1, w2, g, b)


def _rwkv_moba_layer(x2d, w_in, shift_mix, w0, w_up, a0, a_up, g_up, k_k, k_a, r_k, lnx_g, lnx_b,
                     w_out, ln_g, ln_b, *, batch, seq_len):
    d = RWKV_DIM
    tabs = _rotary_tables(seq_len, MOBA_HEAD_DIM ** -0.5)
    w_m = w_in[:, RWKV_PROJ_COLS:]
    w_m = jnp.concatenate([_rotary_weight(w_m[:, :2 * MOBA_WIDTH], 2 * MOBA_HEADS),
                           w_m[:, 2 * MOBA_WIDTH:]], axis=1)
    qkv = _proj(x2d, w_m.astype(BF16), seq_len=seq_len, out_dtype=BF16, tn=MOBA_WIDTH, tabs=tabs,
                kind_starts=(1, 2))
    y_b = _moba(qkv, batch=batch, seq_len=seq_len)
    pad = RWKV_Z_COLS - RWKV_PROJ_COLS
    w_r = jnp.pad(w_in[:, :RWKV_PROJ_COLS], ((0, 0), (0, pad))).astype(BF16)
    mix = jnp.pad(shift_mix, (0, pad))[None, :]
    wl = jnp.zeros((LORA_PAD, 3 * d), F32)
    wl = wl.at[0:DECAY_LORA, 0:d].set(w_up)
    wl = wl.at[DECAY_LORA:DECAY_LORA + ICLR_LORA, d:2 * d].set(a_up)
    wl = wl.at[DECAY_LORA + ICLR_LORA:LORA_COLS, 2 * d:3 * d].set(g_up).astype(BF16)
    par = jnp.stack([w0, a0, k_k, k_a, r_k.reshape(d), lnx_g, lnx_b, jnp.zeros((d,), F32)])
    y_a = _rwkv_scan(x2d, w_r, mix, wl, par, batch=batch, seq_len=seq_len)
    return _outproj_ln([y_a, y_b], w_out.astype(BF16), x2d, ln_g[None, :], ln_b[None, :])


def _dilated_layer(x2d, w_in, w_out, ln_g, ln_b, *, batch, seq_len):
    tabs = _rotary_tables(seq_len, DIL_HEAD_DIM ** -0.5)
    nqk = (DIL_GROUPS + 1) * C_WIDTH
    w_c = jnp.concatenate([_rotary_weight(w_in[:, :nqk], (DIL_GROUPS + 1) * DIL_HEADS), w_in[:, nqk:]], axis=1)
    dilations = [dilation for _, dilation in DIL_PAIRS]
    assert all(span // dilation == DIL_WINDOW for span, dilation in DIL_PAIRS)
    srcs = _dilated_proj(x2d, w_c.astype(BF16), tabs, dilations, batch=batch, seq_len=seq_len)
    outs, lses = [], []
    for src in srcs:
        o, lse = _dilated_group(src, (0, 1, 2), seq_len=seq_len)
        outs.append(o)
        lses.append(lse)
    return _merge_outproj_ln(outs, lses, dilations, w_out.astype(BF16), x2d, ln_g[None, :], ln_b[None, :],
                             batch=batch, seq_len=seq_len)


def kernel(x, ab_w_in, ab_shift_mix, ab_w0, ab_w_up, ab_a0, ab_a_up, ab_g_up, ab_k_k, ab_k_a, ab_r_k,
           ab_lnx_g, ab_lnx_b, ab_w_out, c_w_in, c_w_out, ln1_g, ln1_b, mlp_w1, mlp_w2, ln2_g, ln2_b):
    batch, seq_len, dm = x.shape
    assert dm == D_MODEL
    h = x.reshape(batch * seq_len, dm)
    depth = ln1_g.shape[0]
    for layer in range(depth):
        j = layer // 2
        if layer % 2 == 0:
            h = _rwkv_moba_layer(h, ab_w_in[j], ab_shift_mix[j], ab_w0[j], ab_w_up[j], ab_a0[j], ab_a_up[j],
                                 ab_g_up[j], ab_k_k[j], ab_k_a[j], ab_r_k[j], ab_lnx_g[j], ab_lnx_b[j],
                                 ab_w_out[j], ln1_g[layer], ln1_b[layer], batch=batch, seq_len=seq_len)
        else:
            h = _dilated_layer(h, c_w_in[j], c_w_out[j], ln1_g[layer], ln1_b[layer],
                               batch=batch, seq_len=seq_len)
        h = _mlp_ln(h, mlp_w1[layer].astype(BF16), mlp_w2[layer].astype(BF16),
                    ln2_g[layer][None, :], ln2_b[layer][None, :])
    return h.reshape(batch, seq_len, dm)
```

```python
import functools

import jax
import jax.numpy as jnp
from jax import lax
from jax.experimental import pallas as pl
from jax.experimental.pallas import tpu as pltpu

F32 = jnp.float32
BF16 = jnp.bfloat16

D_MODEL = 1024
DEPTH = 2
DEEPNORM_ALPHA = (2 * DEPTH) ** 0.25
LN_EPS = 1e-5
ROPE_THETA = 500000.0
ROPE_FRACTION = 4
RWKV_HEAD_DIM = 64
RWKV_DIM = D_MODEL // 2
RWKV_HEADS = RWKV_DIM // RWKV_HEAD_DIM
DECAY_LORA = 64
ICLR_LORA = 64
GATE_LORA = 160
RWKV_GN_EPS = 64e-5
RWKV_PROJ_COLS = 3 * RWKV_DIM + DECAY_LORA + ICLR_LORA + GATE_LORA
MOBA_HEAD_DIM = 128
MOBA_WIDTH = D_MODEL - RWKV_DIM
MOBA_HEADS = MOBA_WIDTH // MOBA_HEAD_DIM
MOBA_BLOCK = 256
MOBA_TOPK = 3
DIL_PAIRS = ((128, 1), (512, 4), (2048, 16))
DIL_GROUPS = len(DIL_PAIRS)
DIL_HEAD_DIM = 128
DIL_HEADS = D_MODEL // DIL_HEAD_DIM
C_WIDTH = DIL_HEADS * DIL_HEAD_DIM
C_IN_COLS = DIL_GROUPS * C_WIDTH + 2 * C_WIDTH
MLP_HIDDEN = 4 * D_MODEL
NEG_INF = -1e30

LANES = 128
SUBLANES = 8
VMEM_LIMIT_BYTES = 56 * 1024 * 1024

LORA_COLS = DECAY_LORA + ICLR_LORA + GATE_LORA
LORA_PAD = -(-LORA_COLS // LANES) * LANES
RWKV_Z_COLS = 3 * RWKV_DIM + LORA_PAD
RWKV_CHUNK = 64
RWKV_SCAN_CHUNKS = 4
DIL_WINDOW = 128
DIL_ROWS = 256
DIL_PERM_ROWS = 256
PROJ_ROW_CHUNK = 256
LAYER_ROWS = 512
MLP_HIDDEN_CHUNK = 1024

_NN = (((1,), (0,)), ((), ()))
_NT = (((1,), (1,)), ((), ()))
_TN = (((0,), (0,)), ((), ()))


def _bdot(a, b, dims=_NN):
    return lax.dot_general(a.astype(BF16), b.astype(BF16), dims, preferred_element_type=F32)


def _layer_norm(y, g, b):
    mu = jnp.mean(y, axis=-1, keepdims=True)
    var = jnp.mean(jnp.square(y - mu), axis=-1, keepdims=True)
    return (y - mu) * lax.rsqrt(var + LN_EPS) * g + b


def _params(*sem):
    return pltpu.CompilerParams(dimension_semantics=sem, vmem_limit_bytes=VMEM_LIMIT_BYTES)


ROT_DIMS = MOBA_HEAD_DIM // ROPE_FRACTION
ROT_HALF = ROT_DIMS // 2


def _rotary_weight(w, n_heads):
    rest_a = LANES // 2 - ROT_HALF
    heads = []
    for h in range(n_heads):
        head = w[:, h * LANES:(h + 1) * LANES]
        rot, rest = head[:, :ROT_DIMS], head[:, ROT_DIMS:]
        heads += [rot[:, :ROT_HALF], rest[:, :rest_a], rot[:, ROT_HALF:], rest[:, rest_a:]]
    return jnp.concatenate(heads, axis=1)


def _rotary_tables(seq_len, q_scale):
    inv_freq = ROPE_THETA ** (-jnp.arange(ROT_HALF, dtype=F32) / ROT_HALF)
    ang = jnp.arange(seq_len).astype(F32)[:, None] * inv_freq[None, :]
    cos, sin = jnp.cos(ang), jnp.sin(ang)
    pad = lambda a, fill: jnp.concatenate([a, jnp.full((seq_len, LANES // 2 - ROT_HALF), fill, F32)], axis=1)
    c = jnp.concatenate([pad(cos, 1.0), pad(cos, 1.0)], axis=1)
    s = jnp.concatenate([pad(-sin, 0.0), pad(sin, 0.0)], axis=1)
    one = jnp.ones_like(c)
    return jnp.stack([jnp.stack([c * q_scale, s * q_scale]), jnp.stack([c, s]),
                      jnp.stack([one, jnp.zeros_like(c)])])


def _rotary_proj_kernel(x_ref, w_ref, tab_ref, o_ref, *, head_kinds):
    tm = x_ref.shape[0]
    chunk = min(PROJ_ROW_CHUNK, tm)
    for r0 in range(0, tm, chunk):
        rs = slice(r0, r0 + chunk)
        acc = jnp.dot(x_ref[rs, :].astype(BF16), w_ref[...], preferred_element_type=F32)
        for hb, kind in enumerate(head_kinds):
            z = acc[:, hb * LANES:(hb + 1) * LANES]
            if kind is not None:
                z = z * tab_ref[kind, 0, rs, :] + pltpu.roll(z, LANES // 2, 1) * tab_ref[kind, 1, rs, :]
            o_ref[rs, hb * LANES:(hb + 1) * LANES] = z.astype(o_ref.dtype)


def _rotary_proj(x2d, w, tabs, head_kinds, *, seq_len):
    m, k = x2d.shape
    n = w.shape[1]
    tm = min(1024, seq_len)
    assert m % tm == 0 and seq_len % tm == 0 and n == len(head_kinds) * LANES
    bps = seq_len // tm
    return pl.pallas_call(
        functools.partial(_rotary_proj_kernel, head_kinds=tuple(head_kinds)),
        grid=(m // tm,),
        in_specs=[pl.BlockSpec((tm, k), lambda i: (i, 0)),
                  pl.BlockSpec((k, n), lambda i: (0, 0), pipeline_mode=pl.Buffered(1)),
                  pl.BlockSpec((tabs.shape[0], 2, tm, LANES), lambda i: (0, 0, i % bps, 0))],
        out_specs=pl.BlockSpec((tm, n), lambda i: (i, 0)),
        out_shape=jax.ShapeDtypeStruct((m, n), BF16),
        compiler_params=_params("parallel"),
        name="rotary_proj",
    )(x2d, w, tabs)


def _softplus(x):
    return jnp.maximum(x, 0.0) + jnp.log(1.0 + jnp.exp(-jnp.abs(x)))


def _split_dot(x, ones_bd):
    hi = x.astype(BF16)
    lo = (x - hi.astype(F32)).astype(BF16)
    return (jnp.dot(hi, ones_bd, preferred_element_type=F32)
            + jnp.dot(lo, ones_bd, preferred_element_type=F32))


def _rwkv_prep(z, prev, mix, wl, par, ones_bd):
    d = RWKV_DIM
    pw = 2 * RWKV_HEAD_DIM
    row = lax.broadcasted_iota(jnp.int32, z.shape, 0)
    shifted = jnp.where(row == 0, prev, pltpu.roll(z, 1, 0))
    zz = z + (shifted - z) * mix
    r, k, v, lo = zz[:, 0:d], zz[:, d:2 * d], zz[:, 2 * d:3 * d], zz[:, 3 * d:]
    lane = lax.broadcasted_iota(jnp.int32, lo.shape, 1)
    act = jnp.where(lane < DECAY_LORA, jnp.tanh(lo),
                    jnp.where(lane < DECAY_LORA + ICLR_LORA, lo, jax.nn.sigmoid(lo)))
    lora = _bdot(act, wl)
    w0, a0, k_k, k_a = par[0:1, :], par[1:2, :], par[2:3, :], par[3:4, :]
    w = -_softplus(-(w0 + lora[:, 0:d])) - 0.5
    a = jax.nn.sigmoid(a0 + lora[:, d:2 * d])
    kk = k * k_k
    sq = kk * kk
    ssq = jnp.concatenate([_split_dot(sq[:, p0:p0 + pw], ones_bd) for p0 in range(0, d, pw)], axis=1)
    kk = kk / jnp.maximum(jnp.sqrt(ssq), 1e-12)
    return r, -jnp.exp(w), k * (1.0 + (a - 1.0) * k_a), v, -kk, kk * a, lora[:, 2 * d:3 * d]


def _rwkv_chunk_kernel(x_ref, wr_ref, mix_ref, wl_ref, par_ref, o_ref, s_ref, prev_ref, *, chunks):
    c, n = RWKV_CHUNK, RWKV_HEAD_DIM
    pw = 2 * n
    assert c == n and pw == LANES
    pairs = RWKV_HEADS // 2
    units = [(ch, p) for ch in range(chunks) for p in range(pairs)]

    first = pl.program_id(1) == 0

    @pl.when(first)
    def _():
        s_ref[...] = jnp.zeros_like(s_ref)

    row = lax.broadcasted_iota(jnp.int32, (c, pw), 0)
    col = lax.broadcasted_iota(jnp.int32, (c, pw), 1) & (n - 1)
    tril_inc = col <= row
    tril_str = col < row
    eye = col == row
    level_masks = [((row >> (lvl + 1)) == (col >> (lvl + 1))) & (((row >> lvl) & 1) == 1) & (((col >> lvl) & 1) == 0)
                   for lvl in range(c.bit_length() - 1)]
    r2 = lax.broadcasted_iota(jnp.int32, (pw, pw), 0)
    c2 = lax.broadcasted_iota(jnp.int32, (pw, pw), 1)
    same_head = (r2 < n) == (c2 < n)
    eye2 = r2 == c2
    ones_bd = jnp.where(same_head, 1.0, 0.0).astype(BF16)

    def stack(x):
        xb = x.astype(BF16)
        return jnp.where(same_head, jnp.concatenate([xb, xb], axis=0), jnp.zeros((), BF16))

    rows = chunks * c
    z = jnp.dot(x_ref[...].astype(BF16), wr_ref[...], preferred_element_type=F32)
    prev = jnp.where(first, 0.0, prev_ref[SUBLANES - 1:SUBLANES, :])
    r_all, lw, kp_all, v_all, na_all, b_all, g_all = _rwkv_prep(
        z, prev, mix_ref[...], wl_ref[...], par_ref[...], ones_bd)
    prev_ref[...] = z[rows - SUBLANES:, :]

    tr = lax.broadcasted_iota(jnp.int32, (rows, rows), 0)
    tc = lax.broadcasted_iota(jnp.int32, (rows, rows), 1)
    chunk_tril = ((tr >> (c.bit_length() - 1)) == (tc >> (c.bit_length() - 1))) & (tc <= tr)
    chunk_tril = jnp.where(chunk_tril, 1.0, 0.0).astype(BF16)
    lw_hi = lw.astype(BF16)
    lw_r1 = lw - lw_hi.astype(F32)
    lw_mid = lw_r1.astype(BF16)
    lw_lo = (lw_r1 - lw_mid.astype(F32)).astype(BF16)
    cum = (jnp.dot(chunk_tril, lw_hi, preferred_element_type=F32)
           + jnp.dot(chunk_tril, lw_mid, preferred_element_type=F32)
           + jnp.dot(chunk_tril, lw_lo, preferred_element_type=F32))
    r_k, lnx_g, lnx_b = par_ref[4:5, :], par_ref[5:6, :], par_ref[6:7, :]

    at, rt, bt, kt, bh, kh, vv, p_end = {}, {}, {}, {}, {}, {}, {}, {}
    for ch in range(chunks):
        rs = slice(ch * c, (ch + 1) * c)
        cum_c = cum[rs]
        cum_end = cum_c[c - 1:c, :]
        e_inv = jnp.exp(-cum_c)
        e_end = jnp.exp(cum_end - cum_c)
        at_c = na_all[rs] * jnp.exp(cum_c - lw[rs])
        rt_c = r_all[rs] * jnp.exp(cum_c)
        b_c, kp_c = b_all[rs], kp_all[rs]
        bt_c, kt_c, bh_c, kh_c = b_c * e_inv, kp_c * e_inv, b_c * e_end, kp_c * e_end
        pe_c = jnp.exp(cum_end)
        for p in range(pairs):
            ls = slice(p * pw, (p + 1) * pw)
            u = (ch, p)
            at[u], rt[u], bt[u], kt[u], bh[u], kh[u] = at_c[:, ls], rt_c[:, ls], bt_c[:, ls], kt_c[:, ls], bh_c[:, ls], kh_c[:, ls]
            vv[u], p_end[u] = v_all[rs, ls], pe_c[:, ls]

    ar = {u: jnp.concatenate([at[u], rt[u]], axis=0) for u in units}
    g = {u: _bdot(ar[u], jnp.concatenate([stack(bt[u]), stack(kt[u])], axis=0), _NT)
         for u in units}
    a_ab = {u: jnp.where(tril_str, g[u][:c, :pw], 0.0) for u in units}
    a_rb = {u: jnp.where(tril_inc, g[u][c:, :pw], 0.0) for u in units}
    a_ak = {u: jnp.where(tril_str, g[u][:c, pw:], 0.0) for u in units}
    a_rk = {u: jnp.where(tril_inc, g[u][c:, pw:], 0.0) for u in units}
    av = {u: _bdot(jnp.concatenate([a_ak[u], a_rk[u]], axis=0), stack(vv[u])) for u in units}
    ak_v = {u: av[u][:c] for u in units}
    inv = {u: jnp.where(eye, 1.0, jnp.where(level_masks[0], a_ab[u], 0.0)) for u in units}
    for mask in level_masks[1:]:
        e = {u: _bdot(jnp.where(mask, a_ab[u], 0.0), stack(inv[u])) for u in units}
        inv = {u: inv[u] + _bdot(inv[u], stack(e[u])) for u in units}
    wu = {u: _bdot(inv[u], jnp.concatenate([stack(at[u]), stack(ak_v[u])], axis=1)) for u in units}
    w_mat = {u: wu[u][:, :pw] for u in units}
    u_loc = {u: wu[u][:, pw:] for u in units}
    wub = {u: _bdot(wu[u], bh[u], _TN) for u in units}
    m_c = {u: jnp.where(same_head, wub[u][:pw], 0.0) + jnp.where(eye2, p_end[u], 0.0) for u in units}
    n_c = {u: jnp.where(same_head, wub[u][pw:] + _bdot(vv[u], kh[u], _TN), 0.0) for u in units}
    qy = {u: _bdot(a_rb[u], jnp.concatenate([stack(w_mat[u]), stack(u_loc[u])], axis=1)) for u in units}
    q_c = {u: rt[u] + qy[u][:, :pw] for u in units}
    y_loc = {u: qy[u][:, pw:] + av[u][c:] for u in units}

    state = [s_ref[p] for p in range(pairs)]
    y = {}
    for ch in range(chunks):
        for p in range(pairs):
            u = (ch, p)
            y[u] = _bdot(q_c[u], state[p], _NT) + y_loc[u]
            state[p] = _bdot(state[p], m_c[u]) + n_c[u]
    for p in range(pairs):
        s_ref[p] = state[p]

    inv_n = 1.0 / n
    mu = {u: _bdot(y[u], ones_bd) * inv_n for u in units}
    var = {u: _bdot(jnp.square(y[u] - mu[u]), ones_bd) * inv_n for u in units}
    for (ch, p) in units:
        u = (ch, p)
        rs, ls = slice(ch * c, (ch + 1) * c), slice(p * pw, (p + 1) * pw)
        yn = (y[u] - mu[u]) * lax.rsqrt(var[u] + RWKV_GN_EPS) * lnx_g[:, ls] + lnx_b[:, ls]
        bonus = _bdot(r_all[rs, ls] * kp_all[rs, ls] * r_k[:, ls], ones_bd) * vv[u]
        o_ref[rs, ls] = ((yn + bonus) * g_all[rs, ls]).astype(o_ref.dtype)


def _rwkv_scan(x2d, w_r, mix, wl, par, *, batch, seq_len):
    c = RWKV_CHUNK
    chunks = RWKV_SCAN_CHUNKS
    rows = chunks * c
    dm, zc = w_r.shape
    assert seq_len % rows == 0
    const = lambda a: pl.BlockSpec(a.shape, lambda bi, ci: (0, 0), pipeline_mode=pl.Buffered(1))
    out = pl.pallas_call(
        functools.partial(_rwkv_chunk_kernel, chunks=chunks),
        grid=(batch, seq_len // rows),
        in_specs=[pl.BlockSpec((None, rows, dm), lambda bi, ci: (bi, ci, 0)),
                  const(w_r), const(mix), const(wl), const(par)],
        out_specs=pl.BlockSpec((None, rows, RWKV_DIM), lambda bi, ci: (bi, ci, 0)),
        out_shape=jax.ShapeDtypeStruct((batch, seq_len, RWKV_DIM), BF16),
        scratch_shapes=[pltpu.VMEM((RWKV_HEADS // 2, 2 * RWKV_HEAD_DIM, 2 * RWKV_HEAD_DIM), F32),
                        pltpu.VMEM((SUBLANES, zc), F32)],
        compiler_params=_params("parallel", "arbitrary"),
        name="rwkv_scan",
    )(x2d.reshape(batch, seq_len, dm), w_r, mix, wl, par)
    return out.reshape(batch * seq_len, RWKV_DIM)


def _moba_kernel(q_ref, k_ref, v_ref, pool_ref, o_ref, kext_ref, s_ref):
    blk, dh = MOBA_BLOCK, MOBA_HEAD_DIM
    t_len = k_ref.shape[0]
    nb = t_len // blk
    nbp = -(-nb // SUBLANES) * SUBLANES
    shift = blk.bit_length() - 1
    key_tile = 2 * blk

    lane = lax.broadcasted_iota(jnp.int32, (blk, LANES), 1)
    for j in range(nb):
        rs = slice(j * blk, (j + 1) * blk)
        kext_ref[rs, 0:dh] = k_ref[rs, :]
        kext_ref[rs, dh:dh + LANES] = jnp.where(lane == j, 1.0, 0.0).astype(BF16)

    km = jnp.dot(pool_ref[...], k_ref[...], preferred_element_type=F32)[0:nbp]
    km_hi = km.astype(BF16)
    km_lo = (km - km_hi.astype(F32)).astype(BF16)
    q_all = q_ref[...]
    gate = (lax.dot_general(km_hi, q_all, _NT, preferred_element_type=F32)
            + lax.dot_general(km_lo, q_all, _NT, preferred_element_type=F32))
    bidx = lax.broadcasted_iota(jnp.int32, gate.shape, 0)
    qblk = lax.broadcasted_iota(jnp.int32, gate.shape, 1) >> shift
    past = bidx < qblk
    gate = jnp.where(past, gate, NEG_INF)
    rank = jnp.zeros(gate.shape, jnp.int32)
    for jp in range(nb):
        other = gate[jp:jp + 1, :]
        rank = rank + ((other > gate) | ((other == gate) & (jp < bidx))).astype(jnp.int32)
    allowed = (past & (rank < MOBA_TOPK)) | (bidx == qblk)
    bias_t = jnp.where(allowed, 0.0, NEG_INF).astype(BF16)
    bias_t = jnp.concatenate([bias_t, jnp.zeros((LANES - nbp, t_len), BF16)], axis=0)
    r2 = lax.broadcasted_iota(jnp.int32, (LANES, LANES), 0)
    c2 = lax.broadcasted_iota(jnp.int32, (LANES, LANES), 1)
    eye = jnp.where(r2 == c2, 1.0, 0.0).astype(BF16)
    row = lax.broadcasted_iota(jnp.int32, (blk, blk), 0)
    col = lax.broadcasted_iota(jnp.int32, (blk, blk), 1)
    causal = col <= row
    colmax = lambda x: functools.reduce(jnp.maximum, [x[:, a:a + LANES] for a in range(0, x.shape[1], LANES)])
    colsum = lambda x: functools.reduce(jnp.add, [x[:, a:a + LANES] for a in range(0, x.shape[1], LANES)])

    for i in range(nb):
        qs = slice(i * blk, (i + 1) * blk)
        bias = lax.dot_general(bias_t[:, qs], eye, _TN, preferred_element_type=F32)
        q_ext = jnp.concatenate([q_ref[qs, :], bias.astype(BF16)], axis=1)
        nk = (i + 1) * blk
        tiles = [(a, min(a + key_tile, nk)) for a in range(0, nk, key_tile)]
        m_lane = None
        for a, e in tiles:
            s = lax.dot_general(q_ext, kext_ref[a:e, :], _NT, preferred_element_type=F32)
            if e == nk:
                s_own = jnp.where(causal, s[:, e - a - blk:], NEG_INF)
                s = s_own if e - a == blk else jnp.concatenate([s[:, :e - a - blk], s_own], axis=1)
            s_ref[:, a:e] = s
            part = colmax(s)
            m_lane = part if m_lane is None else jnp.maximum(m_lane, part)
        m = jnp.max(m_lane, axis=-1, keepdims=True)
        l_lane = None
        acc = None
        for a, e in tiles:
            p = jnp.exp(s_ref[:, a:e] - m)
            part = colsum(p)
            l_lane = part if l_lane is None else l_lane + part
            pv = jnp.dot(p.astype(BF16), v_ref[a:e, :], preferred_element_type=F32)
            acc = pv if acc is None else acc + pv
        l = jnp.sum(l_lane, axis=-1, keepdims=True)
        o_ref[qs, :] = (acc / l).astype(o_ref.dtype)


def _moba(qkv, *, batch, seq_len):
    blk, dh, nh = MOBA_BLOCK, MOBA_HEAD_DIM, MOBA_HEADS
    assert seq_len % blk == 0 and seq_len // blk <= LANES and dh == LANES
    pool = (jnp.arange(LANES)[:, None] == (jnp.arange(seq_len) // blk)[None, :]).astype(BF16) / blk
    qkv3 = qkv.reshape(batch, seq_len, 3 * MOBA_WIDTH)
    head = lambda off: pl.BlockSpec((None, seq_len, dh), lambda b, h: (b, 0, off + h))
    out = pl.pallas_call(
        _moba_kernel,
        grid=(batch, nh),
        in_specs=[head(0), head(nh), head(2 * nh), pl.BlockSpec((LANES, seq_len), lambda b, h: (0, 0))],
        out_specs=head(0),
        out_shape=jax.ShapeDtypeStruct((batch, seq_len, MOBA_WIDTH), BF16),
        scratch_shapes=[pltpu.VMEM((seq_len, dh + LANES), BF16), pltpu.VMEM((blk, seq_len), F32)],
        compiler_params=_params("parallel", "parallel"),
        name="moba",
    )(qkv3, qkv3, qkv3, pool)
    return out.reshape(batch * seq_len, MOBA_WIDTH)


def _residue_perm(d):
    p = DIL_PERM_ROWS
    idx = jnp.arange(p)
    nat = (idx % (p // d)) * d + idx // (p // d)
    return (nat[:, None] == idx[None, :]).astype(BF16)


def _dilated_proj_kernel(x_ref, w_ref, tab_ref, *rest, dilations):
    n = len(dilations)
    p_refs, o_refs, xb_ref = rest[:n], rest[n:2 * n], rest[2 * n]
    j = pl.program_id(1)

    @pl.when(j == 0)
    def _():
        xb_ref[...] = x_ref[...].astype(BF16)

    tm = x_ref.shape[0]
    grp = DIL_PERM_ROWS
    nat_ref = o_refs[0]
    for r0 in range(0, tm, grp):
        rs = slice(r0, r0 + grp)
        acc = jnp.dot(xb_ref[rs, :], w_ref[...], preferred_element_type=F32)
        t_z, t_partner = tab_ref[0, rs, :], tab_ref[1, rs, :]
        for hb in range(acc.shape[1] // LANES):
            z = acc[:, hb * LANES:(hb + 1) * LANES]
            nat_ref[0, rs, hb * LANES:(hb + 1) * LANES] = (
                z * t_z + pltpu.roll(z, LANES // 2, 1) * t_partner).astype(nat_ref.dtype)

    for g in range(1, n):
        d = dilations[g]
        w = grp // d

        @pl.when((j == g) | (j >= n))
        def _(g=g, d=d, w=w):
            for gi in range(tm // grp):
                y = jnp.dot(p_refs[g][...], nat_ref[0, gi * grp:(gi + 1) * grp, :],
                            preferred_element_type=F32).astype(o_refs[g].dtype)
                for r in range(d):
                    o_refs[g][r, gi * w:(gi + 1) * w, :] = y[r * w:(r + 1) * w, :]


def _dilated_proj(x2d, w, tabs, dilations, *, batch, seq_len):
    m, k = x2d.shape
    n = len(dilations)
    tm = min(1024, seq_len)
    assert dilations[0] == 1 and w.shape[1] == (n + 2) * C_WIDTH
    assert m % tm == 0 and seq_len % tm == 0 and tm % DIL_PERM_ROWS == 0
    assert all(DIL_PERM_ROWS % (d * 2 * SUBLANES) == 0 for d in dilations[1:])
    bps = seq_len // tm
    perms = [_residue_perm(d) for d in dilations]

    def tab_map(i, j):
        return ((j >= n).astype(jnp.int32) + (j > n).astype(jnp.int32), 0, i % bps, 0)

    def out_map(g):
        return lambda i, j: (i // bps, 0, i % bps, (j > g).astype(jnp.int32) + (j > n).astype(jnp.int32))

    outs = pl.pallas_call(
        functools.partial(_dilated_proj_kernel, dilations=tuple(dilations)),
        grid=(m // tm, n + 2),
        in_specs=[pl.BlockSpec((tm, k), lambda i, j: (i, 0)),
                  pl.BlockSpec((k, C_WIDTH), lambda i, j: (0, j)),
                  pl.BlockSpec((None, 2, tm, LANES), tab_map)]
                 + [pl.BlockSpec(p.shape, lambda i, j: (0, 0)) for p in perms],
        out_specs=[pl.BlockSpec((None, d, tm // d, C_WIDTH), out_map(g)) for g, d in enumerate(dilations)],
        out_shape=[jax.ShapeDtypeStruct((batch, d, seq_len // d, 3 * C_WIDTH), BF16) for d in dilations],
        scratch_shapes=[pltpu.VMEM((tm, k), BF16)],
        compiler_params=_params("parallel", "arbitrary"),
        name="dilated_proj",
    )(x2d, w, tabs, *perms)
    return outs


def _dilated_kernel(q_ref, k_ref, kp_ref, v_ref, vp_ref, o_ref, lse_ref):
    w, dh = DIL_WINDOW, DIL_HEAD_DIM
    tiles = q_ref.shape[0] // w
    has_prev = pl.program_id(2) > 0
    row = lax.broadcasted_iota(jnp.int32, (w, 2 * w), 0)
    col = lax.broadcasted_iota(jnp.int32, (w, 2 * w), 1)
    band = (col >= row) & (col <= row + w)
    first_band = band & (has_prev | (col >= w))
    er = lax.broadcasted_iota(jnp.int32, (w, w), 0)
    ec = lax.broadcasted_iota(jnp.int32, (w, w), 1)
    eye = er == ec
    units = [(h, t) for h in range(DIL_HEADS) for t in range(tiles)]

    def banded(ref, halo_ref, h, t):
        hs = slice(h * dh, (h + 1) * dh)
        if t == 0:
            return jnp.concatenate([halo_ref[:, hs], ref[0:w, hs]], axis=0)
        return ref[(t - 1) * w:(t + 1) * w, hs]

    s = {}
    for h, t in units:
        q = q_ref[t * w:(t + 1) * w, h * dh:(h + 1) * dh]
        sc = lax.dot_general(q, banded(k_ref, kp_ref, h, t), _NT, preferred_element_type=F32)
        s[h, t] = jnp.where(first_band if t == 0 else band, sc, NEG_INF)
    m = {u: jnp.max(jnp.maximum(s[u][:, :w], s[u][:, w:]), axis=-1, keepdims=True) for u in units}
    p = {u: jnp.exp(s[u] - m[u]) for u in units}
    l = {u: jnp.sum(p[u][:, :w] + p[u][:, w:], axis=-1, keepdims=True) for u in units}
    o = {(h, t): jnp.dot(p[h, t].astype(BF16), banded(v_ref, vp_ref, h, t), preferred_element_type=F32)
         for h, t in units}
    for h, t in units:
        u = (h, t)
        o_ref[t * w:(t + 1) * w, h * dh:(h + 1) * dh] = (o[u] / l[u]).astype(o_ref.dtype)
        lse = m[u] + jnp.log(l[u])
        lse_ref[t, h:h + 1, :] = jnp.sum(jnp.where(eye, lse, 0.0), axis=0, keepdims=True)


def _dilated_group(src, cols, *, seq_len):
    batch, d, length, _ = src.shape
    w = DIL_WINDOW
    rows = min(DIL_ROWS, length)
    assert length % rows == 0 and rows % w == 0 and length * d == seq_len
    tiles = rows // w
    qc, kc, vc = cols
    blk = lambda cb: pl.BlockSpec((None, None, rows, C_WIDTH), lambda b, r, i: (b, r, i, cb))
    halo = lambda cb: pl.BlockSpec(
        (None, None, w, C_WIDTH), lambda b, r, i: (b, r, jnp.maximum(i * tiles - 1, 0), cb))
    o, lse = pl.pallas_call(
        _dilated_kernel,
        grid=(batch, d, length // rows),
        in_specs=[blk(qc), blk(kc), halo(kc), blk(vc), halo(vc)],
        out_specs=[
            pl.BlockSpec((None, None, rows, C_WIDTH), lambda b, r, i: (b, r, i, 0)),
            pl.BlockSpec((None, None, tiles, DIL_HEADS, w), lambda b, r, i: (b, r, i, 0, 0)),
        ],
        out_shape=[
            jax.ShapeDtypeStruct((batch, d, length, C_WIDTH), BF16),
            jax.ShapeDtypeStruct((batch, d, length // w, DIL_HEADS, w), F32),
        ],
        compiler_params=_params("parallel", "parallel", "arbitrary"),
        name=f"dilated_d{d}",
    )(src, src, src, src, src)
    lse = lse.transpose(0, 2, 4, 1, 3).reshape(batch, seq_len, DIL_HEADS)
    return o, lse


def _residual_mlp(x, proj, ln1_g, ln1_b, w1_ref, w2_ref, ln2_g, ln2_b):
    x1 = _layer_norm(DEEPNORM_ALPHA * x + proj, ln1_g, ln1_b)
    xb = x1.astype(BF16)
    hs = []
    for c0 in range(0, w1_ref.shape[1], MLP_HIDDEN_CHUNK):
        h = jnp.dot(xb, w1_ref[:, c0:c0 + MLP_HIDDEN_CHUNK], preferred_element_type=F32)
        hs.append(jnp.square(jnp.maximum(h, 0.0)).astype(BF16))
    y = jnp.dot(jnp.concatenate(hs, axis=1), w2_ref[...], preferred_element_type=F32)
    return _layer_norm(DEEPNORM_ALPHA * x1 + y, ln2_g, ln2_b)


def _outproj_mlp_kernel(*refs, n_act):
    acts = refs[:n_act]
    w_ref, x_ref, g1_ref, b1_ref, w1_ref, w2_ref, g2_ref, b2_ref, o_ref = refs[n_act:]
    proj = None
    off = 0
    for a_ref in acts:
        kw = a_ref.shape[1]
        part = jnp.dot(a_ref[...], w_ref[off:off + kw, :], preferred_element_type=F32)
        proj = part if proj is None else proj + part
        off += kw
    o_ref[...] = _residual_mlp(x_ref[...], proj, g1_ref[...], b1_ref[...], w1_ref, w2_ref,
                               g2_ref[...], b2_ref[...])


def _outproj_mlp(acts, w, x2d, ln1, mlp, ln2):
    m, dm = x2d.shape
    tm = LAYER_ROWS
    assert m % tm == 0 and mlp[0].shape[1] % MLP_HIDDEN_CHUNK == 0
    rowspec = lambda a: pl.BlockSpec((tm, a.shape[1]), lambda i: (i, 0))
    const = lambda a: pl.BlockSpec(a.shape, lambda i: (0, 0), pipeline_mode=pl.Buffered(1))
    return pl.pallas_call(
        functools.partial(_outproj_mlp_kernel, n_act=len(acts)),
        grid=(m // tm,),
        in_specs=([rowspec(a) for a in acts] + [const(w), rowspec(x2d)]
                  + [const(a) for a in (*ln1, *mlp, *ln2)]),
        out_specs=pl.BlockSpec((tm, dm), lambda i: (i, 0)),
        out_shape=jax.ShapeDtypeStruct((m, dm), F32),
        compiler_params=_params("parallel"),
        name="outproj_mlp",
    )(*acts, w, x2d, *ln1, *mlp, *ln2)


def _merge_outproj_mlp_kernel(*refs, dilations):
    n = len(dilations)
    o_refs, lse_refs, p_refs = refs[:n], refs[n:2 * n], refs[2 * n:3 * n]
    w_ref, x_ref, g1_ref, b1_ref, w1_ref, w2_ref, g2_ref, b2_ref, out_ref = refs[3 * n:]
    nh, dh, grp = DIL_HEADS, DIL_HEAD_DIM, DIL_PERM_ROWS
    tm = x_ref.shape[0]
    branch = []
    for o_ref, p_ref, d in zip(o_refs, p_refs, dilations):
        if d == 1:
            branch.append(o_ref[0].astype(F32))
            continue
        w = grp // d
        nat = []
        for gi in range(tm // grp):
            perm = jnp.concatenate([o_ref[r, gi * w:(gi + 1) * w, :] for r in range(d)], axis=0)
            nat.append(lax.dot_general(p_ref[...], perm, _TN, preferred_element_type=F32))
        branch.append(jnp.concatenate(nat, axis=0))
    parts = [ref[...] for ref in lse_refs]
    m = functools.reduce(jnp.maximum, parts)
    es = [jnp.exp(p - m) for p in parts]
    den = functools.reduce(jnp.add, es)
    wts = [e / den for e in es]
    heads = []
    for h in range(nh):
        hs = slice(h * dh, (h + 1) * dh)
        heads.append(functools.reduce(jnp.add, [wts[g][:, h:h + 1] * branch[g][:, hs] for g in range(n)]))
    proj = _bdot(jnp.concatenate(heads, axis=1), w_ref[...])
    out_ref[...] = _residual_mlp(x_ref[...], proj, g1_ref[...], b1_ref[...], w1_ref, w2_ref,
                                 g2_ref[...], b2_ref[...])


def _merge_outproj_mlp(outs, lses, dilations, w, x2d, ln1, mlp, ln2, *, batch, seq_len):
    dm = x2d.shape[1]
    tm = LAYER_ROWS
    assert seq_len % tm == 0 and tm % DIL_PERM_ROWS == 0 and mlp[0].shape[1] % MLP_HIDDEN_CHUNK == 0
    x3 = x2d.reshape(batch, seq_len, dm)
    perms = [_residue_perm(d) for d in dilations]
    const = lambda a: pl.BlockSpec(a.shape, lambda bi, i: (0,) * a.ndim, pipeline_mode=pl.Buffered(1))
    rows3 = lambda a: pl.BlockSpec((None, tm, a.shape[2]), lambda bi, i: (bi, i, 0))
    in_specs = ([pl.BlockSpec((None, d, tm // d, C_WIDTH), lambda bi, i: (bi, 0, i, 0)) for d in dilations]
                + [rows3(a) for a in lses] + [const(p) for p in perms] + [const(w), rows3(x3)]
                + [const(a) for a in (*ln1, *mlp, *ln2)])
    out = pl.pallas_call(
        functools.partial(_merge_outproj_mlp_kernel, dilations=tuple(dilations)),
        grid=(batch, seq_len // tm),
        in_specs=in_specs,
        out_specs=pl.BlockSpec((None, tm, dm), lambda bi, i: (bi, i, 0)),
        out_shape=jax.ShapeDtypeStruct((batch, seq_len, dm), F32),
        compiler_params=_params("parallel", "parallel"),
        name="merge_outproj_mlp",
    )(*outs, *lses, *perms, w, x3, *ln1, *mlp, *ln2)
    return out.reshape(batch * seq_len, dm)


def _rwkv_moba_layer(x2d, w_in, shift_mix, w0, w_up, a0, a_up, g_up, k_k, k_a, r_k, lnx_g, lnx_b,
                     w_out, ln1, mlp, ln2, *, batch, seq_len):
    d = RWKV_DIM
    tabs = _rotary_tables(seq_len, MOBA_HEAD_DIM ** -0.5)
    w_m = w_in[:, RWKV_PROJ_COLS:]
    w_m = jnp.concatenate([_rotary_weight(w_m[:, :2 * MOBA_WIDTH], 2 * MOBA_HEADS),
                           w_m[:, 2 * MOBA_WIDTH:]], axis=1)
    kinds = [0] * MOBA_HEADS + [1] * MOBA_HEADS + [None] * MOBA_HEADS
    qkv = _rotary_proj(x2d, w_m.astype(BF16), tabs, kinds, seq_len=seq_len)
    y_b = _moba(qkv, batch=batch, seq_len=seq_len)
    pad = RWKV_Z_COLS - RWKV_PROJ_COLS
    w_r = jnp.pad(w_in[:, :RWKV_PROJ_COLS], ((0, 0), (0, pad))).astype(BF16)
    mix = jnp.pad(shift_mix, (0, pad))[None, :]
    wl = jnp.zeros((LORA_PAD, 3 * d), F32)
    wl = wl.at[0:DECAY_LORA, 0:d].set(w_up)
    wl = wl.at[DECAY_LORA:DECAY_LORA + ICLR_LORA, d:2 * d].set(a_up)
    wl = wl.at[DECAY_LORA + ICLR_LORA:LORA_COLS, 2 * d:3 * d].set(g_up).astype(BF16)
    par = jnp.stack([w0, a0, k_k, k_a, r_k.reshape(d), lnx_g, lnx_b, jnp.zeros((d,), F32)])
    y_a = _rwkv_scan(x2d, w_r, mix, wl, par, batch=batch, seq_len=seq_len)
    return _outproj_mlp([y_a, y_b], w_out.astype(BF16), x2d, ln1, mlp, ln2)


def _dilated_layer(x2d, w_in, w_out, ln1, mlp, ln2, *, batch, seq_len):
    tabs = _rotary_tables(seq_len, DIL_HEAD_DIM ** -0.5)
    nqk = (DIL_GROUPS + 1) * C_WIDTH
    w_c = jnp.concatenate([_rotary_weight(w_in[:, :nqk], (DIL_GROUPS + 1) * DIL_HEADS), w_in[:, nqk:]], axis=1)
    dilations = [dilation for _, dilation in DIL_PAIRS]
    assert all(span // dilation == DIL_WINDOW for span, dilation in DIL_PAIRS)
    srcs = _dilated_proj(x2d, w_c.astype(BF16), tabs, dilations, batch=batch, seq_len=seq_len)
    outs, lses = [], []
    for src in srcs:
        o, lse = _dilated_group(src, (0, 1, 2), seq_len=seq_len)
        outs.append(o)
        lses.append(lse)
    return _merge_outproj_mlp(outs, lses, dilations, w_out.astype(BF16), x2d, ln1, mlp, ln2,
                              batch=batch, seq_len=seq_len)


def kernel(x, ab_w_in, ab_shift_mix, ab_w0, ab_w_up, ab_a0, ab_a_up, ab_g_up, ab_k_k, ab_k_a, ab_r_k,
           ab_lnx_g, ab_lnx_b, ab_w_out, c_w_in, c_w_out, ln1_g, ln1_b, mlp_w1, mlp_w2, ln2_g, ln2_b):
    batch, seq_len, dm = x.shape
    assert dm == D_MODEL
    h = x.reshape(batch * seq_len, dm)
    depth = ln1_g.shape[0]
    for layer in range(depth):
        j = layer // 2
        ln1 = (ln1_g[layer][None, :], ln1_b[layer][None, :])
        ln2 = (ln2_g[layer][None, :], ln2_b[layer][None, :])
        mlp = (mlp_w1[layer].astype(BF16), mlp_w2[layer].astype(BF16))
        if layer % 2 == 0:
            h = _rwkv_moba_layer(h, ab_w_in[j], ab_shift_mix[j], ab_w0[j], ab_w_up[j], ab_a0[j], ab_a_up[j],
                                 ab_g_up[j], ab_k_k[j], ab_k_a[j], ab_r_k[j], ab_lnx_g[j], ab_lnx_b[j],
                                 ab_w_out[j], ln1, mlp, ln2, batch=batch, seq_len=seq_len)
        else:
            h = _dilated_layer(h, c_w_in[j], c_w_out[j], ln1, mlp, ln2, batch=batch, seq_len=seq_len)
    return h.reshape(batch, seq_len, dm)
```

```python
import functools

import jax
import jax.numpy as jnp
from jax import lax
from jax.experimental import pallas as pl
from jax.experimental.pallas import tpu as pltpu

F32 = jnp.float32
BF16 = jnp.bfloat16

D_MODEL = 1024
DEPTH = 2
DEEPNORM_ALPHA = (2 * DEPTH) ** 0.25
LN_EPS = 1e-5
ROPE_THETA = 500000.0
ROPE_FRACTION = 4
RWKV_HEAD_DIM = 64
RWKV_DIM = D_MODEL // 2
RWKV_HEADS = RWKV_DIM // RWKV_HEAD_DIM
DECAY_LORA = 64
ICLR_LORA = 64
GATE_LORA = 160
RWKV_GN_EPS = 64e-5
RWKV_PROJ_COLS = 3 * RWKV_DIM + DECAY_LORA + ICLR_LORA + GATE_LORA
MOBA_HEAD_DIM = 128
MOBA_WIDTH = D_MODEL - RWKV_DIM
MOBA_HEADS = MOBA_WIDTH // MOBA_HEAD_DIM
MOBA_BLOCK = 256
MOBA_TOPK = 3
DIL_PAIRS = ((128, 1), (512, 4), (2048, 16))
DIL_GROUPS = len(DIL_PAIRS)
DIL_HEAD_DIM = 128
DIL_HEADS = D_MODEL // DIL_HEAD_DIM
C_WIDTH = DIL_HEADS * DIL_HEAD_DIM
C_IN_COLS = DIL_GROUPS * C_WIDTH + 2 * C_WIDTH
MLP_HIDDEN = 4 * D_MODEL
NEG_INF = -1e30

LANES = 128
SUBLANES = 8
VMEM_LIMIT_BYTES = 56 * 1024 * 1024

LORA_COLS = DECAY_LORA + ICLR_LORA + GATE_LORA
LORA_PAD = -(-LORA_COLS // LANES) * LANES
RWKV_Z_COLS = 3 * RWKV_DIM + LORA_PAD
RWKV_CHUNK = 64
RWKV_SCAN_CHUNKS = 4
DIL_WINDOW = 128
DIL_ROWS = 256
DIL_PERM_ROWS = 256
PROJ_ROW_CHUNK = 256
LAYER_ROWS = 512
MLP_HIDDEN_CHUNK = 1024

_NN = (((1,), (0,)), ((), ()))
_NT = (((1,), (1,)), ((), ()))
_TN = (((0,), (0,)), ((), ()))


def _bdot(a, b, dims=_NN):
    return lax.dot_general(a.astype(BF16), b.astype(BF16), dims, preferred_element_type=F32)


def _layer_norm(y, g, b):
    mu = jnp.mean(y, axis=-1, keepdims=True)
    var = jnp.mean(jnp.square(y - mu), axis=-1, keepdims=True)
    return (y - mu) * lax.rsqrt(var + LN_EPS) * g + b


def _params(*sem):
    return pltpu.CompilerParams(dimension_semantics=sem, vmem_limit_bytes=VMEM_LIMIT_BYTES)


ROT_DIMS = MOBA_HEAD_DIM // ROPE_FRACTION
ROT_HALF = ROT_DIMS // 2


def _rotary_weight(w, n_heads):
    rest_a = LANES // 2 - ROT_HALF
    heads = []
    for h in range(n_heads):
        head = w[:, h * LANES:(h + 1) * LANES]
        rot, rest = head[:, :ROT_DIMS], head[:, ROT_DIMS:]
        heads += [rot[:, :ROT_HALF], rest[:, :rest_a], rot[:, ROT_HALF:], rest[:, rest_a:]]
    return jnp.concatenate(heads, axis=1)


def _rotary_tables(seq_len, q_scale):
    inv_freq = ROPE_THETA ** (-jnp.arange(ROT_HALF, dtype=F32) / ROT_HALF)
    ang = jnp.arange(seq_len).astype(F32)[:, None] * inv_freq[None, :]
    cos, sin = jnp.cos(ang), jnp.sin(ang)
    pad = lambda a, fill: jnp.concatenate([a, jnp.full((seq_len, LANES // 2 - ROT_HALF), fill, F32)], axis=1)
    c = jnp.concatenate([pad(cos, 1.0), pad(cos, 1.0)], axis=1)
    s = jnp.concatenate([pad(-sin, 0.0), pad(sin, 0.0)], axis=1)
    one = jnp.ones_like(c)
    return jnp.stack([jnp.stack([c * q_scale, s * q_scale]), jnp.stack([c, s]),
                      jnp.stack([one, jnp.zeros_like(c)])])


def _rotary_proj_kernel(x_ref, w_ref, tab_ref, o_ref, *, head_kinds):
    tm = x_ref.shape[0]
    chunk = min(PROJ_ROW_CHUNK, tm)
    for r0 in range(0, tm, chunk):
        rs = slice(r0, r0 + chunk)
        acc = jnp.dot(x_ref[rs, :].astype(BF16), w_ref[...], preferred_element_type=F32)
        for hb, kind in enumerate(head_kinds):
            z = acc[:, hb * LANES:(hb + 1) * LANES]
            if kind is not None:
                z = z * tab_ref[kind, 0, rs, :] + pltpu.roll(z, LANES // 2, 1) * tab_ref[kind, 1, rs, :]
            o_ref[rs, hb * LANES:(hb + 1) * LANES] = z.astype(o_ref.dtype)


def _rotary_proj(x2d, w, tabs, head_kinds, *, seq_len):
    m, k = x2d.shape
    n = w.shape[1]
    tm = min(1024, seq_len)
    assert m % tm == 0 and seq_len % tm == 0 and n == len(head_kinds) * LANES
    bps = seq_len // tm
    return pl.pallas_call(
        functools.partial(_rotary_proj_kernel, head_kinds=tuple(head_kinds)),
        grid=(m // tm,),
        in_specs=[pl.BlockSpec((tm, k), lambda i: (i, 0)),
                  pl.BlockSpec((k, n), lambda i: (0, 0), pipeline_mode=pl.Buffered(1)),
                  pl.BlockSpec((tabs.shape[0], 2, tm, LANES), lambda i: (0, 0, i % bps, 0))],
        out_specs=pl.BlockSpec((tm, n), lambda i: (i, 0)),
        out_shape=jax.ShapeDtypeStruct((m, n), BF16),
        compiler_params=_params("parallel"),
        name="rotary_proj",
    )(x2d, w, tabs)


def _softplus(x):
    return jnp.maximum(x, 0.0) + jnp.log(1.0 + jnp.exp(-jnp.abs(x)))


def _split_dot(x, ones_bd):
    hi = x.astype(BF16)
    lo = (x - hi.astype(F32)).astype(BF16)
    return (jnp.dot(hi, ones_bd, preferred_element_type=F32)
            + jnp.dot(lo, ones_bd, preferred_element_type=F32))


def _rwkv_prep(z, prev, mix, wl, par, ones_bd):
    d = RWKV_DIM
    pw = 2 * RWKV_HEAD_DIM
    row = lax.broadcasted_iota(jnp.int32, z.shape, 0)
    shifted = jnp.where(row == 0, prev, pltpu.roll(z, 1, 0))
    zz = z + (shifted - z) * mix
    r, k, v, lo = zz[:, 0:d], zz[:, d:2 * d], zz[:, 2 * d:3 * d], zz[:, 3 * d:]
    lane = lax.broadcasted_iota(jnp.int32, lo.shape, 1)
    act = jnp.where(lane < DECAY_LORA, jnp.tanh(lo),
                    jnp.where(lane < DECAY_LORA + ICLR_LORA, lo, jax.nn.sigmoid(lo)))
    lora = _bdot(act, wl)
    w0, a0, k_k, k_a = par[0:1, :], par[1:2, :], par[2:3, :], par[3:4, :]
    w = -_softplus(-(w0 + lora[:, 0:d])) - 0.5
    a = jax.nn.sigmoid(a0 + lora[:, d:2 * d])
    kk = k * k_k
    sq = kk * kk
    ssq = jnp.concatenate([_split_dot(sq[:, p0:p0 + pw], ones_bd) for p0 in range(0, d, pw)], axis=1)
    kk = kk / jnp.maximum(jnp.sqrt(ssq), 1e-12)
    return r, -jnp.exp(w), k * (1.0 + (a - 1.0) * k_a), v, -kk, kk * a, lora[:, 2 * d:3 * d]


def _rwkv_chunk_kernel(x_ref, wr_ref, mix_ref, wl_ref, par_ref, o_ref, s_ref, prev_ref, *, chunks):
    c, n = RWKV_CHUNK, RWKV_HEAD_DIM
    pw = 2 * n
    assert c == n and pw == LANES
    pairs = RWKV_HEADS // 2
    units = [(ch, p) for ch in range(chunks) for p in range(pairs)]

    first = pl.program_id(1) == 0

    @pl.when(first)
    def _():
        s_ref[...] = jnp.zeros_like(s_ref)

    row = lax.broadcasted_iota(jnp.int32, (c, pw), 0)
    col = lax.broadcasted_iota(jnp.int32, (c, pw), 1) & (n - 1)
    tril_inc = col <= row
    tril_str = col < row
    eye = col == row
    level_masks = [((row >> (lvl + 1)) == (col >> (lvl + 1))) & (((row >> lvl) & 1) == 1) & (((col >> lvl) & 1) == 0)
                   for lvl in range(c.bit_length() - 1)]
    r2 = lax.broadcasted_iota(jnp.int32, (pw, pw), 0)
    c2 = lax.broadcasted_iota(jnp.int32, (pw, pw), 1)
    same_head = (r2 < n) == (c2 < n)
    eye2 = r2 == c2
    ones_bd = jnp.where(same_head, 1.0, 0.0).astype(BF16)

    def stack(x):
        xb = x.astype(BF16)
        return jnp.where(same_head, jnp.concatenate([xb, xb], axis=0), jnp.zeros((), BF16))

    rows = chunks * c
    z = jnp.dot(x_ref[...].astype(BF16), wr_ref[...], preferred_element_type=F32)
    prev = jnp.where(first, 0.0, prev_ref[SUBLANES - 1:SUBLANES, :])
    r_all, lw, kp_all, v_all, na_all, b_all, g_all = _rwkv_prep(
        z, prev, mix_ref[...], wl_ref[...], par_ref[...], ones_bd)
    prev_ref[...] = z[rows - SUBLANES:, :]

    tr = lax.broadcasted_iota(jnp.int32, (rows, rows), 0)
    tc = lax.broadcasted_iota(jnp.int32, (rows, rows), 1)
    chunk_tril = ((tr >> (c.bit_length() - 1)) == (tc >> (c.bit_length() - 1))) & (tc <= tr)
    chunk_tril = jnp.where(chunk_tril, 1.0, 0.0).astype(BF16)
    lw_hi = lw.astype(BF16)
    lw_r1 = lw - lw_hi.astype(F32)
    lw_mid = lw_r1.astype(BF16)
    lw_lo = (lw_r1 - lw_mid.astype(F32)).astype(BF16)
    cum = (jnp.dot(chunk_tril, lw_hi, preferred_element_type=F32)
           + jnp.dot(chunk_tril, lw_mid, preferred_element_type=F32)
           + jnp.dot(chunk_tril, lw_lo, preferred_element_type=F32))
    r_k, lnx_g, lnx_b = par_ref[4:5, :], par_ref[5:6, :], par_ref[6:7, :]

    at, rt, bt, kt, bh, kh, vv, p_end = {}, {}, {}, {}, {}, {}, {}, {}
    for ch in range(chunks):
        rs = slice(ch * c, (ch + 1) * c)
        cum_c = cum[rs]
        cum_end = cum_c[c - 1:c, :]
        e_inv = jnp.exp(-cum_c)
        e_end = jnp.exp(cum_end - cum_c)
        at_c = na_all[rs] * jnp.exp(cum_c - lw[rs])
        rt_c = r_all[rs] * jnp.exp(cum_c)
        b_c, kp_c = b_all[rs], kp_all[rs]
        bt_c, kt_c, bh_c, kh_c = b_c * e_inv, kp_c * e_inv, b_c * e_end, kp_c * e_end
        pe_c = jnp.exp(cum_end)
        for p in range(pairs):
            ls = slice(p * pw, (p + 1) * pw)
            u = (ch, p)
            at[u], rt[u], bt[u], kt[u], bh[u], kh[u] = at_c[:, ls], rt_c[:, ls], bt_c[:, ls], kt_c[:, ls], bh_c[:, ls], kh_c[:, ls]
            vv[u], p_end[u] = v_all[rs, ls], pe_c[:, ls]

    ar = {u: jnp.concatenate([at[u], rt[u]], axis=0) for u in units}
    g = {u: _bdot(ar[u], jnp.concatenate([stack(bt[u]), stack(kt[u])], axis=0), _NT)
         for u in units}
    a_ab = {u: jnp.where(tril_str, g[u][:c, :pw], 0.0) for u in units}
    a_rb = {u: jnp.where(tril_inc, g[u][c:, :pw], 0.0) for u in units}
    a_ak = {u: jnp.where(tril_str, g[u][:c, pw:], 0.0) for u in units}
    a_rk = {u: jnp.where(tril_inc, g[u][c:, pw:], 0.0) for u in units}
    av = {u: _bdot(jnp.concatenate([a_ak[u], a_rk[u]], axis=0), stack(vv[u])) for u in units}
    ak_v = {u: av[u][:c] for u in units}
    inv = {u: jnp.where(eye, 1.0, jnp.where(level_masks[0], a_ab[u], 0.0)) for u in units}
    for mask in level_masks[1:]:
        e = {u: _bdot(jnp.where(mask, a_ab[u], 0.0), stack(inv[u])) for u in units}
        inv = {u: inv[u] + _bdot(inv[u], stack(e[u])) for u in units}
    wu = {u: _bdot(inv[u], jnp.concatenate([stack(at[u]), stack(ak_v[u])], axis=1)) for u in units}
    w_mat = {u: wu[u][:, :pw] for u in units}
    u_loc = {u: wu[u][:, pw:] for u in units}
    wub = {u: _bdot(wu[u], bh[u], _TN) for u in units}
    m_c = {u: jnp.where(same_head, wub[u][:pw], 0.0) + jnp.where(eye2, p_end[u], 0.0) for u in units}
    n_c = {u: jnp.where(same_head, wub[u][pw:] + _bdot(vv[u], kh[u], _TN), 0.0) for u in units}
    qy = {u: _bdot(a_rb[u], jnp.concatenate([stack(w_mat[u]), stack(u_loc[u])], axis=1)) for u in units}
    q_c = {u: rt[u] + qy[u][:, :pw] for u in units}
    y_loc = {u: qy[u][:, pw:] + av[u][c:] for u in units}

    state = [s_ref[p] for p in range(pairs)]
    y = {}
    for ch in range(chunks):
        for p in range(pairs):
            u = (ch, p)
            y[u] = _bdot(q_c[u], state[p], _NT) + y_loc[u]
            state[p] = _bdot(state[p], m_c[u]) + n_c[u]
    for p in range(pairs):
        s_ref[p] = state[p]

    inv_n = 1.0 / n
    mu = {u: _bdot(y[u], ones_bd) * inv_n for u in units}
    var = {u: _bdot(jnp.square(y[u] - mu[u]), ones_bd) * inv_n for u in units}
    for (ch, p) in units:
        u = (ch, p)
        rs, ls = slice(ch * c, (ch + 1) * c), slice(p * pw, (p + 1) * pw)
        yn = (y[u] - mu[u]) * lax.rsqrt(var[u] + RWKV_GN_EPS) * lnx_g[:, ls] + lnx_b[:, ls]
        bonus = _bdot(r_all[rs, ls] * kp_all[rs, ls] * r_k[:, ls], ones_bd) * vv[u]
        o_ref[rs, ls] = ((yn + bonus) * g_all[rs, ls]).astype(o_ref.dtype)


def _rwkv_scan(x2d, w_r, mix, wl, par, *, batch, seq_len):
    c = RWKV_CHUNK
    chunks = RWKV_SCAN_CHUNKS
    rows = chunks * c
    dm, zc = w_r.shape
    assert seq_len % rows == 0
    const = lambda a: pl.BlockSpec(a.shape, lambda bi, ci: (0, 0), pipeline_mode=pl.Buffered(1))
    out = pl.pallas_call(
        functools.partial(_rwkv_chunk_kernel, chunks=chunks),
        grid=(batch, seq_len // rows),
        in_specs=[pl.BlockSpec((None, rows, dm), lambda bi, ci: (bi, ci, 0)),
                  const(w_r), const(mix), const(wl), const(par)],
        out_specs=pl.BlockSpec((None, rows, RWKV_DIM), lambda bi, ci: (bi, ci, 0)),
        out_shape=jax.ShapeDtypeStruct((batch, seq_len, RWKV_DIM), BF16),
        scratch_shapes=[pltpu.VMEM((RWKV_HEADS // 2, 2 * RWKV_HEAD_DIM, 2 * RWKV_HEAD_DIM), F32),
                        pltpu.VMEM((SUBLANES, zc), F32)],
        compiler_params=_params("parallel", "arbitrary"),
        name="rwkv_scan",
    )(x2d.reshape(batch, seq_len, dm), w_r, mix, wl, par)
    return out.reshape(batch * seq_len, RWKV_DIM)


def _moba_kernel(q_ref, k_ref, v_ref, pool_ref, o_ref, kext_ref, vext_ref, s_ref):
    blk, dh = MOBA_BLOCK, MOBA_HEAD_DIM
    t_len = k_ref.shape[0]
    nb = t_len // blk
    nbp = -(-nb // SUBLANES) * SUBLANES
    shift = blk.bit_length() - 1
    key_tile = 2 * blk

    lane = lax.broadcasted_iota(jnp.int32, (blk, LANES), 1)
    for j in range(nb):
        rs = slice(j * blk, (j + 1) * blk)
        kext_ref[rs, 0:dh] = k_ref[rs, :]
        kext_ref[rs, dh:dh + LANES] = jnp.where(lane == j, 1.0, 0.0).astype(BF16)
        vext_ref[rs, 0:dh] = v_ref[rs, :]
        vext_ref[rs, dh:dh + LANES] = jnp.ones((blk, LANES), BF16)

    km = jnp.dot(pool_ref[...], k_ref[...], preferred_element_type=F32)[0:nbp]
    km_hi = km.astype(BF16)
    km_lo = (km - km_hi.astype(F32)).astype(BF16)
    q_all = q_ref[...]
    gate = (lax.dot_general(km_hi, q_all, _NT, preferred_element_type=F32)
            + lax.dot_general(km_lo, q_all, _NT, preferred_element_type=F32))
    bidx = lax.broadcasted_iota(jnp.int32, gate.shape, 0)
    qblk = lax.broadcasted_iota(jnp.int32, gate.shape, 1) >> shift
    past = bidx < qblk
    gate = jnp.where(past, gate, NEG_INF)
    rank = jnp.zeros(gate.shape, jnp.int32)
    for jp in range(nb):
        other = gate[jp:jp + 1, :]
        rank = rank + ((other > gate) | ((other == gate) & (jp < bidx))).astype(jnp.int32)
    allowed = (past & (rank < MOBA_TOPK)) | (bidx == qblk)
    bias_t = jnp.where(allowed, 0.0, NEG_INF).astype(BF16)
    bias_t = jnp.concatenate([bias_t, jnp.zeros((LANES - nbp, t_len), BF16)], axis=0)
    r2 = lax.broadcasted_iota(jnp.int32, (LANES, LANES), 0)
    c2 = lax.broadcasted_iota(jnp.int32, (LANES, LANES), 1)
    eye = jnp.where(r2 == c2, 1.0, 0.0).astype(BF16)
    row = lax.broadcasted_iota(jnp.int32, (blk, blk), 0)
    col = lax.broadcasted_iota(jnp.int32, (blk, blk), 1)
    causal = col <= row
    colmax = lambda x: functools.reduce(jnp.maximum, [x[:, a:a + LANES] for a in range(0, x.shape[1], LANES)])

    for i in range(nb):
        qs = slice(i * blk, (i + 1) * blk)
        bias = lax.dot_general(bias_t[:, qs], eye, _TN, preferred_element_type=F32)
        q_ext = jnp.concatenate([q_ref[qs, :], bias.astype(BF16)], axis=1)
        nk = (i + 1) * blk
        tiles = [(a, min(a + key_tile, nk)) for a in range(0, nk, key_tile)]
        m_lane = None
        for a, e in tiles:
            s = lax.dot_general(q_ext, kext_ref[a:e, :], _NT, preferred_element_type=F32)
            if e == nk:
                s_own = jnp.where(causal, s[:, e - a - blk:], NEG_INF)
                s = s_own if e - a == blk else jnp.concatenate([s[:, :e - a - blk], s_own], axis=1)
            s_ref[:, a:e] = s
            part = colmax(s)
            m_lane = part if m_lane is None else jnp.maximum(m_lane, part)
        m = jnp.max(m_lane, axis=-1, keepdims=True)
        acc = None
        for a, e in tiles:
            p = jnp.exp((s_ref[:, a:e] - m).astype(BF16))
            pv = jnp.dot(p, vext_ref[a:e, :], preferred_element_type=F32)
            acc = pv if acc is None else acc + pv
        o_ref[qs, :] = (acc[:, :dh] / acc[:, dh:]).astype(o_ref.dtype)


def _moba(qkv, *, batch, seq_len):
    blk, dh, nh = MOBA_BLOCK, MOBA_HEAD_DIM, MOBA_HEADS
    assert seq_len % blk == 0 and seq_len // blk <= LANES and dh == LANES
    pool = (jnp.arange(LANES)[:, None] == (jnp.arange(seq_len) // blk)[None, :]).astype(BF16) / blk
    qkv3 = qkv.reshape(batch, seq_len, 3 * MOBA_WIDTH)
    head = lambda off: pl.BlockSpec((None, seq_len, dh), lambda b, h: (b, 0, off + h))
    out = pl.pallas_call(
        _moba_kernel,
        grid=(batch, nh),
        in_specs=[head(0), head(nh), head(2 * nh), pl.BlockSpec((LANES, seq_len), lambda b, h: (0, 0))],
        out_specs=head(0),
        out_shape=jax.ShapeDtypeStruct((batch, seq_len, MOBA_WIDTH), BF16),
        scratch_shapes=[pltpu.VMEM((seq_len, dh + LANES), BF16), pltpu.VMEM((seq_len, dh + LANES), BF16),
                        pltpu.VMEM((blk, seq_len), F32)],
        compiler_params=_params("parallel", "parallel"),
        name="moba",
    )(qkv3, qkv3, qkv3, pool)
    return out.reshape(batch * seq_len, MOBA_WIDTH)


def _residue_perm(d):
    p = DIL_PERM_ROWS
    idx = jnp.arange(p)
    nat = (idx % (p // d)) * d + idx // (p // d)
    return (nat[:, None] == idx[None, :]).astype(BF16)


def _dilated_proj_kernel(x_ref, w_ref, tab_ref, *rest, dilations):
    n = len(dilations)
    p_refs, o_refs, xb_ref = rest[:n], rest[n:2 * n], rest[2 * n]
    j = pl.program_id(1)

    @pl.when(j == 0)
    def _():
        xb_ref[...] = x_ref[...].astype(BF16)

    tm = x_ref.shape[0]
    grp = DIL_PERM_ROWS
    nat_ref = o_refs[0]
    for r0 in range(0, tm, grp):
        rs = slice(r0, r0 + grp)
        acc = jnp.dot(xb_ref[rs, :], w_ref[...], preferred_element_type=F32)
        t_z, t_partner = tab_ref[0, rs, :], tab_ref[1, rs, :]
        for hb in range(acc.shape[1] // LANES):
            z = acc[:, hb * LANES:(hb + 1) * LANES]
            nat_ref[0, rs, hb * LANES:(hb + 1) * LANES] = (
                z * t_z + pltpu.roll(z, LANES // 2, 1) * t_partner).astype(nat_ref.dtype)

    for g in range(1, n):
        d = dilations[g]
        w = grp // d

        @pl.when((j == g) | (j >= n))
        def _(g=g, d=d, w=w):
            for gi in range(tm // grp):
                y = jnp.dot(p_refs[g][...], nat_ref[0, gi * grp:(gi + 1) * grp, :],
                            preferred_element_type=F32).astype(o_refs[g].dtype)
                for r in range(d):
                    o_refs[g][r, gi * w:(gi + 1) * w, :] = y[r * w:(r + 1) * w, :]


def _dilated_proj(x2d, w, tabs, dilations, *, batch, seq_len):
    m, k = x2d.shape
    n = len(dilations)
    tm = min(1024, seq_len)
    assert dilations[0] == 1 and w.shape[1] == (n + 2) * C_WIDTH
    assert m % tm == 0 and seq_len % tm == 0 and tm % DIL_PERM_ROWS == 0
    assert all(DIL_PERM_ROWS % (d * 2 * SUBLANES) == 0 for d in dilations[1:])
    bps = seq_len // tm
    perms = [_residue_perm(d) for d in dilations]

    def tab_map(i, j):
        return ((j >= n).astype(jnp.int32) + (j > n).astype(jnp.int32), 0, i % bps, 0)

    def out_map(g):
        return lambda i, j: (i // bps, 0, i % bps, (j > g).astype(jnp.int32) + (j > n).astype(jnp.int32))

    outs = pl.pallas_call(
        functools.partial(_dilated_proj_kernel, dilations=tuple(dilations)),
        grid=(m // tm, n + 2),
        in_specs=[pl.BlockSpec((tm, k), lambda i, j: (i, 0)),
                  pl.BlockSpec((k, C_WIDTH), lambda i, j: (0, j)),
                  pl.BlockSpec((None, 2, tm, LANES), tab_map)]
                 + [pl.BlockSpec(p.shape, lambda i, j: (0, 0)) for p in perms],
        out_specs=[pl.BlockSpec((None, d, tm // d, C_WIDTH), out_map(g)) for g, d in enumerate(dilations)],
        out_shape=[jax.ShapeDtypeStruct((batch, d, seq_len // d, 3 * C_WIDTH), BF16) for d in dilations],
        scratch_shapes=[pltpu.VMEM((tm, k), BF16)],
        compiler_params=_params("parallel", "arbitrary"),
        name="dilated_proj",
    )(x2d, w, tabs, *perms)
    return outs


def _dilated_kernel(q_ref, k_ref, kp_ref, v_ref, vp_ref, o_ref, lse_ref):
    w, dh = DIL_WINDOW, DIL_HEAD_DIM
    tiles = q_ref.shape[0] // w
    has_prev = pl.program_id(2) > 0
    row = lax.broadcasted_iota(jnp.int32, (w, 2 * w), 0)
    col = lax.broadcasted_iota(jnp.int32, (w, 2 * w), 1)
    band = (col >= row) & (col <= row + w)
    first_band = band & (has_prev | (col >= w))
    er = lax.broadcasted_iota(jnp.int32, (w, w), 0)
    ec = lax.broadcasted_iota(jnp.int32, (w, w), 1)
    eye = er == ec
    units = [(h, t) for h in range(DIL_HEADS) for t in range(tiles)]

    def banded(ref, halo_ref, h, t):
        hs = slice(h * dh, (h + 1) * dh)
        if t == 0:
            return jnp.concatenate([halo_ref[:, hs], ref[0:w, hs]], axis=0)
        return ref[(t - 1) * w:(t + 1) * w, hs]

    s = {}
    for h, t in units:
        q = q_ref[t * w:(t + 1) * w, h * dh:(h + 1) * dh]
        sc = lax.dot_general(q, banded(k_ref, kp_ref, h, t), _NT, preferred_element_type=F32)
        s[h, t] = jnp.where(first_band if t == 0 else band, sc, NEG_INF)
    m = {u: jnp.max(jnp.maximum(s[u][:, :w], s[u][:, w:]), axis=-1, keepdims=True) for u in units}
    p = {u: jnp.exp((s[u] - m[u]).astype(BF16)) for u in units}
    ones = jnp.ones((2 * w, dh), BF16)
    ol = {(h, t): jnp.dot(p[h, t], jnp.concatenate([banded(v_ref, vp_ref, h, t), ones], axis=1),
                          preferred_element_type=F32) for h, t in units}
    for h, t in units:
        o, l = ol[h, t][:, :dh], ol[h, t][:, dh:]
        o_ref[t * w:(t + 1) * w, h * dh:(h + 1) * dh] = (o / l).astype(o_ref.dtype)
        lse = m[h, t] + jnp.log(l)
        lse_ref[t, h:h + 1, :] = jnp.sum(jnp.where(eye, lse, 0.0), axis=0, keepdims=True)


def _dilated_group(src, cols, *, seq_len):
    batch, d, length, _ = src.shape
    w = DIL_WINDOW
    rows = min(DIL_ROWS, length)
    assert length % rows == 0 and rows % w == 0 and length * d == seq_len
    tiles = rows // w
    qc, kc, vc = cols
    blk = lambda cb: pl.BlockSpec((None, None, rows, C_WIDTH), lambda b, r, i: (b, r, i, cb))
    halo = lambda cb: pl.BlockSpec(
        (None, None, w, C_WIDTH), lambda b, r, i: (b, r, jnp.maximum(i * tiles - 1, 0), cb))
    o, lse = pl.pallas_call(
        _dilated_kernel,
        grid=(batch, d, length // rows),
        in_specs=[blk(qc), blk(kc), halo(kc), blk(vc), halo(vc)],
        out_specs=[
            pl.BlockSpec((None, None, rows, C_WIDTH), lambda b, r, i: (b, r, i, 0)),
            pl.BlockSpec((None, None, tiles, DIL_HEADS, w), lambda b, r, i: (b, r, i, 0, 0)),
        ],
        out_shape=[
            jax.ShapeDtypeStruct((batch, d, length, C_WIDTH), BF16),
            jax.ShapeDtypeStruct((batch, d, length // w, DIL_HEADS, w), F32),
        ],
        compiler_params=_params("parallel", "parallel", "arbitrary"),
        name=f"dilated_d{d}",
    )(src, src, src, src, src)
    lse = lse.transpose(0, 2, 4, 1, 3).reshape(batch, seq_len, DIL_HEADS)
    return o, lse


def _residual_mlp(x, proj, ln1_g, ln1_b, w1_ref, w2_ref, ln2_g, ln2_b):
    x1 = _layer_norm(DEEPNORM_ALPHA * x + proj, ln1_g, ln1_b)
    xb = x1.astype(BF16)
    hs = []
    for c0 in range(0, w1_ref.shape[1], MLP_HIDDEN_CHUNK):
        h = jnp.dot(xb, w1_ref[:, c0:c0 + MLP_HIDDEN_CHUNK], preferred_element_type=F32)
        hs.append(jnp.square(jnp.maximum(h, 0.0)).astype(BF16))
    y = jnp.dot(jnp.concatenate(hs, axis=1), w2_ref[...], preferred_element_type=F32)
    return _layer_norm(DEEPNORM_ALPHA * x1 + y, ln2_g, ln2_b)


def _outproj_mlp_kernel(*refs, n_act):
    acts = refs[:n_act]
    w_ref, x_ref, g1_ref, b1_ref, w1_ref, w2_ref, g2_ref, b2_ref, o_ref = refs[n_act:]
    proj = None
    off = 0
    for a_ref in acts:
        kw = a_ref.shape[1]
        part = jnp.dot(a_ref[...], w_ref[off:off + kw, :], preferred_element_type=F32)
        proj = part if proj is None else proj + part
        off += kw
    o_ref[...] = _residual_mlp(x_ref[...], proj, g1_ref[...], b1_ref[...], w1_ref, w2_ref,
                               g2_ref[...], b2_ref[...])


def _outproj_mlp(acts, w, x2d, ln1, mlp, ln2):
    m, dm = x2d.shape
    tm = LAYER_ROWS
    assert m % tm == 0 and mlp[0].shape[1] % MLP_HIDDEN_CHUNK == 0
    rowspec = lambda a: pl.BlockSpec((tm, a.shape[1]), lambda i: (i, 0))
    const = lambda a: pl.BlockSpec(a.shape, lambda i: (0, 0), pipeline_mode=pl.Buffered(1))
    return pl.pallas_call(
        functools.partial(_outproj_mlp_kernel, n_act=len(acts)),
        grid=(m // tm,),
        in_specs=([rowspec(a) for a in acts] + [const(w), rowspec(x2d)]
                  + [const(a) for a in (*ln1, *mlp, *ln2)]),
        out_specs=pl.BlockSpec((tm, dm), lambda i: (i, 0)),
        out_shape=jax.ShapeDtypeStruct((m, dm), F32),
        compiler_params=_params("parallel"),
        name="outproj_mlp",
    )(*acts, w, x2d, *ln1, *mlp, *ln2)


def _merge_outproj_mlp_kernel(*refs, dilations):
    n = len(dilations)
    o_refs, lse_refs, p_refs = refs[:n], refs[n:2 * n], refs[2 * n:3 * n]
    w_ref, x_ref, g1_ref, b1_ref, w1_ref, w2_ref, g2_ref, b2_ref, out_ref = refs[3 * n:]
    nh, dh, grp = DIL_HEADS, DIL_HEAD_DIM, DIL_PERM_ROWS
    tm = x_ref.shape[0]
    branch = []
    for o_ref, p_ref, d in zip(o_refs, p_refs, dilations):
        if d == 1:
            branch.append(o_ref[0].astype(F32))
            continue
        w = grp // d
        nat = []
        for gi in range(tm // grp):
            perm = jnp.concatenate([o_ref[r, gi * w:(gi + 1) * w, :] for r in range(d)], axis=0)
            nat.append(lax.dot_general(p_ref[...], perm, _TN, preferred_element_type=F32))
        branch.append(jnp.concatenate(nat, axis=0))
    parts = [ref[...] for ref in lse_refs]
    m = functools.reduce(jnp.maximum, parts)
    es = [jnp.exp(p - m) for p in parts]
    den = functools.reduce(jnp.add, es)
    wts = [e / den for e in es]
    heads = []
    for h in range(nh):
        hs = slice(h * dh, (h + 1) * dh)
        heads.append(functools.reduce(jnp.add, [wts[g][:, h:h + 1] * branch[g][:, hs] for g in range(n)]))
    proj = _bdot(jnp.concatenate(heads, axis=1), w_ref[...])
    out_ref[...] = _residual_mlp(x_ref[...], proj, g1_ref[...], b1_ref[...], w1_ref, w2_ref,
                                 g2_ref[...], b2_ref[...])


def _merge_outproj_mlp(outs, lses, dilations, w, x2d, ln1, mlp, ln2, *, batch, seq_len):
    dm = x2d.shape[1]
    tm = LAYER_ROWS
    assert seq_len % tm == 0 and tm % DIL_PERM_ROWS == 0 and mlp[0].shape[1] % MLP_HIDDEN_CHUNK == 0
    x3 = x2d.reshape(batch, seq_len, dm)
    perms = [_residue_perm(d) for d in dilations]
    const = lambda a: pl.BlockSpec(a.shape, lambda bi, i: (0,) * a.ndim, pipeline_mode=pl.Buffered(1))
    rows3 = lambda a: pl.BlockSpec((None, tm, a.shape[2]), lambda bi, i: (bi, i, 0))
    in_specs = ([pl.BlockSpec((None, d, tm // d, C_WIDTH), lambda bi, i: (bi, 0, i, 0)) for d in dilations]
                + [rows3(a) for a in lses] + [const(p) for p in perms] + [const(w), rows3(x3)]
                + [const(a) for a in (*ln1, *mlp, *ln2)])
    out = pl.pallas_call(
        functools.partial(_merge_outproj_mlp_kernel, dilations=tuple(dilations)),
        grid=(batch, seq_len // tm),
        in_specs=in_specs,
        out_specs=pl.BlockSpec((None, tm, dm), lambda bi, i: (bi, i, 0)),
        out_shape=jax.ShapeDtypeStruct((batch, seq_len, dm), F32),
        compiler_params=_params("parallel", "parallel"),
        name="merge_outproj_mlp",
    )(*outs, *lses, *perms, w, x3, *ln1, *mlp, *ln2)
    return out.reshape(batch * seq_len, dm)


def _rwkv_moba_layer(x2d, w_in, shift_mix, w0, w_up, a0, a_up, g_up, k_k, k_a, r_k, lnx_g, lnx_b,
                     w_out, ln1, mlp, ln2, *, batch, seq_len):
    d = RWKV_DIM
    tabs = _rotary_tables(seq_len, MOBA_HEAD_DIM ** -0.5)
    w_m = w_in[:, RWKV_PROJ_COLS:].astype(BF16)
    w_m = jnp.concatenate([_rotary_weight(w_m[:, :2 * MOBA_WIDTH], 2 * MOBA_HEADS),
                           w_m[:, 2 * MOBA_WIDTH:]], axis=1)
    kinds = [0] * MOBA_HEADS + [1] * MOBA_HEADS + [None] * MOBA_HEADS
    qkv = _rotary_proj(x2d, w_m, tabs, kinds, seq_len=seq_len)
    y_b = _moba(qkv, batch=batch, seq_len=seq_len)
    pad = RWKV_Z_COLS - RWKV_PROJ_COLS
    w_r = jnp.pad(w_in[:, :RWKV_PROJ_COLS], ((0, 0), (0, pad))).astype(BF16)
    mix = jnp.pad(shift_mix, (0, pad))[None, :]
    wl = jnp.zeros((LORA_PAD, 3 * d), F32)
    wl = wl.at[0:DECAY_LORA, 0:d].set(w_up)
    wl = wl.at[DECAY_LORA:DECAY_LORA + ICLR_LORA, d:2 * d].set(a_up)
    wl = wl.at[DECAY_LORA + ICLR_LORA:LORA_COLS, 2 * d:3 * d].set(g_up).astype(BF16)
    par = jnp.stack([w0, a0, k_k, k_a, r_k.reshape(d), lnx_g, lnx_b, jnp.zeros((d,), F32)])
    y_a = _rwkv_scan(x2d, w_r, mix, wl, par, batch=batch, seq_len=seq_len)
    return _outproj_mlp([y_a, y_b], w_out.astype(BF16), x2d, ln1, mlp, ln2)


def _dilated_layer(x2d, w_in, w_out, ln1, mlp, ln2, *, batch, seq_len):
    tabs = _rotary_tables(seq_len, DIL_HEAD_DIM ** -0.5)
    nqk = (DIL_GROUPS + 1) * C_WIDTH
    w_c = w_in.astype(BF16)
    w_c = jnp.concatenate([_rotary_weight(w_c[:, :nqk], (DIL_GROUPS + 1) * DIL_HEADS), w_c[:, nqk:]], axis=1)
    dilations = [dilation for _, dilation in DIL_PAIRS]
    assert all(span // dilation == DIL_WINDOW for span, dilation in DIL_PAIRS)
    srcs = _dilated_proj(x2d, w_c, tabs, dilations, batch=batch, seq_len=seq_len)
    outs, lses = [], []
    for src in srcs:
        o, lse = _dilated_group(src, (0, 1, 2), seq_len=seq_len)
        outs.append(o)
        lses.append(lse)
    return _merge_outproj_mlp(outs, lses, dilations, w_out.astype(BF16), x2d, ln1, mlp, ln2,
                              batch=batch, seq_len=seq_len)


def kernel(x, ab_w_in, ab_shift_mix, ab_w0, ab_w_up, ab_a0, ab_a_up, ab_g_up, ab_k_k, ab_k_a, ab_r_k,
           ab_lnx_g, ab_lnx_b, ab_w_out, c_w_in, c_w_out, ln1_g, ln1_b, mlp_w1, mlp_w2, ln2_g, ln2_b):
    batch, seq_len, dm = x.shape
    assert dm == D_MODEL
    h = x.reshape(batch * seq_len, dm)
    depth = ln1_g.shape[0]
    for layer in range(depth):
        j = layer // 2
        ln1 = (ln1_g[layer][None, :], ln1_b[layer][None, :])
        ln2 = (ln2_g[layer][None, :], ln2_b[layer][None, :])
        mlp = (mlp_w1[layer].astype(BF16), mlp_w2[layer].astype(BF16))
        if layer % 2 == 0:
            h = _rwkv_moba_layer(h, ab_w_in[j], ab_shift_mix[j], ab_w0[j], ab_w_up[j], ab_a0[j], ab_a_up[j],
                                 ab_g_up[j], ab_k_k[j], ab_k_a[j], ab_r_k[j], ab_lnx_g[j], ab_lnx_b[j],
                                 ab_w_out[j], ln1, mlp, ln2, batch=batch, seq_len=seq_len)
        else:
            h = _dilated_layer(h, c_w_in[j], c_w_out[j], ln1, mlp, ln2, batch=batch, seq_len=seq_len)
    return h.reshape(batch, seq_len, dm)
```

```python
import functools

import jax
import jax.numpy as jnp
from jax import lax
from jax.experimental import pallas as pl
from jax.experimental.pallas import tpu as pltpu

F32 = jnp.float32
BF16 = jnp.bfloat16

D_MODEL = 1024
DEPTH = 2
DEEPNORM_ALPHA = (2 * DEPTH) ** 0.25
LN_EPS = 1e-5
ROPE_THETA = 500000.0
ROPE_FRACTION = 4
RWKV_HEAD_DIM = 64
RWKV_DIM = D_MODEL // 2
RWKV_HEADS = RWKV_DIM // RWKV_HEAD_DIM
DECAY_LORA = 64
ICLR_LORA = 64
GATE_LORA = 160
RWKV_GN_EPS = 64e-5
RWKV_PROJ_COLS = 3 * RWKV_DIM + DECAY_LORA + ICLR_LORA + GATE_LORA
MOBA_HEAD_DIM = 128
MOBA_WIDTH = D_MODEL - RWKV_DIM
MOBA_HEADS = MOBA_WIDTH // MOBA_HEAD_DIM
MOBA_BLOCK = 256
MOBA_TOPK = 3
DIL_PAIRS = ((128, 1), (512, 4), (2048, 16))
DIL_GROUPS = len(DIL_PAIRS)
DIL_HEAD_DIM = 128
DIL_HEADS = D_MODEL // DIL_HEAD_DIM
C_WIDTH = DIL_HEADS * DIL_HEAD_DIM
C_IN_COLS = DIL_GROUPS * C_WIDTH + 2 * C_WIDTH
MLP_HIDDEN = 4 * D_MODEL
NEG_INF = -1e30

LANES = 128
SUBLANES = 8
VMEM_LIMIT_BYTES = 56 * 1024 * 1024

LORA_COLS = DECAY_LORA + ICLR_LORA + GATE_LORA
LORA_PAD = -(-LORA_COLS // LANES) * LANES
RWKV_Z_COLS = 3 * RWKV_DIM + LORA_PAD
RWKV_CHUNK = 64
RWKV_SCAN_CHUNKS = 4
DIL_WINDOW = 128
DIL_ROWS = 512
DIL_PERM_ROWS = 256
PROJ_ROW_CHUNK = 256
LAYER_ROWS = 512
MLP_HIDDEN_CHUNK = 1024

_NN = (((1,), (0,)), ((), ()))
_NT = (((1,), (1,)), ((), ()))
_TN = (((0,), (0,)), ((), ()))


def _bdot(a, b, dims=_NN):
    return lax.dot_general(a.astype(BF16), b.astype(BF16), dims, preferred_element_type=F32)


def _layer_norm(y, g, b):
    mu = jnp.mean(y, axis=-1, keepdims=True)
    var = jnp.mean(jnp.square(y - mu), axis=-1, keepdims=True)
    return (y - mu) * lax.rsqrt(var + LN_EPS) * g + b


def _params(*sem):
    return pltpu.CompilerParams(dimension_semantics=sem, vmem_limit_bytes=VMEM_LIMIT_BYTES)


ROT_DIMS = MOBA_HEAD_DIM // ROPE_FRACTION
ROT_HALF = ROT_DIMS // 2


def _rotary_weight(w, n_heads):
    rest_a = LANES // 2 - ROT_HALF
    heads = []
    for h in range(n_heads):
        head = w[:, h * LANES:(h + 1) * LANES]
        rot, rest = head[:, :ROT_DIMS], head[:, ROT_DIMS:]
        heads += [rot[:, :ROT_HALF], rest[:, :rest_a], rot[:, ROT_HALF:], rest[:, rest_a:]]
    return jnp.concatenate(heads, axis=1)


def _rotary_tables(seq_len, q_scale):
    inv_freq = ROPE_THETA ** (-jnp.arange(ROT_HALF, dtype=F32) / ROT_HALF)
    ang = jnp.arange(seq_len).astype(F32)[:, None] * inv_freq[None, :]
    cos, sin = jnp.cos(ang), jnp.sin(ang)
    pad = lambda a, fill: jnp.concatenate([a, jnp.full((seq_len, LANES // 2 - ROT_HALF), fill, F32)], axis=1)
    c = jnp.concatenate([pad(cos, 1.0), pad(cos, 1.0)], axis=1)
    s = jnp.concatenate([pad(-sin, 0.0), pad(sin, 0.0)], axis=1)
    one = jnp.ones_like(c)
    return jnp.stack([jnp.stack([c * q_scale, s * q_scale]), jnp.stack([c, s]),
                      jnp.stack([one, jnp.zeros_like(c)])])


def _rotary_proj_kernel(x_ref, w_ref, tab_ref, o_ref, *, head_kinds):
    tm = x_ref.shape[0]
    chunk = min(PROJ_ROW_CHUNK, tm)
    for r0 in range(0, tm, chunk):
        rs = slice(r0, r0 + chunk)
        acc = jnp.dot(x_ref[rs, :].astype(BF16), w_ref[...], preferred_element_type=F32)
        for hb, kind in enumerate(head_kinds):
            z = acc[:, hb * LANES:(hb + 1) * LANES]
            if kind is not None:
                z = z * tab_ref[kind, 0, rs, :] + pltpu.roll(z, LANES // 2, 1) * tab_ref[kind, 1, rs, :]
            o_ref[rs, hb * LANES:(hb + 1) * LANES] = z.astype(o_ref.dtype)


def _rotary_proj(x2d, w, tabs, head_kinds, *, seq_len):
    m, k = x2d.shape
    n = w.shape[1]
    tm = min(1024, seq_len)
    assert m % tm == 0 and seq_len % tm == 0 and n == len(head_kinds) * LANES
    bps = seq_len // tm
    return pl.pallas_call(
        functools.partial(_rotary_proj_kernel, head_kinds=tuple(head_kinds)),
        grid=(m // tm,),
        in_specs=[pl.BlockSpec((tm, k), lambda i: (i, 0)),
                  pl.BlockSpec((k, n), lambda i: (0, 0), pipeline_mode=pl.Buffered(1)),
                  pl.BlockSpec((tabs.shape[0], 2, tm, LANES), lambda i: (0, 0, i % bps, 0))],
        out_specs=pl.BlockSpec((tm, n), lambda i: (i, 0)),
        out_shape=jax.ShapeDtypeStruct((m, n), BF16),
        compiler_params=_params("parallel"),
        name="rotary_proj",
    )(x2d, w, tabs)


def _softplus(x):
    return jnp.maximum(x, 0.0) + jnp.log(1.0 + jnp.exp(-jnp.abs(x)))


def _split_dot(x, ones_bd):
    hi = x.astype(BF16)
    lo = (x - hi.astype(F32)).astype(BF16)
    return (jnp.dot(hi, ones_bd, preferred_element_type=F32)
            + jnp.dot(lo, ones_bd, preferred_element_type=F32))


def _rwkv_prep(z, prev, mix, wl, par, ones_bd):
    d = RWKV_DIM
    pw = 2 * RWKV_HEAD_DIM
    row = lax.broadcasted_iota(jnp.int32, z.shape, 0)
    shifted = jnp.where(row == 0, prev, pltpu.roll(z, 1, 0))
    zz = z + (shifted - z) * mix
    r, k, v, lo = zz[:, 0:d], zz[:, d:2 * d], zz[:, 2 * d:3 * d], zz[:, 3 * d:]
    lane = lax.broadcasted_iota(jnp.int32, lo.shape, 1)
    act = jnp.where(lane < DECAY_LORA, jnp.tanh(lo),
                    jnp.where(lane < DECAY_LORA + ICLR_LORA, lo, jax.nn.sigmoid(lo)))
    lora = _bdot(act, wl)
    w0, a0, k_k, k_a = par[0:1, :], par[1:2, :], par[2:3, :], par[3:4, :]
    w = -_softplus(-(w0 + lora[:, 0:d])) - 0.5
    a = jax.nn.sigmoid(a0 + lora[:, d:2 * d])
    kk = k * k_k
    sq = kk * kk
    ssq = jnp.concatenate([_split_dot(sq[:, p0:p0 + pw], ones_bd) for p0 in range(0, d, pw)], axis=1)
    kk = kk / jnp.maximum(jnp.sqrt(ssq), 1e-12)
    return r, -jnp.exp(w), k * (1.0 + (a - 1.0) * k_a), v, -kk, kk * a, lora[:, 2 * d:3 * d]


def _rwkv_chunk_kernel(x_ref, wr_ref, mix_ref, wl_ref, par_ref, o_ref, s_ref, prev_ref, *, chunks):
    c, n = RWKV_CHUNK, RWKV_HEAD_DIM
    pw = 2 * n
    assert c == n and pw == LANES
    pairs = RWKV_HEADS // 2
    units = [(ch, p) for ch in range(chunks) for p in range(pairs)]

    first = pl.program_id(1) == 0

    @pl.when(first)
    def _():
        s_ref[...] = jnp.zeros_like(s_ref)

    row = lax.broadcasted_iota(jnp.int32, (c, pw), 0)
    col = lax.broadcasted_iota(jnp.int32, (c, pw), 1) & (n - 1)
    tril_inc = col <= row
    tril_str = col < row
    eye = col == row
    level_masks = [((row >> (lvl + 1)) == (col >> (lvl + 1))) & (((row >> lvl) & 1) == 1) & (((col >> lvl) & 1) == 0)
                   for lvl in range(c.bit_length() - 1)]
    r2 = lax.broadcasted_iota(jnp.int32, (pw, pw), 0)
    c2 = lax.broadcasted_iota(jnp.int32, (pw, pw), 1)
    same_head = (r2 < n) == (c2 < n)
    eye2 = r2 == c2
    ones_bd = jnp.where(same_head, 1.0, 0.0).astype(BF16)

    def stack(x):
        xb = x.astype(BF16)
        return jnp.where(same_head, jnp.concatenate([xb, xb], axis=0), jnp.zeros((), BF16))

    rows = chunks * c
    z = jnp.dot(x_ref[...].astype(BF16), wr_ref[...], preferred_element_type=F32)
    prev = jnp.where(first, 0.0, prev_ref[SUBLANES - 1:SUBLANES, :])
    r_all, lw, kp_all, v_all, na_all, b_all, g_all = _rwkv_prep(
        z, prev, mix_ref[...], wl_ref[...], par_ref[...], ones_bd)
    prev_ref[...] = z[rows - SUBLANES:, :]

    tr = lax.broadcasted_iota(jnp.int32, (rows, rows), 0)
    tc = lax.broadcasted_iota(jnp.int32, (rows, rows), 1)
    chunk_tril = ((tr >> (c.bit_length() - 1)) == (tc >> (c.bit_length() - 1))) & (tc <= tr)
    chunk_tril = jnp.where(chunk_tril, 1.0, 0.0).astype(BF16)
    lw_hi = lw.astype(BF16)
    lw_r1 = lw - lw_hi.astype(F32)
    lw_mid = lw_r1.astype(BF16)
    lw_lo = (lw_r1 - lw_mid.astype(F32)).astype(BF16)
    cum = (jnp.dot(chunk_tril, lw_hi, preferred_element_type=F32)
           + jnp.dot(chunk_tril, lw_mid, preferred_element_type=F32)
           + jnp.dot(chunk_tril, lw_lo, preferred_element_type=F32))
    r_k, lnx_g, lnx_b = par_ref[4:5, :], par_ref[5:6, :], par_ref[6:7, :]

    at, rt, bt, kt, bh, kh, vv, p_end = {}, {}, {}, {}, {}, {}, {}, {}
    for ch in range(chunks):
        rs = slice(ch * c, (ch + 1) * c)
        cum_c = cum[rs]
        cum_end = cum_c[c - 1:c, :]
        e_inv = jnp.exp(-cum_c)
        e_end = jnp.exp(cum_end - cum_c)
        at_c = na_all[rs] * jnp.exp(cum_c - lw[rs])
        rt_c = r_all[rs] * jnp.exp(cum_c)
        b_c, kp_c = b_all[rs], kp_all[rs]
        bt_c, kt_c, bh_c, kh_c = b_c * e_inv, kp_c * e_inv, b_c * e_end, kp_c * e_end
        pe_c = jnp.exp(cum_end)
        for p in range(pairs):
            ls = slice(p * pw, (p + 1) * pw)
            u = (ch, p)
            at[u], rt[u], bt[u], kt[u], bh[u], kh[u] = at_c[:, ls], rt_c[:, ls], bt_c[:, ls], kt_c[:, ls], bh_c[:, ls], kh_c[:, ls]
            vv[u], p_end[u] = v_all[rs, ls], pe_c[:, ls]

    ar = {u: jnp.concatenate([at[u], rt[u]], axis=0) for u in units}
    g = {u: _bdot(ar[u], jnp.concatenate([stack(bt[u]), stack(kt[u])], axis=0), _NT)
         for u in units}
    a_ab = {u: jnp.where(tril_str, g[u][:c, :pw], 0.0) for u in units}
    a_rb = {u: jnp.where(tril_inc, g[u][c:, :pw], 0.0) for u in units}
    a_ak = {u: jnp.where(tril_str, g[u][:c, pw:], 0.0) for u in units}
    a_rk = {u: jnp.where(tril_inc, g[u][c:, pw:], 0.0) for u in units}
    av = {u: _bdot(jnp.concatenate([a_ak[u], a_rk[u]], axis=0), stack(vv[u])) for u in units}
    ak_v = {u: av[u][:c] for u in units}
    inv = {u: jnp.where(eye, 1.0, jnp.where(level_masks[0], a_ab[u], 0.0)) for u in units}
    for mask in level_masks[1:]:
        e = {u: _bdot(jnp.where(mask, a_ab[u], 0.0), stack(inv[u])) for u in units}
        inv = {u: inv[u] + _bdot(inv[u], stack(e[u])) for u in units}
    wu = {u: _bdot(inv[u], jnp.concatenate([stack(at[u]), stack(ak_v[u])], axis=1)) for u in units}
    w_mat = {u: wu[u][:, :pw] for u in units}
    u_loc = {u: wu[u][:, pw:] for u in units}
    wub = {u: _bdot(wu[u], bh[u], _TN) for u in units}
    m_c = {u: jnp.where(same_head, wub[u][:pw], 0.0) + jnp.where(eye2, p_end[u], 0.0) for u in units}
    n_c = {u: jnp.where(same_head, wub[u][pw:] + _bdot(vv[u], kh[u], _TN), 0.0) for u in units}
    qy = {u: _bdot(a_rb[u], jnp.concatenate([stack(w_mat[u]), stack(u_loc[u])], axis=1)) for u in units}
    q_c = {u: rt[u] + qy[u][:, :pw] for u in units}
    y_loc = {u: qy[u][:, pw:] + av[u][c:] for u in units}

    state = [s_ref[p] for p in range(pairs)]
    y = {}
    for ch in range(chunks):
        for p in range(pairs):
            u = (ch, p)
            y[u] = _bdot(q_c[u], state[p], _NT) + y_loc[u]
            state[p] = _bdot(state[p], m_c[u]) + n_c[u]
    for p in range(pairs):
        s_ref[p] = state[p]

    inv_n = 1.0 / n
    mu = {u: _bdot(y[u], ones_bd) * inv_n for u in units}
    var = {u: _bdot(jnp.square(y[u] - mu[u]), ones_bd) * inv_n for u in units}
    for (ch, p) in units:
        u = (ch, p)
        rs, ls = slice(ch * c, (ch + 1) * c), slice(p * pw, (p + 1) * pw)
        yn = (y[u] - mu[u]) * lax.rsqrt(var[u] + RWKV_GN_EPS) * lnx_g[:, ls] + lnx_b[:, ls]
        bonus = _bdot(r_all[rs, ls] * kp_all[rs, ls] * r_k[:, ls], ones_bd) * vv[u]
        o_ref[rs, ls] = ((yn + bonus) * g_all[rs, ls]).astype(o_ref.dtype)


def _rwkv_scan(x2d, w_r, mix, wl, par, *, batch, seq_len):
    c = RWKV_CHUNK
    chunks = RWKV_SCAN_CHUNKS
    rows = chunks * c
    dm, zc = w_r.shape
    assert seq_len % rows == 0
    const = lambda a: pl.BlockSpec(a.shape, lambda bi, ci: (0, 0), pipeline_mode=pl.Buffered(1))
    out = pl.pallas_call(
        functools.partial(_rwkv_chunk_kernel, chunks=chunks),
        grid=(batch, seq_len // rows),
        in_specs=[pl.BlockSpec((None, rows, dm), lambda bi, ci: (bi, ci, 0)),
                  const(w_r), const(mix), const(wl), const(par)],
        out_specs=pl.BlockSpec((None, rows, RWKV_DIM), lambda bi, ci: (bi, ci, 0)),
        out_shape=jax.ShapeDtypeStruct((batch, seq_len, RWKV_DIM), BF16),
        scratch_shapes=[pltpu.VMEM((RWKV_HEADS // 2, 2 * RWKV_HEAD_DIM, 2 * RWKV_HEAD_DIM), F32),
                        pltpu.VMEM((SUBLANES, zc), F32)],
        compiler_params=_params("parallel", "arbitrary"),
        name="rwkv_scan",
    )(x2d.reshape(batch, seq_len, dm), w_r, mix, wl, par)
    return out.reshape(batch * seq_len, RWKV_DIM)


def _moba_kernel(q_ref, k_ref, v_ref, pool_ref, o_ref, kext_ref, vext_ref, s_ref):
    blk, dh = MOBA_BLOCK, MOBA_HEAD_DIM
    t_len = k_ref.shape[0]
    nb = t_len // blk
    nbp = -(-nb // SUBLANES) * SUBLANES
    shift = blk.bit_length() - 1
    key_tile = 2 * blk

    lane = lax.broadcasted_iota(jnp.int32, (blk, LANES), 1)
    for j in range(nb):
        rs = slice(j * blk, (j + 1) * blk)
        kext_ref[rs, 0:dh] = k_ref[rs, :]
        kext_ref[rs, dh:dh + LANES] = jnp.where(lane == j, 1.0, 0.0).astype(BF16)
        vext_ref[rs, 0:dh] = v_ref[rs, :]
        vext_ref[rs, dh:dh + LANES] = jnp.ones((blk, LANES), BF16)

    km = jnp.dot(pool_ref[...], k_ref[...], preferred_element_type=F32)[0:nbp]
    km_hi = km.astype(BF16)
    km_lo = (km - km_hi.astype(F32)).astype(BF16)
    q_all = q_ref[...]
    gate = (lax.dot_general(km_hi, q_all, _NT, preferred_element_type=F32)
            + lax.dot_general(km_lo, q_all, _NT, preferred_element_type=F32))
    bidx = lax.broadcasted_iota(jnp.int32, gate.shape, 0)
    qblk = lax.broadcasted_iota(jnp.int32, gate.shape, 1) >> shift
    past = bidx < qblk
    gate = jnp.where(past, gate, NEG_INF)
    rank = jnp.zeros(gate.shape, jnp.int32)
    for jp in range(nb):
        other = gate[jp:jp + 1, :]
        rank = rank + ((other > gate) | ((other == gate) & (jp < bidx))).astype(jnp.int32)
    allowed = (past & (rank < MOBA_TOPK)) | (bidx == qblk)
    bias_t = jnp.where(allowed, 0.0, NEG_INF).astype(BF16)
    bias_t = jnp.concatenate([bias_t, jnp.zeros((LANES - nbp, t_len), BF16)], axis=0)
    r2 = lax.broadcasted_iota(jnp.int32, (LANES, LANES), 0)
    c2 = lax.broadcasted_iota(jnp.int32, (LANES, LANES), 1)
    eye = jnp.where(r2 == c2, 1.0, 0.0).astype(BF16)
    row = lax.broadcasted_iota(jnp.int32, (blk, blk), 0)
    col = lax.broadcasted_iota(jnp.int32, (blk, blk), 1)
    causal = col <= row
    colmax = lambda x: functools.reduce(jnp.maximum, [x[:, a:a + LANES] for a in range(0, x.shape[1], LANES)])

    for i in range(nb):
        qs = slice(i * blk, (i + 1) * blk)
        bias = lax.dot_general(bias_t[:, qs], eye, _TN, preferred_element_type=F32)
        q_ext = jnp.concatenate([q_ref[qs, :], bias.astype(BF16)], axis=1)
        nk = (i + 1) * blk
        tiles = [(a, min(a + key_tile, nk)) for a in range(0, nk, key_tile)]
        m_lane = None
        for a, e in tiles:
            s = lax.dot_general(q_ext, kext_ref[a:e, :], _NT, preferred_element_type=F32)
            if e == nk:
                s_own = jnp.where(causal, s[:, e - a - blk:], NEG_INF)
                s = s_own if e - a == blk else jnp.concatenate([s[:, :e - a - blk], s_own], axis=1)
            s_ref[:, a:e] = s
            part = colmax(s)
            m_lane = part if m_lane is None else jnp.maximum(m_lane, part)
        m = jnp.max(m_lane, axis=-1, keepdims=True)
        acc = None
        for a, e in tiles:
            p = jnp.exp((s_ref[:, a:e] - m).astype(BF16))
            pv = jnp.dot(p, vext_ref[a:e, :], preferred_element_type=F32)
            acc = pv if acc is None else acc + pv
        o_ref[qs, :] = (acc[:, :dh] / acc[:, dh:]).astype(o_ref.dtype)


def _moba(qkv, *, batch, seq_len):
    blk, dh, nh = MOBA_BLOCK, MOBA_HEAD_DIM, MOBA_HEADS
    assert seq_len % blk == 0 and seq_len // blk <= LANES and dh == LANES
    pool = (jnp.arange(LANES)[:, None] == (jnp.arange(seq_len) // blk)[None, :]).astype(BF16) / blk
    qkv3 = qkv.reshape(batch, seq_len, 3 * MOBA_WIDTH)
    head = lambda off: pl.BlockSpec((None, seq_len, dh), lambda b, h: (b, 0, off + h))
    out = pl.pallas_call(
        _moba_kernel,
        grid=(batch, nh),
        in_specs=[head(0), head(nh), head(2 * nh), pl.BlockSpec((LANES, seq_len), lambda b, h: (0, 0))],
        out_specs=head(0),
        out_shape=jax.ShapeDtypeStruct((batch, seq_len, MOBA_WIDTH), BF16),
        scratch_shapes=[pltpu.VMEM((seq_len, dh + LANES), BF16), pltpu.VMEM((seq_len, dh + LANES), BF16),
                        pltpu.VMEM((blk, seq_len), F32)],
        compiler_params=_params("parallel", "parallel"),
        name="moba",
    )(qkv3, qkv3, qkv3, pool)
    return out.reshape(batch * seq_len, MOBA_WIDTH)


def _residue_perm(d):
    p = DIL_PERM_ROWS
    idx = jnp.arange(p)
    nat = (idx % (p // d)) * d + idx // (p // d)
    return (nat[:, None] == idx[None, :]).astype(BF16)


def _dilated_proj_kernel(x_ref, w_ref, tab_ref, *rest, dilations):
    n = len(dilations)
    p_refs, o_refs, xb_ref = rest[:n], rest[n:2 * n], rest[2 * n]
    j = pl.program_id(1)

    @pl.when(j == 0)
    def _():
        xb_ref[...] = x_ref[...].astype(BF16)

    tm = x_ref.shape[0]
    grp = DIL_PERM_ROWS
    nat_ref = o_refs[0]
    for r0 in range(0, tm, grp):
        rs = slice(r0, r0 + grp)
        acc = jnp.dot(xb_ref[rs, :], w_ref[...], preferred_element_type=F32)
        t_z, t_partner = tab_ref[0, rs, :], tab_ref[1, rs, :]
        for hb in range(acc.shape[1] // LANES):
            z = acc[:, hb * LANES:(hb + 1) * LANES]
            nat_ref[0, rs, hb * LANES:(hb + 1) * LANES] = (
                z * t_z + pltpu.roll(z, LANES // 2, 1) * t_partner).astype(nat_ref.dtype)

    for g in range(1, n):
        d = dilations[g]
        w = grp // d

        @pl.when((j == g) | (j >= n))
        def _(g=g, d=d, w=w):
            for gi in range(tm // grp):
                y = jnp.dot(p_refs[g][...], nat_ref[0, gi * grp:(gi + 1) * grp, :],
                            preferred_element_type=F32).astype(o_refs[g].dtype)
                for r in range(d):
                    o_refs[g][r, gi * w:(gi + 1) * w, :] = y[r * w:(r + 1) * w, :]


def _dilated_proj(x2d, w, tabs, dilations, *, batch, seq_len):
    m, k = x2d.shape
    n = len(dilations)
    tm = min(1024, seq_len)
    assert dilations[0] == 1 and w.shape[1] == (n + 2) * C_WIDTH
    assert m % tm == 0 and seq_len % tm == 0 and tm % DIL_PERM_ROWS == 0
    assert all(DIL_PERM_ROWS % (d * 2 * SUBLANES) == 0 for d in dilations[1:])
    bps = seq_len // tm
    perms = [_residue_perm(d) for d in dilations]

    def tab_map(i, j):
        return ((j >= n).astype(jnp.int32) + (j > n).astype(jnp.int32), 0, i % bps, 0)

    def out_map(g):
        return lambda i, j: (i // bps, 0, i % bps, (j > g).astype(jnp.int32) + (j > n).astype(jnp.int32))

    outs = pl.pallas_call(
        functools.partial(_dilated_proj_kernel, dilations=tuple(dilations)),
        grid=(m // tm, n + 2),
        in_specs=[pl.BlockSpec((tm, k), lambda i, j: (i, 0)),
                  pl.BlockSpec((k, C_WIDTH), lambda i, j: (0, j)),
                  pl.BlockSpec((None, 2, tm, LANES), tab_map)]
                 + [pl.BlockSpec(p.shape, lambda i, j: (0, 0)) for p in perms],
        out_specs=[pl.BlockSpec((None, d, tm // d, C_WIDTH), out_map(g)) for g, d in enumerate(dilations)],
        out_shape=[jax.ShapeDtypeStruct((batch, d, seq_len // d, 3 * C_WIDTH), BF16) for d in dilations],
        scratch_shapes=[pltpu.VMEM((tm, k), BF16)],
        compiler_params=_params("parallel", "arbitrary"),
        name="dilated_proj",
    )(x2d, w, tabs, *perms)
    return outs


def _dilated_kernel(q_ref, k_ref, kp_ref, v_ref, vp_ref, o_ref, lse_ref):
    w, dh = DIL_WINDOW, DIL_HEAD_DIM
    tiles = q_ref.shape[0] // w
    has_prev = pl.program_id(2) > 0
    row = lax.broadcasted_iota(jnp.int32, (w, 2 * w), 0)
    col = lax.broadcasted_iota(jnp.int32, (w, 2 * w), 1)
    band = (col >= row) & (col <= row + w)
    first_band = band & (has_prev | (col >= w))
    er = lax.broadcasted_iota(jnp.int32, (w, w), 0)
    ec = lax.broadcasted_iota(jnp.int32, (w, w), 1)
    eye = er == ec
    units = [(h, t) for h in range(DIL_HEADS) for t in range(tiles)]

    def banded(ref, halo_ref, h, t):
        hs = slice(h * dh, (h + 1) * dh)
        if t == 0:
            return jnp.concatenate([halo_ref[:, hs], ref[0:w, hs]], axis=0)
        return ref[(t - 1) * w:(t + 1) * w, hs]

    s = {}
    for h, t in units:
        q = q_ref[t * w:(t + 1) * w, h * dh:(h + 1) * dh]
        sc = lax.dot_general(q, banded(k_ref, kp_ref, h, t), _NT, preferred_element_type=F32)
        s[h, t] = jnp.where(first_band if t == 0 else band, sc, NEG_INF)
    m = {u: jnp.max(jnp.maximum(s[u][:, :w], s[u][:, w:]), axis=-1, keepdims=True) for u in units}
    p = {u: jnp.exp((s[u] - m[u]).astype(BF16)) for u in units}
    ones = jnp.ones((2 * w, dh), BF16)
    ol = {(h, t): jnp.dot(p[h, t], jnp.concatenate([banded(v_ref, vp_ref, h, t), ones], axis=1),
                          preferred_element_type=F32) for h, t in units}
    for h, t in units:
        o, l = ol[h, t][:, :dh], ol[h, t][:, dh:]
        o_ref[t * w:(t + 1) * w, h * dh:(h + 1) * dh] = (o / l).astype(o_ref.dtype)
        lse = m[h, t] + jnp.log(l)
        lse_ref[t, h:h + 1, :] = jnp.sum(jnp.where(eye, lse, 0.0), axis=0, keepdims=True)


def _dilated_group(src, cols, *, seq_len):
    batch, d, length, _ = src.shape
    w = DIL_WINDOW
    rows = min(DIL_ROWS, length)
    assert length % rows == 0 and rows % w == 0 and length * d == seq_len
    tiles = rows // w
    qc, kc, vc = cols
    blk = lambda cb: pl.BlockSpec((None, None, rows, C_WIDTH), lambda b, r, i: (b, r, i, cb))
    halo = lambda cb: pl.BlockSpec(
        (None, None, w, C_WIDTH), lambda b, r, i: (b, r, jnp.maximum(i * tiles - 1, 0), cb))
    o, lse = pl.pallas_call(
        _dilated_kernel,
        grid=(batch, d, length // rows),
        in_specs=[blk(qc), blk(kc), halo(kc), blk(vc), halo(vc)],
        out_specs=[
            pl.BlockSpec((None, None, rows, C_WIDTH), lambda b, r, i: (b, r, i, 0)),
            pl.BlockSpec((None, None, tiles, DIL_HEADS, w), lambda b, r, i: (b, r, i, 0, 0)),
        ],
        out_shape=[
            jax.ShapeDtypeStruct((batch, d, length, C_WIDTH), BF16),
            jax.ShapeDtypeStruct((batch, d, length // w, DIL_HEADS, w), F32),
        ],
        compiler_params=_params("parallel", "parallel", "arbitrary"),
        name=f"dilated_d{d}",
    )(src, src, src, src, src)
    lse = lse.transpose(0, 2, 4, 1, 3).reshape(batch, seq_len, DIL_HEADS)
    return o, lse


def _mlp_ln(x1_ref, w1_ref, w2_ref, ln2_g, ln2_b):
    xb = x1_ref[...].astype(BF16)
    hs = []
    for c0 in range(0, w1_ref.shape[1], MLP_HIDDEN_CHUNK):
        h = jnp.dot(xb, w1_ref[:, c0:c0 + MLP_HIDDEN_CHUNK], preferred_element_type=F32)
        hs.append(jnp.square(jnp.maximum(h, 0.0)).astype(BF16))
    y = jnp.dot(jnp.concatenate(hs, axis=1), w2_ref[...], preferred_element_type=F32)
    return _layer_norm(DEEPNORM_ALPHA * x1_ref[...] + y, ln2_g, ln2_b)


def _outproj_mlp_kernel(*refs, n_act):
    acts = refs[:n_act]
    w_ref, x_ref, g1_ref, b1_ref, w1_ref, w2_ref, g2_ref, b2_ref, o_ref, x1_ref = refs[n_act:]

    @pl.when(pl.program_id(0) == 0)
    def _():
        x1_ref[...] = jnp.zeros_like(x1_ref)

    o_ref[...] = _mlp_ln(x1_ref, w1_ref, w2_ref, g2_ref[...], b2_ref[...])
    proj = None
    off = 0
    for a_ref in acts:
        kw = a_ref.shape[1]
        part = jnp.dot(a_ref[...], w_ref[off:off + kw, :], preferred_element_type=F32)
        proj = part if proj is None else proj + part
        off += kw
    x1_ref[...] = _layer_norm(DEEPNORM_ALPHA * x_ref[...] + proj, g1_ref[...], b1_ref[...])


def _outproj_mlp(acts, w, x2d, ln1, mlp, ln2):
    m, dm = x2d.shape
    tm = LAYER_ROWS
    assert m % tm == 0 and mlp[0].shape[1] % MLP_HIDDEN_CHUNK == 0
    last = m // tm - 1
    rowspec = lambda a: pl.BlockSpec((tm, a.shape[1]), lambda s: (jnp.minimum(s, last), 0))
    const = lambda a: pl.BlockSpec(a.shape, lambda s: (0, 0), pipeline_mode=pl.Buffered(1))
    return pl.pallas_call(
        functools.partial(_outproj_mlp_kernel, n_act=len(acts)),
        grid=(m // tm + 1,),
        in_specs=([rowspec(a) for a in acts] + [const(w), rowspec(x2d)]
                  + [const(a) for a in (*ln1, *mlp, *ln2)]),
        out_specs=pl.BlockSpec((tm, dm), lambda s: (jnp.maximum(s - 1, 0), 0)),
        out_shape=jax.ShapeDtypeStruct((m, dm), F32),
        scratch_shapes=[pltpu.VMEM((tm, dm), F32)],
        compiler_params=_params("arbitrary"),
        name="outproj_mlp",
    )(*acts, w, x2d, *ln1, *mlp, *ln2)


def _merge_outproj_mlp_kernel(*refs, dilations):
    n = len(dilations)
    o_refs, lse_refs, p_refs = refs[:n], refs[n:2 * n], refs[2 * n:3 * n]
    w_ref, x_ref, g1_ref, b1_ref, w1_ref, w2_ref, g2_ref, b2_ref, out_ref, x1_ref = refs[3 * n:]
    nh, dh, grp = DIL_HEADS, DIL_HEAD_DIM, DIL_PERM_ROWS
    tm = x_ref.shape[0]

    @pl.when(pl.program_id(0) == 0)
    def _():
        x1_ref[...] = jnp.zeros_like(x1_ref)

    out_ref[...] = _mlp_ln(x1_ref, w1_ref, w2_ref, g2_ref[...], b2_ref[...])
    branch = []
    for o_ref, p_ref, d in zip(o_refs, p_refs, dilations):
        if d == 1:
            branch.append(o_ref[0].astype(F32))
            continue
        w = grp // d
        nat = []
        for gi in range(tm // grp):
            perm = jnp.concatenate([o_ref[r, gi * w:(gi + 1) * w, :] for r in range(d)], axis=0)
            nat.append(lax.dot_general(p_ref[...], perm, _TN, preferred_element_type=F32))
        branch.append(jnp.concatenate(nat, axis=0))
    parts = [ref[...] for ref in lse_refs]
    m = functools.reduce(jnp.maximum, parts)
    es = [jnp.exp(p - m) for p in parts]
    den = functools.reduce(jnp.add, es)
    wts = [e / den for e in es]
    heads = []
    for h in range(nh):
        hs = slice(h * dh, (h + 1) * dh)
        heads.append(functools.reduce(jnp.add, [wts[g][:, h:h + 1] * branch[g][:, hs] for g in range(n)]))
    proj = _bdot(jnp.concatenate(heads, axis=1), w_ref[...])
    x1_ref[...] = _layer_norm(DEEPNORM_ALPHA * x_ref[...] + proj, g1_ref[...], b1_ref[...])


def _merge_outproj_mlp(outs, lses, dilations, w, x2d, ln1, mlp, ln2, *, batch, seq_len):
    dm = x2d.shape[1]
    tm = LAYER_ROWS
    assert seq_len % tm == 0 and tm % DIL_PERM_ROWS == 0 and mlp[0].shape[1] % MLP_HIDDEN_CHUNK == 0
    x3 = x2d.reshape(batch, seq_len, dm)
    perms = [_residue_perm(d) for d in dilations]
    bps = seq_len // tm
    n_tiles = batch * bps
    tile = lambda s: jnp.minimum(s, n_tiles - 1)
    done = lambda s: jnp.maximum(s - 1, 0)
    const = lambda a: pl.BlockSpec(a.shape, lambda s: (0,) * a.ndim, pipeline_mode=pl.Buffered(1))
    rows3 = lambda a: pl.BlockSpec((None, tm, a.shape[2]), lambda s: (tile(s) // bps, tile(s) % bps, 0))
    in_specs = ([pl.BlockSpec((None, d, tm // d, C_WIDTH), lambda s: (tile(s) // bps, 0, tile(s) % bps, 0))
                 for d in dilations]
                + [rows3(a) for a in lses] + [const(p) for p in perms] + [const(w), rows3(x3)]
                + [const(a) for a in (*ln1, *mlp, *ln2)])
    out = pl.pallas_call(
        functools.partial(_merge_outproj_mlp_kernel, dilations=tuple(dilations)),
        grid=(n_tiles + 1,),
        in_specs=in_specs,
        out_specs=pl.BlockSpec((None, tm, dm), lambda s: (done(s) // bps, done(s) % bps, 0)),
        out_shape=jax.ShapeDtypeStruct((batch, seq_len, dm), F32),
        scratch_shapes=[pltpu.VMEM((tm, dm), F32)],
        compiler_params=_params("arbitrary"),
        name="merge_outproj_mlp",
    )(*outs, *lses, *perms, w, x3, *ln1, *mlp, *ln2)
    return out.reshape(batch * seq_len, dm)


def _rwkv_moba_layer(x2d, w_in, shift_mix, w0, w_up, a0, a_up, g_up, k_k, k_a, r_k, lnx_g, lnx_b,
                     w_out, ln1, mlp, ln2, *, batch, seq_len):
    d = RWKV_DIM
    tabs = _rotary_tables(seq_len, MOBA_HEAD_DIM ** -0.5)
    w_m = w_in[:, RWKV_PROJ_COLS:].astype(BF16)
    w_m = jnp.concatenate([_rotary_weight(w_m[:, :2 * MOBA_WIDTH], 2 * MOBA_HEADS),
                           w_m[:, 2 * MOBA_WIDTH:]], axis=1)
    kinds = [0] * MOBA_HEADS + [1] * MOBA_HEADS + [None] * MOBA_HEADS
    qkv = _rotary_proj(x2d, w_m, tabs, kinds, seq_len=seq_len)
    y_b = _moba(qkv, batch=batch, seq_len=seq_len)
    pad = RWKV_Z_COLS - RWKV_PROJ_COLS
    w_r = jnp.pad(w_in[:, :RWKV_PROJ_COLS], ((0, 0), (0, pad))).astype(BF16)
    mix = jnp.pad(shift_mix, (0, pad))[None, :]
    wl = jnp.zeros((LORA_PAD, 3 * d), F32)
    wl = wl.at[0:DECAY_LORA, 0:d].set(w_up)
    wl = wl.at[DECAY_LORA:DECAY_LORA + ICLR_LORA, d:2 * d].set(a_up)
    wl = wl.at[DECAY_LORA + ICLR_LORA:LORA_COLS, 2 * d:3 * d].set(g_up).astype(BF16)
    par = jnp.stack([w0, a0, k_k, k_a, r_k.reshape(d), lnx_g, lnx_b, jnp.zeros((d,), F32)])
    y_a = _rwkv_scan(x2d, w_r, mix, wl, par, batch=batch, seq_len=seq_len)
    return _outproj_mlp([y_a, y_b], w_out.astype(BF16), x2d, ln1, mlp, ln2)


def _dilated_layer(x2d, w_in, w_out, ln1, mlp, ln2, *, batch, seq_len):
    tabs = _rotary_tables(seq_len, DIL_HEAD_DIM ** -0.5)
    nqk = (DIL_GROUPS + 1) * C_WIDTH
    w_c = w_in.astype(BF16)
    w_c = jnp.concatenate([_rotary_weight(w_c[:, :nqk], (DIL_GROUPS + 1) * DIL_HEADS), w_c[:, nqk:]], axis=1)
    dilations = [dilation for _, dilation in DIL_PAIRS]
    assert all(span // dilation == DIL_WINDOW for span, dilation in DIL_PAIRS)
    srcs = _dilated_proj(x2d, w_c, tabs, dilations, batch=batch, seq_len=seq_len)
    outs, lses = [], []
    for src in srcs:
        o, lse = _dilated_group(src, (0, 1, 2), seq_len=seq_len)
        outs.append(o)
        lses.append(lse)
    return _merge_outproj_mlp(outs, lses, dilations, w_out.astype(BF16), x2d, ln1, mlp, ln2,
                              batch=batch, seq_len=seq_len)


def kernel(x, ab_w_in, ab_shift_mix, ab_w0, ab_w_up, ab_a0, ab_a_up, ab_g_up, ab_k_k, ab_k_a, ab_r_k,
           ab_lnx_g, ab_lnx_b, ab_w_out, c_w_in, c_w_out, ln1_g, ln1_b, mlp_w1, mlp_w2, ln2_g, ln2_b):
    batch, seq_len, dm = x.shape
    assert dm == D_MODEL
    h = x.reshape(batch * seq_len, dm)
    depth = ln1_g.shape[0]
    for layer in range(depth):
        j = layer // 2
        ln1 = (ln1_g[layer][None, :], ln1_b[layer][None, :])
        ln2 = (ln2_g[layer][None, :], ln2_b[layer][None, :])
        mlp = (mlp_w1[layer].astype(BF16), mlp_w2[layer].astype(BF16))
        if layer % 2 == 0:
            h = _rwkv_moba_layer(h, ab_w_in[j], ab_shift_mix[j], ab_w0[j], ab_w_up[j], ab_a0[j], ab_a_up[j],
                                 ab_g_up[j], ab_k_k[j], ab_k_a[j], ab_r_k[j], ab_lnx_g[j], ab_lnx_b[j],
                                 ab_w_out[j], ln1, mlp, ln2, batch=batch, seq_len=seq_len)
        else:
            h = _dilated_layer(h, c_w_in[j], c_w_out[j], ln1, mlp, ln2, batch=batch, seq_len=seq_len)
    return h.reshape(batch, seq_len, dm)
```

```python
import functools

import jax
import jax.numpy as jnp
from jax import lax
from jax.experimental import pallas as pl
from jax.experimental.pallas import tpu as pltpu

F32 = jnp.float32
BF16 = jnp.bfloat16

D_MODEL = 1024
DEPTH = 2
DEEPNORM_ALPHA = (2 * DEPTH) ** 0.25
LN_EPS = 1e-5
ROPE_THETA = 500000.0
ROPE_FRACTION = 4
RWKV_HEAD_DIM = 64
RWKV_DIM = D_MODEL // 2
RWKV_HEADS = RWKV_DIM // RWKV_HEAD_DIM
DECAY_LORA = 64
ICLR_LORA = 64
GATE_LORA = 160
RWKV_GN_EPS = 64e-5
RWKV_PROJ_COLS = 3 * RWKV_DIM + DECAY_LORA + ICLR_LORA + GATE_LORA
MOBA_HEAD_DIM = 128
MOBA_WIDTH = D_MODEL - RWKV_DIM
MOBA_HEADS = MOBA_WIDTH // MOBA_HEAD_DIM
MOBA_BLOCK = 256
MOBA_TOPK = 3
DIL_PAIRS = ((128, 1), (512, 4), (2048, 16))
DIL_GROUPS = len(DIL_PAIRS)
DIL_HEAD_DIM = 128
DIL_HEADS = D_MODEL // DIL_HEAD_DIM
C_WIDTH = DIL_HEADS * DIL_HEAD_DIM
C_IN_COLS = DIL_GROUPS * C_WIDTH + 2 * C_WIDTH
MLP_HIDDEN = 4 * D_MODEL
NEG_INF = -1e30

LANES = 128
SUBLANES = 8
VMEM_LIMIT_BYTES = 56 * 1024 * 1024

LORA_COLS = DECAY_LORA + ICLR_LORA + GATE_LORA
LORA_PAD = -(-LORA_COLS // LANES) * LANES
RWKV_Z_COLS = 3 * RWKV_DIM + LORA_PAD
RWKV_CHUNK = 64
RWKV_SCAN_CHUNKS = 8
DIL_WINDOW = 128
DIL_ROWS = 512
DIL_PERM_ROWS = 256
PROJ_ROW_CHUNK = 256
LAYER_ROWS = 512
MLP_HIDDEN_CHUNK = 1024

_NN = (((1,), (0,)), ((), ()))
_NT = (((1,), (1,)), ((), ()))
_TN = (((0,), (0,)), ((), ()))


def _bdot(a, b, dims=_NN):
    return lax.dot_general(a.astype(BF16), b.astype(BF16), dims, preferred_element_type=F32)


def _layer_norm(y, g, b):
    mu = jnp.mean(y, axis=-1, keepdims=True)
    var = jnp.mean(jnp.square(y - mu), axis=-1, keepdims=True)
    return (y - mu) * lax.rsqrt(var + LN_EPS) * g + b


def _params(*sem):
    return pltpu.CompilerParams(dimension_semantics=sem, vmem_limit_bytes=VMEM_LIMIT_BYTES)


ROT_DIMS = MOBA_HEAD_DIM // ROPE_FRACTION
ROT_HALF = ROT_DIMS // 2


def _rotary_weight(w, n_heads):
    split = ROT_DIMS + LANES // 2 - ROT_HALF
    w3 = w.reshape(w.shape[0], n_heads, LANES)
    w3 = jnp.concatenate([w3[..., :ROT_HALF], w3[..., ROT_DIMS:split], w3[..., ROT_HALF:ROT_DIMS],
                          w3[..., split:]], axis=-1)
    return w3.reshape(w.shape[0], n_heads * LANES)


def _rotary_tables(seq_len, q_scale):
    inv_freq = ROPE_THETA ** (-jnp.arange(ROT_HALF, dtype=F32) / ROT_HALF)
    ang = jnp.arange(seq_len).astype(F32)[:, None] * inv_freq[None, :]
    cos, sin = jnp.cos(ang), jnp.sin(ang)
    pad = lambda a, fill: jnp.concatenate([a, jnp.full((seq_len, LANES // 2 - ROT_HALF), fill, F32)], axis=1)
    c = jnp.concatenate([pad(cos, 1.0), pad(cos, 1.0)], axis=1)
    s = jnp.concatenate([pad(-sin, 0.0), pad(sin, 0.0)], axis=1)
    one = jnp.ones_like(c)
    return jnp.stack([jnp.stack([c * q_scale, s * q_scale]), jnp.stack([c, s]),
                      jnp.stack([one, jnp.zeros_like(c)])])


def _rotary_proj_kernel(x_ref, w_ref, tab_ref, o_ref, *, head_kinds):
    tm = x_ref.shape[0]
    chunk = min(PROJ_ROW_CHUNK, tm)
    for r0 in range(0, tm, chunk):
        rs = slice(r0, r0 + chunk)
        acc = jnp.dot(x_ref[rs, :].astype(BF16), w_ref[...], preferred_element_type=F32)
        for hb, kind in enumerate(head_kinds):
            z = acc[:, hb * LANES:(hb + 1) * LANES]
            if kind is not None:
                z = z * tab_ref[kind, 0, rs, :] + pltpu.roll(z, LANES // 2, 1) * tab_ref[kind, 1, rs, :]
            o_ref[rs, hb * LANES:(hb + 1) * LANES] = z.astype(o_ref.dtype)


def _rotary_proj(x2d, w, tabs, head_kinds, *, seq_len):
    m, k = x2d.shape
    n = w.shape[1]
    tm = min(1024, seq_len)
    assert m % tm == 0 and seq_len % tm == 0 and n == len(head_kinds) * LANES
    bps = seq_len // tm
    return pl.pallas_call(
        functools.partial(_rotary_proj_kernel, head_kinds=tuple(head_kinds)),
        grid=(m // tm,),
        in_specs=[pl.BlockSpec((tm, k), lambda i: (i, 0)),
                  pl.BlockSpec((k, n), lambda i: (0, 0), pipeline_mode=pl.Buffered(1)),
                  pl.BlockSpec((tabs.shape[0], 2, tm, LANES), lambda i: (0, 0, i % bps, 0))],
        out_specs=pl.BlockSpec((tm, n), lambda i: (i, 0)),
        out_shape=jax.ShapeDtypeStruct((m, n), BF16),
        compiler_params=_params("parallel"),
        name="rotary_proj",
    )(x2d, w, tabs)


def _softplus(x):
    return jnp.maximum(x, 0.0) + jnp.log(1.0 + jnp.exp(-jnp.abs(x)))


def _split_dot(x, ones_bd):
    hi = x.astype(BF16)
    lo = (x - hi.astype(F32)).astype(BF16)
    return (jnp.dot(hi, ones_bd, preferred_element_type=F32)
            + jnp.dot(lo, ones_bd, preferred_element_type=F32))


def _rwkv_prep(z, prev, mix, wl, par, ones_bd):
    d = RWKV_DIM
    pw = 2 * RWKV_HEAD_DIM
    row = lax.broadcasted_iota(jnp.int32, z.shape, 0)
    shifted = jnp.where(row == 0, prev, pltpu.roll(z, 1, 0))
    zz = z + (shifted - z) * mix
    r, k, v, lo = zz[:, 0:d], zz[:, d:2 * d], zz[:, 2 * d:3 * d], zz[:, 3 * d:]
    lane = lax.broadcasted_iota(jnp.int32, lo.shape, 1)
    act = jnp.where(lane < DECAY_LORA, jnp.tanh(lo),
                    jnp.where(lane < DECAY_LORA + ICLR_LORA, lo, jax.nn.sigmoid(lo)))
    lora = _bdot(act, wl)
    w0, a0, k_k, k_a = par[0:1, :], par[1:2, :], par[2:3, :], par[3:4, :]
    w = -_softplus(-(w0 + lora[:, 0:d])) - 0.5
    a = jax.nn.sigmoid(a0 + lora[:, d:2 * d])
    kk = k * k_k
    sq = kk * kk
    ssq = jnp.concatenate([_split_dot(sq[:, p0:p0 + pw], ones_bd) for p0 in range(0, d, pw)], axis=1)
    kk = kk / jnp.maximum(jnp.sqrt(ssq), 1e-12)
    return r, -jnp.exp(w), k * (1.0 + (a - 1.0) * k_a), v, -kk, kk * a, lora[:, 2 * d:3 * d]


def _rwkv_chunk_kernel(x_ref, wr_ref, mix_ref, wl_ref, par_ref, o_ref, s_ref, prev_ref, *, chunks):
    c, n = RWKV_CHUNK, RWKV_HEAD_DIM
    pw = 2 * n
    assert c == n and pw == LANES
    pairs = RWKV_HEADS // 2
    units = [(ch, p) for ch in range(chunks) for p in range(pairs)]

    first = pl.program_id(1) == 0

    @pl.when(first)
    def _():
        s_ref[...] = jnp.zeros_like(s_ref)

    row = lax.broadcasted_iota(jnp.int32, (c, pw), 0)
    col = lax.broadcasted_iota(jnp.int32, (c, pw), 1) & (n - 1)
    tril_inc = col <= row
    tril_str = col < row
    eye = col == row
    level_masks = [((row >> (lvl + 1)) == (col >> (lvl + 1))) & (((row >> lvl) & 1) == 1) & (((col >> lvl) & 1) == 0)
                   for lvl in range(c.bit_length() - 1)]
    r2 = lax.broadcasted_iota(jnp.int32, (pw, pw), 0)
    c2 = lax.broadcasted_iota(jnp.int32, (pw, pw), 1)
    same_head = (r2 < n) == (c2 < n)
    eye2 = r2 == c2
    ones_bd = jnp.where(same_head, 1.0, 0.0).astype(BF16)

    def stack(x):
        xb = x.astype(BF16)
        return jnp.where(same_head, jnp.concatenate([xb, xb], axis=0), jnp.zeros((), BF16))

    rows = chunks * c
    z = jnp.dot(x_ref[...].astype(BF16), wr_ref[...], preferred_element_type=F32)
    prev = jnp.where(first, 0.0, prev_ref[SUBLANES - 1:SUBLANES, :])
    r_all, lw, kp_all, v_all, na_all, b_all, g_all = _rwkv_prep(
        z, prev, mix_ref[...], wl_ref[...], par_ref[...], ones_bd)
    prev_ref[...] = z[rows - SUBLANES:, :]

    tr = lax.broadcasted_iota(jnp.int32, (rows, rows), 0)
    tc = lax.broadcasted_iota(jnp.int32, (rows, rows), 1)
    chunk_tril = ((tr >> (c.bit_length() - 1)) == (tc >> (c.bit_length() - 1))) & (tc <= tr)
    chunk_tril = jnp.where(chunk_tril, 1.0, 0.0).astype(BF16)
    lw_hi = lw.astype(BF16)
    lw_r1 = lw - lw_hi.astype(F32)
    lw_mid = lw_r1.astype(BF16)
    lw_lo = (lw_r1 - lw_mid.astype(F32)).astype(BF16)
    cum = (jnp.dot(chunk_tril, lw_hi, preferred_element_type=F32)
           + jnp.dot(chunk_tril, lw_mid, preferred_element_type=F32)
           + jnp.dot(chunk_tril, lw_lo, preferred_element_type=F32))
    r_k, lnx_g, lnx_b = par_ref[4:5, :], par_ref[5:6, :], par_ref[6:7, :]

    at, rt, bt, kt, bh, kh, vv, p_end = {}, {}, {}, {}, {}, {}, {}, {}
    for ch in range(chunks):
        rs = slice(ch * c, (ch + 1) * c)
        cum_c = cum[rs]
        cum_end = cum_c[c - 1:c, :]
        e_inv = jnp.exp(-cum_c)
        e_end = jnp.exp(cum_end - cum_c)
        at_c = na_all[rs] * jnp.exp(cum_c - lw[rs])
        rt_c = r_all[rs] * jnp.exp(cum_c)
        b_c, kp_c = b_all[rs], kp_all[rs]
        bt_c, kt_c, bh_c, kh_c = b_c * e_inv, kp_c * e_inv, b_c * e_end, kp_c * e_end
        pe_c = jnp.exp(cum_end)
        for p in range(pairs):
            ls = slice(p * pw, (p + 1) * pw)
            u = (ch, p)
            at[u], rt[u], bt[u], kt[u], bh[u], kh[u] = at_c[:, ls], rt_c[:, ls], bt_c[:, ls], kt_c[:, ls], bh_c[:, ls], kh_c[:, ls]
            vv[u], p_end[u] = v_all[rs, ls], pe_c[:, ls]

    ar = {u: jnp.concatenate([at[u], rt[u]], axis=0) for u in units}
    g = {u: _bdot(ar[u], jnp.concatenate([stack(bt[u]), stack(kt[u])], axis=0), _NT)
         for u in units}
    a_ab = {u: jnp.where(tril_str, g[u][:c, :pw], 0.0) for u in units}
    a_rb = {u: jnp.where(tril_inc, g[u][c:, :pw], 0.0) for u in units}
    a_ak = {u: jnp.where(tril_str, g[u][:c, pw:], 0.0) for u in units}
    a_rk = {u: jnp.where(tril_inc, g[u][c:, pw:], 0.0) for u in units}
    av = {u: _bdot(jnp.concatenate([a_ak[u], a_rk[u]], axis=0), stack(vv[u])) for u in units}
    ak_v = {u: av[u][:c] for u in units}
    inv = {u: jnp.where(eye, 1.0, jnp.where(level_masks[0], a_ab[u], 0.0)) for u in units}
    for mask in level_masks[1:]:
        e = {u: _bdot(jnp.where(mask, a_ab[u], 0.0), stack(inv[u])) for u in units}
        inv = {u: inv[u] + _bdot(inv[u], stack(e[u])) for u in units}
    wu = {u: _bdot(inv[u], jnp.concatenate([stack(at[u]), stack(ak_v[u])], axis=1)) for u in units}
    w_mat = {u: wu[u][:, :pw] for u in units}
    u_loc = {u: wu[u][:, pw:] for u in units}
    wub = {u: _bdot(wu[u], bh[u], _TN) for u in units}
    m_c = {u: jnp.where(same_head, wub[u][:pw], 0.0) + jnp.where(eye2, p_end[u], 0.0) for u in units}
    n_c = {u: jnp.where(same_head, wub[u][pw:] + _bdot(vv[u], kh[u], _TN), 0.0) for u in units}
    qy = {u: _bdot(a_rb[u], jnp.concatenate([stack(w_mat[u]), stack(u_loc[u])], axis=1)) for u in units}
    q_c = {u: rt[u] + qy[u][:, :pw] for u in units}
    y_loc = {u: qy[u][:, pw:] + av[u][c:] for u in units}

    state = [s_ref[p] for p in range(pairs)]
    y = {}
    for ch in range(chunks):
        for p in range(pairs):
            u = (ch, p)
            y[u] = _bdot(q_c[u], state[p], _NT) + y_loc[u]
            state[p] = _bdot(state[p], m_c[u]) + n_c[u]
    for p in range(pairs):
        s_ref[p] = state[p]

    inv_n = 1.0 / n
    mu = {u: _bdot(y[u], ones_bd) * inv_n for u in units}
    var = {u: _bdot(jnp.square(y[u] - mu[u]), ones_bd) * inv_n for u in units}
    for (ch, p) in units:
        u = (ch, p)
        rs, ls = slice(ch * c, (ch + 1) * c), slice(p * pw, (p + 1) * pw)
        yn = (y[u] - mu[u]) * lax.rsqrt(var[u] + RWKV_GN_EPS) * lnx_g[:, ls] + lnx_b[:, ls]
        bonus = _bdot(r_all[rs, ls] * kp_all[rs, ls] * r_k[:, ls], ones_bd) * vv[u]
        o_ref[rs, ls] = ((yn + bonus) * g_all[rs, ls]).astype(o_ref.dtype)


def _rwkv_scan(x2d, w_r, mix, wl, par, *, batch, seq_len):
    c = RWKV_CHUNK
    chunks = RWKV_SCAN_CHUNKS
    rows = chunks * c
    dm, zc = w_r.shape
    assert seq_len % rows == 0
    const = lambda a: pl.BlockSpec(a.shape, lambda bi, ci: (0, 0), pipeline_mode=pl.Buffered(1))
    out = pl.pallas_call(
        functools.partial(_rwkv_chunk_kernel, chunks=chunks),
        grid=(batch, seq_len // rows),
        in_specs=[pl.BlockSpec((None, rows, dm), lambda bi, ci: (bi, ci, 0)),
                  const(w_r), const(mix), const(wl), const(par)],
        out_specs=pl.BlockSpec((None, rows, RWKV_DIM), lambda bi, ci: (bi, ci, 0)),
        out_shape=jax.ShapeDtypeStruct((batch, seq_len, RWKV_DIM), BF16),
        scratch_shapes=[pltpu.VMEM((RWKV_HEADS // 2, 2 * RWKV_HEAD_DIM, 2 * RWKV_HEAD_DIM), F32),
                        pltpu.VMEM((SUBLANES, zc), F32)],
        compiler_params=_params("parallel", "arbitrary"),
        name="rwkv_scan",
    )(x2d.reshape(batch, seq_len, dm), w_r, mix, wl, par)
    return out.reshape(batch * seq_len, RWKV_DIM)


def _moba_kernel(q_ref, k_ref, v_ref, pool_ref, o_ref, kext_ref, vext_ref, s_ref):
    blk, dh = MOBA_BLOCK, MOBA_HEAD_DIM
    t_len = k_ref.shape[0]
    nb = t_len // blk
    nbp = -(-nb // SUBLANES) * SUBLANES
    shift = blk.bit_length() - 1
    key_tile = 2 * blk

    lane = lax.broadcasted_iota(jnp.int32, (blk, LANES), 1)
    for j in range(nb):
        rs = slice(j * blk, (j + 1) * blk)
        kext_ref[rs, 0:dh] = k_ref[rs, :]
        kext_ref[rs, dh:dh + LANES] = jnp.where(lane == j, 1.0, 0.0).astype(BF16)
        vext_ref[rs, 0:dh] = v_ref[rs, :]
        vext_ref[rs, dh:dh + LANES] = jnp.ones((blk, LANES), BF16)

    km = jnp.dot(pool_ref[...], k_ref[...], preferred_element_type=F32)[0:nbp]
    km_hi = km.astype(BF16)
    km_lo = (km - km_hi.astype(F32)).astype(BF16)
    q_all = q_ref[...]
    gate = (lax.dot_general(km_hi, q_all, _NT, preferred_element_type=F32)
            + lax.dot_general(km_lo, q_all, _NT, preferred_element_type=F32))
    bidx = lax.broadcasted_iota(jnp.int32, gate.shape, 0)
    qblk = lax.broadcasted_iota(jnp.int32, gate.shape, 1) >> shift
    past = bidx < qblk
    gate = jnp.where(past, gate, NEG_INF)
    rank = jnp.zeros(gate.shape, jnp.int32)
    for jp in range(nb):
        other = gate[jp:jp + 1, :]
        rank = rank + ((other > gate) | ((other == gate) & (jp < bidx))).astype(jnp.int32)
    allowed = (past & (rank < MOBA_TOPK)) | (bidx == qblk)
    bias_t = jnp.where(allowed, 0.0, NEG_INF).astype(BF16)
    bias_t = jnp.concatenate([bias_t, jnp.zeros((LANES - nbp, t_len), BF16)], axis=0)
    r2 = lax.broadcasted_iota(jnp.int32, (LANES, LANES), 0)
    c2 = lax.broadcasted_iota(jnp.int32, (LANES, LANES), 1)
    eye = jnp.where(r2 == c2, 1.0, 0.0).astype(BF16)
    row = lax.broadcasted_iota(jnp.int32, (blk, blk), 0)
    col = lax.broadcasted_iota(jnp.int32, (blk, blk), 1)
    causal = col <= row
    colmax = lambda x: functools.reduce(jnp.maximum, [x[:, a:a + LANES] for a in range(0, x.shape[1], LANES)])

    for i in range(nb):
        qs = slice(i * blk, (i + 1) * blk)
        bias = lax.dot_general(bias_t[:, qs], eye, _TN, preferred_element_type=F32)
        q_ext = jnp.concatenate([q_ref[qs, :], bias.astype(BF16)], axis=1)
        nk = (i + 1) * blk
        tiles = [(a, min(a + key_tile, nk)) for a in range(0, nk, key_tile)]
        m_lane = None
        for a, e in tiles:
            s = lax.dot_general(q_ext, kext_ref[a:e, :], _NT, preferred_element_type=F32)
            if e == nk:
                s_own = jnp.where(causal, s[:, e - a - blk:], NEG_INF)
                s = s_own if e - a == blk else jnp.concatenate([s[:, :e - a - blk], s_own], axis=1)
            s_ref[:, a:e] = s
            part = colmax(s)
            m_lane = part if m_lane is None else jnp.maximum(m_lane, part)
        m = jnp.max(m_lane, axis=-1, keepdims=True)
        acc = None
        for a, e in tiles:
            p = jnp.exp((s_ref[:, a:e] - m).astype(BF16))
            pv = jnp.dot(p, vext_ref[a:e, :], preferred_element_type=F32)
            acc = pv if acc is None else acc + pv
        o_ref[qs, :] = (acc[:, :dh] / acc[:, dh:]).astype(o_ref.dtype)


def _moba(qkv, *, batch, seq_len):
    blk, dh, nh = MOBA_BLOCK, MOBA_HEAD_DIM, MOBA_HEADS
    assert seq_len % blk == 0 and seq_len // blk <= LANES and dh == LANES
    pool = (jnp.arange(LANES)[:, None] == (jnp.arange(seq_len) // blk)[None, :]).astype(BF16) / blk
    qkv3 = qkv.reshape(batch, seq_len, 3 * MOBA_WIDTH)
    head = lambda off: pl.BlockSpec((None, seq_len, dh), lambda b, h: (b, 0, off + h))
    out = pl.pallas_call(
        _moba_kernel,
        grid=(batch, nh),
        in_specs=[head(0), head(nh), head(2 * nh), pl.BlockSpec((LANES, seq_len), lambda b, h: (0, 0))],
        out_specs=head(0),
        out_shape=jax.ShapeDtypeStruct((batch, seq_len, MOBA_WIDTH), BF16),
        scratch_shapes=[pltpu.VMEM((seq_len, dh + LANES), BF16), pltpu.VMEM((seq_len, dh + LANES), BF16),
                        pltpu.VMEM((blk, seq_len), F32)],
        compiler_params=_params("parallel", "parallel"),
        name="moba",
    )(qkv3, qkv3, qkv3, pool)
    return out.reshape(batch * seq_len, MOBA_WIDTH)


def _residue_perm(d):
    p = DIL_PERM_ROWS
    idx = jnp.arange(p)
    nat = (idx % (p // d)) * d + idx // (p // d)
    return (nat[:, None] == idx[None, :]).astype(BF16)


def _dilated_proj_kernel(x_ref, w_ref, tab_ref, *rest, dilations):
    n = len(dilations)
    p_refs, o_refs, xb_ref = rest[:n], rest[n:2 * n], rest[2 * n]
    j = pl.program_id(1)

    @pl.when(j == 0)
    def _():
        xb_ref[...] = x_ref[...].astype(BF16)

    tm = x_ref.shape[0]
    grp = DIL_PERM_ROWS
    nat_ref = o_refs[0]
    for r0 in range(0, tm, grp):
        rs = slice(r0, r0 + grp)
        acc = jnp.dot(xb_ref[rs, :], w_ref[...], preferred_element_type=F32)
        t_z, t_partner = tab_ref[0, rs, :], tab_ref[1, rs, :]
        for hb in range(acc.shape[1] // LANES):
            z = acc[:, hb * LANES:(hb + 1) * LANES]
            nat_ref[0, rs, hb * LANES:(hb + 1) * LANES] = (
                z * t_z + pltpu.roll(z, LANES // 2, 1) * t_partner).astype(nat_ref.dtype)

    for g in range(1, n):
        d = dilations[g]
        w = grp // d

        @pl.when((j == g) | (j >= n))
        def _(g=g, d=d, w=w):
            for gi in range(tm // grp):
                y = jnp.dot(p_refs[g][...], nat_ref[0, gi * grp:(gi + 1) * grp, :],
                            preferred_element_type=F32).astype(o_refs[g].dtype)
                for r in range(d):
                    o_refs[g][r, gi * w:(gi + 1) * w, :] = y[r * w:(r + 1) * w, :]


def _dilated_proj(x2d, w, tabs, dilations, *, batch, seq_len):
    m, k = x2d.shape
    n = len(dilations)
    tm = min(1024, seq_len)
    assert dilations[0] == 1 and w.shape[1] == (n + 2) * C_WIDTH
    assert m % tm == 0 and seq_len % tm == 0 and tm % DIL_PERM_ROWS == 0
    assert all(DIL_PERM_ROWS % (d * 2 * SUBLANES) == 0 for d in dilations[1:])
    bps = seq_len // tm
    perms = [_residue_perm(d) for d in dilations]

    def tab_map(i, j):
        return ((j >= n).astype(jnp.int32) + (j > n).astype(jnp.int32), 0, i % bps, 0)

    def out_map(g):
        return lambda i, j: (i // bps, 0, i % bps, (j > g).astype(jnp.int32) + (j > n).astype(jnp.int32))

    outs = pl.pallas_call(
        functools.partial(_dilated_proj_kernel, dilations=tuple(dilations)),
        grid=(m // tm, n + 2),
        in_specs=[pl.BlockSpec((tm, k), lambda i, j: (i, 0)),
                  pl.BlockSpec((k, C_WIDTH), lambda i, j: (0, j)),
                  pl.BlockSpec((None, 2, tm, LANES), tab_map)]
                 + [pl.BlockSpec(p.shape, lambda i, j: (0, 0)) for p in perms],
        out_specs=[pl.BlockSpec((None, d, tm // d, C_WIDTH), out_map(g)) for g, d in enumerate(dilations)],
        out_shape=[jax.ShapeDtypeStruct((batch, d, seq_len // d, 3 * C_WIDTH), BF16) for d in dilations],
        scratch_shapes=[pltpu.VMEM((tm, k), BF16)],
        compiler_params=_params("parallel", "arbitrary"),
        name="dilated_proj",
    )(x2d, w, tabs, *perms)
    return outs


def _dilated_kernel(q_ref, k_ref, kp_ref, v_ref, vp_ref, o_ref, lse_ref):
    w, dh = DIL_WINDOW, DIL_HEAD_DIM
    n_res, tiles = q_ref.shape[0], q_ref.shape[1] // w
    has_prev = pl.program_id(2) > 0
    row = lax.broadcasted_iota(jnp.int32, (w, 2 * w), 0)
    col = lax.broadcasted_iota(jnp.int32, (w, 2 * w), 1)
    band = (col >= row) & (col <= row + w)
    first_band = band & (has_prev | (col >= w))
    er = lax.broadcasted_iota(jnp.int32, (w, w), 0)
    ec = lax.broadcasted_iota(jnp.int32, (w, w), 1)
    eye = er == ec
    units = [(r, h, t) for r in range(n_res) for h in range(DIL_HEADS) for t in range(tiles)]

    def banded(ref, halo_ref, r, h, t):
        hs = slice(h * dh, (h + 1) * dh)
        if t == 0:
            return jnp.concatenate([halo_ref[r, :, hs], ref[r, 0:w, hs]], axis=0)
        return ref[r, (t - 1) * w:(t + 1) * w, hs]

    s = {}
    for r, h, t in units:
        q = q_ref[r, t * w:(t + 1) * w, h * dh:(h + 1) * dh]
        sc = lax.dot_general(q, banded(k_ref, kp_ref, r, h, t), _NT, preferred_element_type=F32)
        s[r, h, t] = jnp.where(first_band if t == 0 else band, sc, NEG_INF)
    m = {u: jnp.max(jnp.maximum(s[u][:, :w], s[u][:, w:]), axis=-1, keepdims=True) for u in units}
    p = {u: jnp.exp((s[u] - m[u]).astype(BF16)) for u in units}
    ones = jnp.ones((2 * w, dh), BF16)
    ol = {u: jnp.dot(p[u], jnp.concatenate([banded(v_ref, vp_ref, *u), ones], axis=1),
                     preferred_element_type=F32) for u in units}
    for r, h, t in units:
        o, l = ol[r, h, t][:, :dh], ol[r, h, t][:, dh:]
        o_ref[r, t * w:(t + 1) * w, h * dh:(h + 1) * dh] = (o / l).astype(o_ref.dtype)
        lse = m[r, h, t] + jnp.log(l)
        lse_ref[r, t, h:h + 1, :] = jnp.sum(jnp.where(eye, lse, 0.0), axis=0, keepdims=True)


def _dilated_group(src, cols, *, seq_len):
    batch, d, length, _ = src.shape
    w = DIL_WINDOW
    rows = min(DIL_ROWS, length)
    n_res = min(d, DIL_ROWS // rows)
    assert length % rows == 0 and rows % w == 0 and length * d == seq_len and d % n_res == 0
    tiles = rows // w
    qc, kc, vc = cols
    blk = lambda cb: pl.BlockSpec((None, n_res, rows, C_WIDTH), lambda b, r, i: (b, r, i, cb))
    halo = lambda cb: pl.BlockSpec(
        (None, n_res, w, C_WIDTH), lambda b, r, i: (b, r, jnp.maximum(i * tiles - 1, 0), cb))
    o, lse = pl.pallas_call(
        _dilated_kernel,
        grid=(batch, d // n_res, length // rows),
        in_specs=[blk(qc), blk(kc), halo(kc), blk(vc), halo(vc)],
        out_specs=[
            pl.BlockSpec((None, n_res, rows, C_WIDTH), lambda b, r, i: (b, r, i, 0)),
            pl.BlockSpec((None, n_res, tiles, DIL_HEADS, w), lambda b, r, i: (b, r, i, 0, 0)),
        ],
        out_shape=[
            jax.ShapeDtypeStruct((batch, d, length, C_WIDTH), BF16),
            jax.ShapeDtypeStruct((batch, d, length // w, DIL_HEADS, w), F32),
        ],
        compiler_params=_params("parallel", "parallel", "arbitrary"),
        name=f"dilated_d{d}",
    )(src, src, src, src, src)
    lse = lse.transpose(0, 2, 4, 1, 3).reshape(batch, seq_len, DIL_HEADS)
    return o, lse


def _mlp_ln(x1_ref, w1_ref, w2_ref, ln2_g, ln2_b):
    xb = x1_ref[...].astype(BF16)
    hs = []
    for c0 in range(0, w1_ref.shape[1], MLP_HIDDEN_CHUNK):
        h = jnp.dot(xb, w1_ref[:, c0:c0 + MLP_HIDDEN_CHUNK], preferred_element_type=F32)
        hs.append(jnp.square(jnp.maximum(h, 0.0)).astype(BF16))
    y = jnp.dot(jnp.concatenate(hs, axis=1), w2_ref[...], preferred_element_type=F32)
    return _layer_norm(DEEPNORM_ALPHA * x1_ref[...] + y, ln2_g, ln2_b)


def _outproj_mlp_kernel(*refs, n_act):
    acts = refs[:n_act]
    w_ref, x_ref, g1_ref, b1_ref, w1_ref, w2_ref, g2_ref, b2_ref, o_ref, x1_ref = refs[n_act:]

    @pl.when(pl.program_id(0) == 0)
    def _():
        x1_ref[...] = jnp.zeros_like(x1_ref)

    o_ref[...] = _mlp_ln(x1_ref, w1_ref, w2_ref, g2_ref[...], b2_ref[...])
    proj = None
    off = 0
    for a_ref in acts:
        kw = a_ref.shape[1]
        part = jnp.dot(a_ref[...], w_ref[off:off + kw, :], preferred_element_type=F32)
        proj = part if proj is None else proj + part
        off += kw
    x1_ref[...] = _layer_norm(DEEPNORM_ALPHA * x_ref[...] + proj, g1_ref[...], b1_ref[...])


def _outproj_mlp(acts, w, x2d, ln1, mlp, ln2):
    m, dm = x2d.shape
    tm = LAYER_ROWS
    assert m % tm == 0 and mlp[0].shape[1] % MLP_HIDDEN_CHUNK == 0
    last = m // tm - 1
    rowspec = lambda a: pl.BlockSpec((tm, a.shape[1]), lambda s: (jnp.minimum(s, last), 0))
    const = lambda a: pl.BlockSpec(a.shape, lambda s: (0, 0), pipeline_mode=pl.Buffered(1))
    return pl.pallas_call(
        functools.partial(_outproj_mlp_kernel, n_act=len(acts)),
        grid=(m // tm + 1,),
        in_specs=([rowspec(a) for a in acts] + [const(w), rowspec(x2d)]
                  + [const(a) for a in (*ln1, *mlp, *ln2)]),
        out_specs=pl.BlockSpec((tm, dm), lambda s: (jnp.maximum(s - 1, 0), 0)),
        out_shape=jax.ShapeDtypeStruct((m, dm), F32),
        scratch_shapes=[pltpu.VMEM((tm, dm), F32)],
        compiler_params=_params("arbitrary"),
        name="outproj_mlp",
    )(*acts, w, x2d, *ln1, *mlp, *ln2)


def _merge_outproj_mlp_kernel(*refs, dilations):
    n = len(dilations)
    o_refs, lse_refs, p_refs = refs[:n], refs[n:2 * n], refs[2 * n:3 * n]
    w_ref, x_ref, g1_ref, b1_ref, w1_ref, w2_ref, g2_ref, b2_ref, out_ref, x1_ref = refs[3 * n:]
    nh, dh, grp = DIL_HEADS, DIL_HEAD_DIM, DIL_PERM_ROWS
    tm = x_ref.shape[0]

    @pl.when(pl.program_id(0) == 0)
    def _():
        x1_ref[...] = jnp.zeros_like(x1_ref)

    out_ref[...] = _mlp_ln(x1_ref, w1_ref, w2_ref, g2_ref[...], b2_ref[...])
    branch = []
    for o_ref, p_ref, d in zip(o_refs, p_refs, dilations):
        if d == 1:
            branch.append(o_ref[0].astype(F32))
            continue
        w = grp // d
        nat = []
        for gi in range(tm // grp):
            perm = jnp.concatenate([o_ref[r, gi * w:(gi + 1) * w, :] for r in range(d)], axis=0)
            nat.append(lax.dot_general(p_ref[...], perm, _TN, preferred_element_type=F32))
        branch.append(jnp.concatenate(nat, axis=0))
    parts = [ref[...] for ref in lse_refs]
    m = functools.reduce(jnp.maximum, parts)
    es = [jnp.exp(p - m) for p in parts]
    den = functools.reduce(jnp.add, es)
    wts = [e / den for e in es]
    heads = []
    for h in range(nh):
        hs = slice(h * dh, (h + 1) * dh)
        heads.append(functools.reduce(jnp.add, [wts[g][:, h:h + 1] * branch[g][:, hs] for g in range(n)]))
    proj = _bdot(jnp.concatenate(heads, axis=1), w_ref[...])
    x1_ref[...] = _layer_norm(DEEPNORM_ALPHA * x_ref[...] + proj, g1_ref[...], b1_ref[...])


def _merge_outproj_mlp(outs, lses, dilations, w, x2d, ln1, mlp, ln2, *, batch, seq_len):
    dm = x2d.shape[1]
    tm = LAYER_ROWS
    assert seq_len % tm == 0 and tm % DIL_PERM_ROWS == 0 and mlp[0].shape[1] % MLP_HIDDEN_CHUNK == 0
    x3 = x2d.reshape(batch, seq_len, dm)
    perms = [_residue_perm(d) for d in dilations]
    bps = seq_len // tm
    n_tiles = batch * bps
    tile = lambda s: jnp.minimum(s, n_tiles - 1)
    done = lambda s: jnp.maximum(s - 1, 0)
    const = lambda a: pl.BlockSpec(a.shape, lambda s: (0,) * a.ndim, pipeline_mode=pl.Buffered(1))
    rows3 = lambda a: pl.BlockSpec((None, tm, a.shape[2]), lambda s: (tile(s) // bps, tile(s) % bps, 0))
    in_specs = ([pl.BlockSpec((None, d, tm // d, C_WIDTH), lambda s: (tile(s) // bps, 0, tile(s) % bps, 0))
                 for d in dilations]
                + [rows3(a) for a in lses] + [const(p) for p in perms] + [const(w), rows3(x3)]
                + [const(a) for a in (*ln1, *mlp, *ln2)])
    out = pl.pallas_call(
        functools.partial(_merge_outproj_mlp_kernel, dilations=tuple(dilations)),
        grid=(n_tiles + 1,),
        in_specs=in_specs,
        out_specs=pl.BlockSpec((None, tm, dm), lambda s: (done(s) // bps, done(s) % bps, 0)),
        out_shape=jax.ShapeDtypeStruct((batch, seq_len, dm), F32),
        scratch_shapes=[pltpu.VMEM((tm, dm), F32)],
        compiler_params=_params("arbitrary"),
        name="merge_outproj_mlp",
    )(*outs, *lses, *perms, w, x3, *ln1, *mlp, *ln2)
    return out.reshape(batch * seq_len, dm)


def _rwkv_moba_layer(x2d, w_in, shift_mix, w0, w_up, a0, a_up, g_up, k_k, k_a, r_k, lnx_g, lnx_b,
                     w_out, ln1, mlp, ln2, *, batch, seq_len):
    d = RWKV_DIM
    tabs = _rotary_tables(seq_len, MOBA_HEAD_DIM ** -0.5)
    w_m = w_in[:, RWKV_PROJ_COLS:].astype(BF16)
    w_m = jnp.concatenate([_rotary_weight(w_m[:, :2 * MOBA_WIDTH], 2 * MOBA_HEADS),
                           w_m[:, 2 * MOBA_WIDTH:]], axis=1)
    kinds = [0] * MOBA_HEADS + [1] * MOBA_HEADS + [None] * MOBA_HEADS
    qkv = _rotary_proj(x2d, w_m, tabs, kinds, seq_len=seq_len)
    y_b = _moba(qkv, batch=batch, seq_len=seq_len)
    pad = RWKV_Z_COLS - RWKV_PROJ_COLS
    w_r = jnp.pad(w_in[:, :RWKV_PROJ_COLS], ((0, 0), (0, pad))).astype(BF16)
    mix = jnp.pad(shift_mix, (0, pad))[None, :]
    wl = jnp.zeros((LORA_PAD, 3 * d), F32)
    wl = wl.at[0:DECAY_LORA, 0:d].set(w_up)
    wl = wl.at[DECAY_LORA:DECAY_LORA + ICLR_LORA, d:2 * d].set(a_up)
    wl = wl.at[DECAY_LORA + ICLR_LORA:LORA_COLS, 2 * d:3 * d].set(g_up).astype(BF16)
    par = jnp.stack([w0, a0, k_k, k_a, r_k.reshape(d), lnx_g, lnx_b, jnp.zeros((d,), F32)])
    y_a = _rwkv_scan(x2d, w_r, mix, wl, par, batch=batch, seq_len=seq_len)
    return _outproj_mlp([y_a, y_b], w_out.astype(BF16), x2d, ln1, mlp, ln2)


def _dilated_layer(x2d, w_in, w_out, ln1, mlp, ln2, *, batch, seq_len):
    tabs = _rotary_tables(seq_len, DIL_HEAD_DIM ** -0.5)
    nqk = (DIL_GROUPS + 1) * C_WIDTH
    w_c = w_in.astype(BF16)
    w_c = jnp.concatenate([_rotary_weight(w_c[:, :nqk], (DIL_GROUPS + 1) * DIL_HEADS), w_c[:, nqk:]], axis=1)
    dilations = [dilation for _, dilation in DIL_PAIRS]
    assert all(span // dilation == DIL_WINDOW for span, dilation in DIL_PAIRS)
    srcs = _dilated_proj(x2d, w_c, tabs, dilations, batch=batch, seq_len=seq_len)
    outs, lses = [], []
    for src in srcs:
        o, lse = _dilated_group(src, (0, 1, 2), seq_len=seq_len)
        outs.append(o)
        lses.append(lse)
    return _merge_outproj_mlp(outs, lses, dilations, w_out.astype(BF16), x2d, ln1, mlp, ln2,
                              batch=batch, seq_len=seq_len)


def kernel(x, ab_w_in, ab_shift_mix, ab_w0, ab_w_up, ab_a0, ab_a_up, ab_g_up, ab_k_k, ab_k_a, ab_r_k,
           ab_lnx_g, ab_lnx_b, ab_w_out, c_w_in, c_w_out, ln1_g, ln1_b, mlp_w1, mlp_w2, ln2_g, ln2_b):
    batch, seq_len, dm = x.shape
    assert dm == D_MODEL
    h = x.reshape(batch * seq_len, dm)
    depth = ln1_g.shape[0]
    for layer in range(depth):
        j = layer // 2
        ln1 = (ln1_g[layer][None, :], ln1_b[layer][None, :])
        ln2 = (ln2_g[layer][None, :], ln2_b[layer][None, :])
        mlp = (mlp_w1[layer].astype(BF16), mlp_w2[layer].astype(BF16))
        if layer % 2 == 0:
            h = _rwkv_moba_layer(h, ab_w_in[j], ab_shift_mix[j], ab_w0[j], ab_w_up[j], ab_a0[j], ab_a_up[j],
                                 ab_g_up[j], ab_k_k[j], ab_k_a[j], ab_r_k[j], ab_lnx_g[j], ab_lnx_b[j],
                                 ab_w_out[j], ln1, mlp, ln2, batch=batch, seq_len=seq_len)
        else:
            h = _dilated_layer(h, c_w_in[j], c_w_out[j], ln1, mlp, ln2, batch=batch, seq_len=seq_len)
    return h.reshape(batch, seq_len, dm)
```

```python
import functools

import jax
import jax.numpy as jnp
from jax import lax
from jax.experimental import pallas as pl
from jax.experimental.pallas import tpu as pltpu

F32 = jnp.float32
BF16 = jnp.bfloat16

D_MODEL = 1024
DEPTH = 2
DEEPNORM_ALPHA = (2 * DEPTH) ** 0.25
LN_EPS = 1e-5
ROPE_THETA = 500000.0
ROPE_FRACTION = 4
RWKV_HEAD_DIM = 64
RWKV_DIM = D_MODEL // 2
RWKV_HEADS = RWKV_DIM // RWKV_HEAD_DIM
DECAY_LORA = 64
ICLR_LORA = 64
GATE_LORA = 160
RWKV_GN_EPS = 64e-5
RWKV_PROJ_COLS = 3 * RWKV_DIM + DECAY_LORA + ICLR_LORA + GATE_LORA
MOBA_HEAD_DIM = 128
MOBA_WIDTH = D_MODEL - RWKV_DIM
MOBA_HEADS = MOBA_WIDTH // MOBA_HEAD_DIM
MOBA_BLOCK = 256
MOBA_TOPK = 3
DIL_PAIRS = ((128, 1), (512, 4), (2048, 16))
DIL_GROUPS = len(DIL_PAIRS)
DIL_HEAD_DIM = 128
DIL_HEADS = D_MODEL // DIL_HEAD_DIM
C_WIDTH = DIL_HEADS * DIL_HEAD_DIM
C_IN_COLS = DIL_GROUPS * C_WIDTH + 2 * C_WIDTH
MLP_HIDDEN = 4 * D_MODEL
NEG_INF = -1e30

LANES = 128
SUBLANES = 8
VMEM_LIMIT_BYTES = 56 * 1024 * 1024

LORA_COLS = DECAY_LORA + ICLR_LORA + GATE_LORA
LORA_PAD = -(-LORA_COLS // LANES) * LANES
RWKV_Z_COLS = 3 * RWKV_DIM + LORA_PAD
RWKV_CHUNK = 64
RWKV_SCAN_CHUNKS = 8
DIL_WINDOW = 128
DIL_ROWS = 512
DIL_PERM_ROWS = 256
PROJ_ROWS = 1024
PROJ_ROW_CHUNK = 256
LAYER_ROWS = 512
MLP_HIDDEN_CHUNK = 1024

_NN = (((1,), (0,)), ((), ()))
_NT = (((1,), (1,)), ((), ()))
_TN = (((0,), (0,)), ((), ()))


def _bdot(a, b, dims=_NN):
    return lax.dot_general(a.astype(BF16), b.astype(BF16), dims, preferred_element_type=F32)


def _layer_norm(y, g, b):
    mu = jnp.mean(y, axis=-1, keepdims=True)
    var = jnp.mean(jnp.square(y - mu), axis=-1, keepdims=True)
    return (y - mu) * lax.rsqrt(var + LN_EPS) * g + b


def _params(*sem):
    return pltpu.CompilerParams(dimension_semantics=sem, vmem_limit_bytes=VMEM_LIMIT_BYTES)


ROT_DIMS = MOBA_HEAD_DIM // ROPE_FRACTION
ROT_HALF = ROT_DIMS // 2


def _rotary_weight(w, n_heads):
    split = ROT_DIMS + LANES // 2 - ROT_HALF
    w3 = w.reshape(w.shape[0], n_heads, LANES)
    w3 = jnp.concatenate([w3[..., :ROT_HALF], w3[..., ROT_DIMS:split], w3[..., ROT_HALF:ROT_DIMS],
                          w3[..., split:]], axis=-1)
    return w3.reshape(w.shape[0], n_heads * LANES)


def _rotary_tables(seq_len, q_scale):
    inv_freq = ROPE_THETA ** (-jnp.arange(ROT_HALF, dtype=F32) / ROT_HALF)
    ang = jnp.arange(seq_len).astype(F32)[:, None] * inv_freq[None, :]
    cos, sin = jnp.cos(ang), jnp.sin(ang)
    pad = lambda a, fill: jnp.concatenate([a, jnp.full((seq_len, LANES // 2 - ROT_HALF), fill, F32)], axis=1)
    c = jnp.concatenate([pad(cos, 1.0), pad(cos, 1.0)], axis=1)
    s = jnp.concatenate([pad(-sin, 0.0), pad(sin, 0.0)], axis=1)
    one = jnp.ones_like(c)
    return jnp.stack([jnp.stack([c * q_scale, s * q_scale]), jnp.stack([c, s]),
                      jnp.stack([one, jnp.zeros_like(c)])])


def _rotary_proj_kernel(x_ref, w_ref, tab_ref, o_ref, *, head_kinds):
    tm = x_ref.shape[0]
    chunk = min(PROJ_ROW_CHUNK, tm)
    for r0 in range(0, tm, chunk):
        rs = slice(r0, r0 + chunk)
        acc = jnp.dot(x_ref[rs, :].astype(BF16), w_ref[...], preferred_element_type=F32)
        for hb, kind in enumerate(head_kinds):
            z = acc[:, hb * LANES:(hb + 1) * LANES]
            if kind is not None:
                z = z * tab_ref[kind, 0, rs, :] + pltpu.roll(z, LANES // 2, 1) * tab_ref[kind, 1, rs, :]
            o_ref[rs, hb * LANES:(hb + 1) * LANES] = z.astype(o_ref.dtype)


def _rotary_proj(x2d, w, tabs, head_kinds, *, seq_len):
    m, k = x2d.shape
    n = w.shape[1]
    tm = min(PROJ_ROWS, seq_len)
    assert m % tm == 0 and seq_len % tm == 0 and n == len(head_kinds) * LANES
    bps = seq_len // tm
    return pl.pallas_call(
        functools.partial(_rotary_proj_kernel, head_kinds=tuple(head_kinds)),
        grid=(m // tm,),
        in_specs=[pl.BlockSpec((tm, k), lambda i: (i, 0)),
                  pl.BlockSpec((k, n), lambda i: (0, 0), pipeline_mode=pl.Buffered(1)),
                  pl.BlockSpec((tabs.shape[0], 2, tm, LANES), lambda i: (0, 0, i % bps, 0))],
        out_specs=pl.BlockSpec((tm, n), lambda i: (i, 0)),
        out_shape=jax.ShapeDtypeStruct((m, n), BF16),
        compiler_params=_params("parallel"),
        name="rotary_proj",
    )(x2d, w, tabs)


def _softplus(x):
    return jnp.maximum(x, 0.0) + jnp.log(1.0 + jnp.exp(-jnp.abs(x)))


def _split_dot(x, ones_bd):
    hi = x.astype(BF16)
    lo = (x - hi.astype(F32)).astype(BF16)
    return (jnp.dot(hi, ones_bd, preferred_element_type=F32)
            + jnp.dot(lo, ones_bd, preferred_element_type=F32))


def _rwkv_prep(z, prev, mix, wl_ref, par, ones_bd):
    d = RWKV_DIM
    pw = 2 * RWKV_HEAD_DIM
    row = lax.broadcasted_iota(jnp.int32, z.shape, 0)
    shifted = jnp.where(row == 0, prev, pltpu.roll(z, 1, 0))
    zz = z + (shifted - z) * mix
    r, k, v, lo = zz[:, 0:d], zz[:, d:2 * d], zz[:, 2 * d:3 * d], zz[:, 3 * d:]
    lane = lax.broadcasted_iota(jnp.int32, lo.shape, 1)
    act = jnp.where(lane < DECAY_LORA, jnp.tanh(lo),
                    jnp.where(lane < DECAY_LORA + ICLR_LORA, lo, jax.nn.sigmoid(lo)))
    n_wa = DECAY_LORA + ICLR_LORA
    lora_wa = _bdot(act[:, :n_wa], wl_ref[0:n_wa, 0:2 * d])
    lora_g = _bdot(act[:, n_wa:], wl_ref[n_wa:, 2 * d:3 * d])
    w0, a0, k_k, k_a = par[0:1, :], par[1:2, :], par[2:3, :], par[3:4, :]
    w = -_softplus(-(w0 + lora_wa[:, 0:d])) - 0.5
    a = jax.nn.sigmoid(a0 + lora_wa[:, d:2 * d])
    kk = k * k_k
    sq = kk * kk
    ssq = jnp.concatenate([_split_dot(sq[:, p0:p0 + pw], ones_bd) for p0 in range(0, d, pw)], axis=1)
    kk = kk / jnp.maximum(jnp.sqrt(ssq), 1e-12)
    return r, -jnp.exp(w), k * (1.0 + (a - 1.0) * k_a), v, -kk, kk * a, lora_g


def _rwkv_chunk_kernel(x_ref, wr_ref, mix_ref, wl_ref, par_ref, o_ref, s_ref, prev_ref, *, chunks):
    c, n = RWKV_CHUNK, RWKV_HEAD_DIM
    pw = 2 * n
    assert c == n and pw == LANES
    pairs = RWKV_HEADS // 2
    units = [(ch, p) for ch in range(chunks) for p in range(pairs)]

    first = pl.program_id(1) == 0

    @pl.when(first)
    def _():
        s_ref[...] = jnp.zeros_like(s_ref)

    row = lax.broadcasted_iota(jnp.int32, (c, pw), 0)
    col = lax.broadcasted_iota(jnp.int32, (c, pw), 1) & (n - 1)
    tril_inc = col <= row
    tril_str = col < row
    eye = col == row
    level_masks = [((row >> (lvl + 1)) == (col >> (lvl + 1))) & (((row >> lvl) & 1) == 1) & (((col >> lvl) & 1) == 0)
                   for lvl in range(c.bit_length() - 1)]
    r2 = lax.broadcasted_iota(jnp.int32, (pw, pw), 0)
    c2 = lax.broadcasted_iota(jnp.int32, (pw, pw), 1)
    same_head = (r2 < n) == (c2 < n)
    eye2 = r2 == c2
    ones_bd = jnp.where(same_head, 1.0, 0.0).astype(BF16)

    def stack(x):
        xb = x.astype(BF16)
        return jnp.where(same_head, jnp.concatenate([xb, xb], axis=0), jnp.zeros((), BF16))

    rows = chunks * c
    z = jnp.dot(x_ref[...].astype(BF16), wr_ref[...], preferred_element_type=F32)
    prev = jnp.where(first, 0.0, prev_ref[SUBLANES - 1:SUBLANES, :])
    r_all, lw, kp_all, v_all, na_all, b_all, g_all = _rwkv_prep(
        z, prev, mix_ref[...], wl_ref, par_ref[...], ones_bd)
    prev_ref[...] = z[rows - SUBLANES:, :]

    tr = lax.broadcasted_iota(jnp.int32, (pw, pw), 0)
    tc = lax.broadcasted_iota(jnp.int32, (pw, pw), 1)
    chunk_tril = ((tr >> (c.bit_length() - 1)) == (tc >> (c.bit_length() - 1))) & (tc <= tr)
    chunk_tril = jnp.where(chunk_tril, 1.0, 0.0).astype(BF16)
    lw_hi = lw.astype(BF16)
    lw_r1 = lw - lw_hi.astype(F32)
    lw_mid = lw_r1.astype(BF16)
    lw_lo = (lw_r1 - lw_mid.astype(F32)).astype(BF16)
    cum = jnp.concatenate(
        [functools.reduce(jnp.add, [jnp.dot(chunk_tril, piece[r0:r0 + pw], preferred_element_type=F32)
                                    for piece in (lw_hi, lw_mid, lw_lo)])
         for r0 in range(0, rows, pw)], axis=0)
    r_k, lnx_g, lnx_b = par_ref[4:5, :], par_ref[5:6, :], par_ref[6:7, :]

    at, rt, bt, kt, bh, kh, vv, p_end = {}, {}, {}, {}, {}, {}, {}, {}
    for ch in range(chunks):
        rs = slice(ch * c, (ch + 1) * c)
        cum_c = cum[rs]
        cum_end = cum_c[c - 1:c, :]
        e_inv = jnp.exp(-cum_c)
        e_end = jnp.exp(cum_end - cum_c)
        at_c = na_all[rs] * jnp.exp(cum_c - lw[rs])
        rt_c = r_all[rs] * jnp.exp(cum_c)
        b_c, kp_c = b_all[rs], kp_all[rs]
        bt_c, kt_c, bh_c, kh_c = b_c * e_inv, kp_c * e_inv, b_c * e_end, kp_c * e_end
        pe_c = jnp.exp(cum_end)
        for p in range(pairs):
            ls = slice(p * pw, (p + 1) * pw)
            u = (ch, p)
            at[u], rt[u], bt[u], kt[u], bh[u], kh[u] = at_c[:, ls], rt_c[:, ls], bt_c[:, ls], kt_c[:, ls], bh_c[:, ls], kh_c[:, ls]
            vv[u], p_end[u] = v_all[rs, ls], pe_c[:, ls]

    ar = {u: jnp.concatenate([at[u], rt[u]], axis=0) for u in units}
    g = {u: _bdot(ar[u], jnp.concatenate([stack(bt[u]), stack(kt[u])], axis=0), _NT)
         for u in units}
    a_ab = {u: jnp.where(tril_str, g[u][:c, :pw], 0.0) for u in units}
    a_rb = {u: jnp.where(tril_inc, g[u][c:, :pw], 0.0) for u in units}
    a_ak = {u: jnp.where(tril_str, g[u][:c, pw:], 0.0) for u in units}
    a_rk = {u: jnp.where(tril_inc, g[u][c:, pw:], 0.0) for u in units}
    av = {u: _bdot(jnp.concatenate([a_ak[u], a_rk[u]], axis=0), stack(vv[u])) for u in units}
    ak_v = {u: av[u][:c] for u in units}
    inv = {u: jnp.where(eye, 1.0, jnp.where(level_masks[0], a_ab[u], 0.0)) for u in units}
    for mask in level_masks[1:]:
        e = {u: _bdot(jnp.where(mask, a_ab[u], 0.0), stack(inv[u])) for u in units}
        inv = {u: inv[u] + _bdot(inv[u], stack(e[u])) for u in units}
    wu = {u: _bdot(inv[u], jnp.concatenate([stack(at[u]), stack(ak_v[u])], axis=1)) for u in units}
    w_mat = {u: wu[u][:, :pw] for u in units}
    u_loc = {u: wu[u][:, pw:] for u in units}
    wub = {u: _bdot(wu[u], bh[u], _TN) for u in units}
    m_c = {u: jnp.where(same_head, wub[u][:pw], 0.0) + jnp.where(eye2, p_end[u], 0.0) for u in units}
    n_c = {u: jnp.where(same_head, wub[u][pw:] + _bdot(vv[u], kh[u], _TN), 0.0) for u in units}
    qy = {u: _bdot(a_rb[u], jnp.concatenate([stack(w_mat[u]), stack(u_loc[u])], axis=1)) for u in units}
    q_c = {u: rt[u] + qy[u][:, :pw] for u in units}
    y_loc = {u: qy[u][:, pw:] + av[u][c:] for u in units}

    state = [s_ref[p] for p in range(pairs)]
    y = {}
    for ch in range(chunks):
        for p in range(pairs):
            u = (ch, p)
            y[u] = _bdot(q_c[u], state[p], _NT) + y_loc[u]
            state[p] = _bdot(state[p], m_c[u]) + n_c[u]
    for p in range(pairs):
        s_ref[p] = state[p]

    inv_n = 1.0 / n
    mu = {u: _bdot(y[u], ones_bd) * inv_n for u in units}
    var = {u: _bdot(jnp.square(y[u] - mu[u]), ones_bd) * inv_n for u in units}
    for (ch, p) in units:
        u = (ch, p)
        rs, ls = slice(ch * c, (ch + 1) * c), slice(p * pw, (p + 1) * pw)
        yn = (y[u] - mu[u]) * lax.rsqrt(var[u] + RWKV_GN_EPS) * lnx_g[:, ls] + lnx_b[:, ls]
        bonus = _bdot(r_all[rs, ls] * kp_all[rs, ls] * r_k[:, ls], ones_bd) * vv[u]
        o_ref[rs, ls] = ((yn + bonus) * g_all[rs, ls]).astype(o_ref.dtype)


def _rwkv_scan(x2d, w_r, mix, wl, par, *, batch, seq_len):
    c = RWKV_CHUNK
    chunks = RWKV_SCAN_CHUNKS
    rows = chunks * c
    dm, zc = w_r.shape
    assert seq_len % rows == 0
    const = lambda a: pl.BlockSpec(a.shape, lambda bi, ci: (0, 0), pipeline_mode=pl.Buffered(1))
    out = pl.pallas_call(
        functools.partial(_rwkv_chunk_kernel, chunks=chunks),
        grid=(batch, seq_len // rows),
        in_specs=[pl.BlockSpec((None, rows, dm), lambda bi, ci: (bi, ci, 0)),
                  const(w_r), const(mix), const(wl), const(par)],
        out_specs=pl.BlockSpec((None, rows, RWKV_DIM), lambda bi, ci: (bi, ci, 0)),
        out_shape=jax.ShapeDtypeStruct((batch, seq_len, RWKV_DIM), BF16),
        scratch_shapes=[pltpu.VMEM((RWKV_HEADS // 2, 2 * RWKV_HEAD_DIM, 2 * RWKV_HEAD_DIM), F32),
                        pltpu.VMEM((SUBLANES, zc), F32)],
        compiler_params=_params("parallel", "arbitrary"),
        name="rwkv_scan",
    )(x2d.reshape(batch, seq_len, dm), w_r, mix, wl, par)
    return out.reshape(batch * seq_len, RWKV_DIM)


def _moba_kernel(q_ref, k_ref, v_ref, pool_ref, o_ref, kext_ref, vext_ref, s_ref):
    blk, dh = MOBA_BLOCK, MOBA_HEAD_DIM
    t_len = k_ref.shape[0]
    nb = t_len // blk
    nbp = -(-nb // SUBLANES) * SUBLANES
    shift = blk.bit_length() - 1
    key_tile = 2 * blk

    lane = lax.broadcasted_iota(jnp.int32, (blk, LANES), 1)
    for j in range(nb):
        rs = slice(j * blk, (j + 1) * blk)
        kext_ref[rs, 0:dh] = k_ref[rs, :]
        kext_ref[rs, dh:dh + LANES] = jnp.where(lane == j, 1.0, 0.0).astype(BF16)
        vext_ref[rs, 0:dh] = v_ref[rs, :]
        vext_ref[rs, dh:dh + LANES] = jnp.ones((blk, LANES), BF16)

    km = jnp.dot(pool_ref[...], k_ref[...], preferred_element_type=F32)[0:nbp]
    km_hi = km.astype(BF16)
    km_lo = (km - km_hi.astype(F32)).astype(BF16)
    q_all = q_ref[...]
    gate = (lax.dot_general(km_hi, q_all, _NT, preferred_element_type=F32)
            + lax.dot_general(km_lo, q_all, _NT, preferred_element_type=F32))
    bidx = lax.broadcasted_iota(jnp.int32, gate.shape, 0)
    qblk = lax.broadcasted_iota(jnp.int32, gate.shape, 1) >> shift
    past = bidx < qblk
    gate = jnp.where(past, gate, NEG_INF)
    rank = jnp.zeros(gate.shape, jnp.int32)
    for jp in range(nb):
        other = gate[jp:jp + 1, :]
        rank = rank + ((other > gate) | ((other == gate) & (jp < bidx))).astype(jnp.int32)
    allowed = (past & (rank < MOBA_TOPK)) | (bidx == qblk)
    bias_t = jnp.where(allowed, 0.0, NEG_INF).astype(BF16)
    bias_t = jnp.concatenate([bias_t, jnp.zeros((LANES - nbp, t_len), BF16)], axis=0)
    r2 = lax.broadcasted_iota(jnp.int32, (LANES, LANES), 0)
    c2 = lax.broadcasted_iota(jnp.int32, (LANES, LANES), 1)
    eye = jnp.where(r2 == c2, 1.0, 0.0).astype(BF16)
    row = lax.broadcasted_iota(jnp.int32, (blk, blk), 0)
    col = lax.broadcasted_iota(jnp.int32, (blk, blk), 1)
    causal = col <= row
    colmax = lambda x: functools.reduce(jnp.maximum, [x[:, a:a + LANES] for a in range(0, x.shape[1], LANES)])

    for i in range(nb):
        qs = slice(i * blk, (i + 1) * blk)
        bias = lax.dot_general(bias_t[:, qs], eye, _TN, preferred_element_type=F32)
        q_ext = jnp.concatenate([q_ref[qs, :], bias.astype(BF16)], axis=1)
        nk = (i + 1) * blk
        tiles = [(a, min(a + key_tile, nk)) for a in range(0, nk, key_tile)]
        m_lane = None
        for a, e in tiles:
            s = lax.dot_general(q_ext, kext_ref[a:e, :], _NT, preferred_element_type=F32)
            if e == nk:
                s_own = jnp.where(causal, s[:, e - a - blk:], NEG_INF)
                s = s_own if e - a == blk else jnp.concatenate([s[:, :e - a - blk], s_own], axis=1)
            s_ref[:, a:e] = s
            part = colmax(s)
            m_lane = part if m_lane is None else jnp.maximum(m_lane, part)
        m = jnp.max(m_lane, axis=-1, keepdims=True)
        acc = None
        for a, e in tiles:
            p = jnp.exp((s_ref[:, a:e] - m).astype(BF16))
            pv = jnp.dot(p, vext_ref[a:e, :], preferred_element_type=F32)
            acc = pv if acc is None else acc + pv
        o_ref[qs, :] = (acc[:, :dh] / acc[:, dh:]).astype(o_ref.dtype)


def _moba(qkv, *, batch, seq_len):
    blk, dh, nh = MOBA_BLOCK, MOBA_HEAD_DIM, MOBA_HEADS
    assert seq_len % blk == 0 and seq_len // blk <= LANES and dh == LANES
    pool = (jnp.arange(LANES)[:, None] == (jnp.arange(seq_len) // blk)[None, :]).astype(BF16) / blk
    qkv3 = qkv.reshape(batch, seq_len, 3 * MOBA_WIDTH)
    head = lambda off: pl.BlockSpec((None, seq_len, dh), lambda b, h: (b, 0, off + h))
    out = pl.pallas_call(
        _moba_kernel,
        grid=(batch, nh),
        in_specs=[head(0), head(nh), head(2 * nh), pl.BlockSpec((LANES, seq_len), lambda b, h: (0, 0))],
        out_specs=head(0),
        out_shape=jax.ShapeDtypeStruct((batch, seq_len, MOBA_WIDTH), BF16),
        scratch_shapes=[pltpu.VMEM((seq_len, dh + LANES), BF16), pltpu.VMEM((seq_len, dh + LANES), BF16),
                        pltpu.VMEM((blk, seq_len), F32)],
        compiler_params=_params("parallel", "parallel"),
        name="moba",
    )(qkv3, qkv3, qkv3, pool)
    return out.reshape(batch * seq_len, MOBA_WIDTH)


def _residue_perm(d):
    p = DIL_PERM_ROWS
    idx = jnp.arange(p)
    nat = (idx % (p // d)) * d + idx // (p // d)
    return (nat[:, None] == idx[None, :]).astype(BF16)


def _dilated_proj_kernel(x_ref, w_ref, tab_ref, *rest, dilations):
    n = len(dilations)
    p_refs, o_refs, xb_ref = rest[:n], rest[n:2 * n], rest[2 * n]
    j = pl.program_id(1)

    @pl.when(j == 0)
    def _():
        xb_ref[...] = x_ref[...].astype(BF16)

    tm = x_ref.shape[0]
    grp = DIL_PERM_ROWS
    nat_ref = o_refs[0]
    for r0 in range(0, tm, grp):
        rs = slice(r0, r0 + grp)
        acc = jnp.dot(xb_ref[rs, :], w_ref[...], preferred_element_type=F32)
        t_z, t_partner = tab_ref[0, rs, :], tab_ref[1, rs, :]
        for hb in range(acc.shape[1] // LANES):
            z = acc[:, hb * LANES:(hb + 1) * LANES]
            nat_ref[0, rs, hb * LANES:(hb + 1) * LANES] = (
                z * t_z + pltpu.roll(z, LANES // 2, 1) * t_partner).astype(nat_ref.dtype)

    for g in range(1, n):
        d = dilations[g]
        w = grp // d

        @pl.when((j == g) | (j >= n))
        def _(g=g, d=d, w=w):
            for gi in range(tm // grp):
                y = jnp.dot(p_refs[g][...], nat_ref[0, gi * grp:(gi + 1) * grp, :],
                            preferred_element_type=F32).astype(o_refs[g].dtype)
                for r in range(d):
                    o_refs[g][r, gi * w:(gi + 1) * w, :] = y[r * w:(r + 1) * w, :]


def _dilated_proj(x2d, w, tabs, dilations, *, batch, seq_len):
    m, k = x2d.shape
    n = len(dilations)
    tm = min(PROJ_ROWS, seq_len)
    assert dilations[0] == 1 and w.shape[1] == (n + 2) * C_WIDTH
    assert m % tm == 0 and seq_len % tm == 0 and tm % DIL_PERM_ROWS == 0
    assert all(DIL_PERM_ROWS % (d * 2 * SUBLANES) == 0 for d in dilations[1:])
    bps = seq_len // tm
    perms = [_residue_perm(d) for d in dilations]

    def tab_map(i, j):
        return ((j >= n).astype(jnp.int32) + (j > n).astype(jnp.int32), 0, i % bps, 0)

    def out_map(g):
        return lambda i, j: (i // bps, 0, i % bps, (j > g).astype(jnp.int32) + (j > n).astype(jnp.int32))

    outs = pl.pallas_call(
        functools.partial(_dilated_proj_kernel, dilations=tuple(dilations)),
        grid=(m // tm, n + 2),
        in_specs=[pl.BlockSpec((tm, k), lambda i, j: (i, 0)),
                  pl.BlockSpec((k, C_WIDTH), lambda i, j: (0, j)),
                  pl.BlockSpec((None, 2, tm, LANES), tab_map)]
                 + [pl.BlockSpec(p.shape, lambda i, j: (0, 0)) for p in perms],
        out_specs=[pl.BlockSpec((None, d, tm // d, C_WIDTH), out_map(g)) for g, d in enumerate(dilations)],
        out_shape=[jax.ShapeDtypeStruct((batch, d, seq_len // d, 3 * C_WIDTH), BF16) for d in dilations],
        scratch_shapes=[pltpu.VMEM((tm, k), BF16)],
        compiler_params=_params("parallel", "arbitrary"),
        name="dilated_proj",
    )(x2d, w, tabs, *perms)
    return outs


def _dilated_kernel(q_ref, k_ref, kp_ref, v_ref, vp_ref, o_ref, lse_ref):
    w, dh = DIL_WINDOW, DIL_HEAD_DIM
    n_res, tiles = q_ref.shape[0], q_ref.shape[1] // w
    has_prev = pl.program_id(2) > 0
    row = lax.broadcasted_iota(jnp.int32, (w, 2 * w), 0)
    col = lax.broadcasted_iota(jnp.int32, (w, 2 * w), 1)
    band = (col >= row) & (col <= row + w)
    first_band = band & (has_prev | (col >= w))
    er = lax.broadcasted_iota(jnp.int32, (w, w), 0)
    ec = lax.broadcasted_iota(jnp.int32, (w, w), 1)
    eye = er == ec
    units = [(r, h, t) for r in range(n_res) for h in range(DIL_HEADS) for t in range(tiles)]

    def banded(ref, halo_ref, r, h, t):
        hs = slice(h * dh, (h + 1) * dh)
        if t == 0:
            return jnp.concatenate([halo_ref[r, :, hs], ref[r, 0:w, hs]], axis=0)
        return ref[r, (t - 1) * w:(t + 1) * w, hs]

    s = {}
    for r, h, t in units:
        q = q_ref[r, t * w:(t + 1) * w, h * dh:(h + 1) * dh]
        sc = lax.dot_general(q, banded(k_ref, kp_ref, r, h, t), _NT, preferred_element_type=F32)
        s[r, h, t] = jnp.where(first_band if t == 0 else band, sc, NEG_INF)
    m = {u: jnp.max(jnp.maximum(s[u][:, :w], s[u][:, w:]), axis=-1, keepdims=True) for u in units}
    p = {u: jnp.exp((s[u] - m[u]).astype(BF16)) for u in units}
    ones = jnp.ones((2 * w, dh), BF16)
    ol = {u: jnp.dot(p[u], jnp.concatenate([banded(v_ref, vp_ref, *u), ones], axis=1),
                     preferred_element_type=F32) for u in units}
    for r, h, t in units:
        o, l = ol[r, h, t][:, :dh], ol[r, h, t][:, dh:]
        o_ref[r, t * w:(t + 1) * w, h * dh:(h + 1) * dh] = (o / l).astype(o_ref.dtype)
        lse = m[r, h, t] + jnp.log(l)
        lse_ref[r, t, h:h + 1, :] = jnp.sum(jnp.where(eye, lse, 0.0), axis=0, keepdims=True)


def _dilated_group(src, cols, *, seq_len):
    batch, d, length, _ = src.shape
    w = DIL_WINDOW
    rows = min(DIL_ROWS, length)
    n_res = min(d, DIL_ROWS // rows)
    assert length % rows == 0 and rows % w == 0 and length * d == seq_len and d % n_res == 0
    tiles = rows // w
    qc, kc, vc = cols
    blk = lambda cb: pl.BlockSpec((None, n_res, rows, C_WIDTH), lambda b, r, i: (b, r, i, cb))
    halo = lambda cb: pl.BlockSpec(
        (None, n_res, w, C_WIDTH), lambda b, r, i: (b, r, jnp.maximum(i * tiles - 1, 0), cb))
    o, lse = pl.pallas_call(
        _dilated_kernel,
        grid=(batch, d // n_res, length // rows),
        in_specs=[blk(qc), blk(kc), halo(kc), blk(vc), halo(vc)],
        out_specs=[
            pl.BlockSpec((None, n_res, rows, C_WIDTH), lambda b, r, i: (b, r, i, 0)),
            pl.BlockSpec((None, n_res, tiles, DIL_HEADS, w), lambda b, r, i: (b, r, i, 0, 0)),
        ],
        out_shape=[
            jax.ShapeDtypeStruct((batch, d, length, C_WIDTH), BF16),
            jax.ShapeDtypeStruct((batch, d, length // w, DIL_HEADS, w), F32),
        ],
        compiler_params=_params("parallel", "parallel", "arbitrary"),
        name=f"dilated_d{d}",
    )(src, src, src, src, src)
    lse = lse.transpose(0, 2, 4, 1, 3).reshape(batch, seq_len, DIL_HEADS)
    return o, lse


def _mlp_ln(x1_ref, w1_ref, w2_ref, ln2_g, ln2_b):
    xb = x1_ref[...].astype(BF16)
    hs = []
    for c0 in range(0, w1_ref.shape[1], MLP_HIDDEN_CHUNK):
        h = jnp.dot(xb, w1_ref[:, c0:c0 + MLP_HIDDEN_CHUNK], preferred_element_type=F32)
        hs.append(jnp.square(jnp.maximum(h, 0.0)).astype(BF16))
    y = jnp.dot(jnp.concatenate(hs, axis=1), w2_ref[...], preferred_element_type=F32)
    return _layer_norm(DEEPNORM_ALPHA * x1_ref[...] + y, ln2_g, ln2_b)


def _outproj_mlp_kernel(*refs, n_act):
    acts = refs[:n_act]
    w_ref, x_ref, g1_ref, b1_ref, w1_ref, w2_ref, g2_ref, b2_ref, o_ref, x1_ref = refs[n_act:]

    @pl.when(pl.program_id(0) == 0)
    def _():
        x1_ref[...] = jnp.zeros_like(x1_ref)

    o_ref[...] = _mlp_ln(x1_ref, w1_ref, w2_ref, g2_ref[...], b2_ref[...])
    proj = None
    off = 0
    for a_ref in acts:
        kw = a_ref.shape[1]
        part = jnp.dot(a_ref[...], w_ref[off:off + kw, :], preferred_element_type=F32)
        proj = part if proj is None else proj + part
        off += kw
    x1_ref[...] = _layer_norm(DEEPNORM_ALPHA * x_ref[...] + proj, g1_ref[...], b1_ref[...])


def _outproj_mlp(acts, w, x2d, ln1, mlp, ln2):
    m, dm = x2d.shape
    tm = LAYER_ROWS
    assert m % tm == 0 and mlp[0].shape[1] % MLP_HIDDEN_CHUNK == 0
    last = m // tm - 1
    rowspec = lambda a: pl.BlockSpec((tm, a.shape[1]), lambda s: (jnp.minimum(s, last), 0))
    const = lambda a: pl.BlockSpec(a.shape, lambda s: (0, 0), pipeline_mode=pl.Buffered(1))
    return pl.pallas_call(
        functools.partial(_outproj_mlp_kernel, n_act=len(acts)),
        grid=(m // tm + 1,),
        in_specs=([rowspec(a) for a in acts] + [const(w), rowspec(x2d)]
                  + [const(a) for a in (*ln1, *mlp, *ln2)]),
        out_specs=pl.BlockSpec((tm, dm), lambda s: (jnp.maximum(s - 1, 0), 0)),
        out_shape=jax.ShapeDtypeStruct((m, dm), F32),
        scratch_shapes=[pltpu.VMEM((tm, dm), F32)],
        compiler_params=_params("arbitrary"),
        name="outproj_mlp",
    )(*acts, w, x2d, *ln1, *mlp, *ln2)


def _merge_outproj_mlp_kernel(*refs, dilations):
    n = len(dilations)
    o_refs, lse_refs, p_refs = refs[:n], refs[n:2 * n], refs[2 * n:3 * n]
    w_ref, x_ref, g1_ref, b1_ref, w1_ref, w2_ref, g2_ref, b2_ref, out_ref, x1_ref = refs[3 * n:]
    nh, dh, grp = DIL_HEADS, DIL_HEAD_DIM, DIL_PERM_ROWS
    tm = x_ref.shape[0]

    @pl.when(pl.program_id(0) == 0)
    def _():
        x1_ref[...] = jnp.zeros_like(x1_ref)

    out_ref[...] = _mlp_ln(x1_ref, w1_ref, w2_ref, g2_ref[...], b2_ref[...])
    branch = []
    for o_ref, p_ref, d in zip(o_refs, p_refs, dilations):
        if d == 1:
            branch.append(o_ref[0].astype(F32))
            continue
        w = grp // d
        nat = []
        for gi in range(tm // grp):
            perm = jnp.concatenate([o_ref[r, gi * w:(gi + 1) * w, :] for r in range(d)], axis=0)
            nat.append(lax.dot_general(p_ref[...], perm, _TN, preferred_element_type=F32))
        branch.append(jnp.concatenate(nat, axis=0))
    parts = [ref[...] for ref in lse_refs]
    m = functools.reduce(jnp.maximum, parts)
    es = [jnp.exp(p - m) for p in parts]
    den = functools.reduce(jnp.add, es)
    wts = [e / den for e in es]
    heads = []
    for h in range(nh):
        hs = slice(h * dh, (h + 1) * dh)
        heads.append(functools.reduce(jnp.add, [wts[g][:, h:h + 1] * branch[g][:, hs] for g in range(n)]))
    proj = _bdot(jnp.concatenate(heads, axis=1), w_ref[...])
    x1_ref[...] = _layer_norm(DEEPNORM_ALPHA * x_ref[...] + proj, g1_ref[...], b1_ref[...])


def _merge_outproj_mlp(outs, lses, dilations, w, x2d, ln1, mlp, ln2, *, batch, seq_len):
    dm = x2d.shape[1]
    tm = LAYER_ROWS
    assert seq_len % tm == 0 and tm % DIL_PERM_ROWS == 0 and mlp[0].shape[1] % MLP_HIDDEN_CHUNK == 0
    x3 = x2d.reshape(batch, seq_len, dm)
    perms = [_residue_perm(d) for d in dilations]
    bps = seq_len // tm
    n_tiles = batch * bps
    tile = lambda s: jnp.minimum(s, n_tiles - 1)
    done = lambda s: jnp.maximum(s - 1, 0)
    const = lambda a: pl.BlockSpec(a.shape, lambda s: (0,) * a.ndim, pipeline_mode=pl.Buffered(1))
    rows3 = lambda a: pl.BlockSpec((None, tm, a.shape[2]), lambda s: (tile(s) // bps, tile(s) % bps, 0))
    in_specs = ([pl.BlockSpec((None, d, tm // d, C_WIDTH), lambda s: (tile(s) // bps, 0, tile(s) % bps, 0))
                 for d in dilations]
                + [rows3(a) for a in lses] + [const(p) for p in perms] + [const(w), rows3(x3)]
                + [const(a) for a in (*ln1, *mlp, *ln2)])
    out = pl.pallas_call(
        functools.partial(_merge_outproj_mlp_kernel, dilations=tuple(dilations)),
        grid=(n_tiles + 1,),
        in_specs=in_specs,
        out_specs=pl.BlockSpec((None, tm, dm), lambda s: (done(s) // bps, done(s) % bps, 0)),
        out_shape=jax.ShapeDtypeStruct((batch, seq_len, dm), F32),
        scratch_shapes=[pltpu.VMEM((tm, dm), F32)],
        compiler_params=_params("arbitrary"),
        name="merge_outproj_mlp",
    )(*outs, *lses, *perms, w, x3, *ln1, *mlp, *ln2)
    return out.reshape(batch * seq_len, dm)


def _rwkv_moba_layer(x2d, w_in, shift_mix, w0, w_up, a0, a_up, g_up, k_k, k_a, r_k, lnx_g, lnx_b,
                     w_out, ln1, mlp, ln2, *, batch, seq_len):
    d = RWKV_DIM
    tabs = _rotary_tables(seq_len, MOBA_HEAD_DIM ** -0.5)
    w_m = w_in[:, RWKV_PROJ_COLS:].astype(BF16)
    w_m = jnp.concatenate([_rotary_weight(w_m[:, :2 * MOBA_WIDTH], 2 * MOBA_HEADS),
                           w_m[:, 2 * MOBA_WIDTH:]], axis=1)
    kinds = [0] * MOBA_HEADS + [1] * MOBA_HEADS + [None] * MOBA_HEADS
    qkv = _rotary_proj(x2d, w_m, tabs, kinds, seq_len=seq_len)
    y_b = _moba(qkv, batch=batch, seq_len=seq_len)
    pad = RWKV_Z_COLS - RWKV_PROJ_COLS
    w_r = jnp.pad(w_in[:, :RWKV_PROJ_COLS], ((0, 0), (0, pad))).astype(BF16)
    mix = jnp.pad(shift_mix, (0, pad))[None, :]
    wl = jnp.zeros((LORA_PAD, 3 * d), F32)
    wl = wl.at[0:DECAY_LORA, 0:d].set(w_up)
    wl = wl.at[DECAY_LORA:DECAY_LORA + ICLR_LORA, d:2 * d].set(a_up)
    wl = wl.at[DECAY_LORA + ICLR_LORA:LORA_COLS, 2 * d:3 * d].set(g_up).astype(BF16)
    par = jnp.stack([w0, a0, k_k, k_a, r_k.reshape(d), lnx_g, lnx_b, jnp.zeros((d,), F32)])
    y_a = _rwkv_scan(x2d, w_r, mix, wl, par, batch=batch, seq_len=seq_len)
    return _outproj_mlp([y_a, y_b], w_out.astype(BF16), x2d, ln1, mlp, ln2)


def _dilated_layer(x2d, w_in, w_out, ln1, mlp, ln2, *, batch, seq_len):
    tabs = _rotary_tables(seq_len, DIL_HEAD_DIM ** -0.5)
    nqk = (DIL_GROUPS + 1) * C_WIDTH
    w_c = w_in.astype(BF16)
    w_c = jnp.concatenate([_rotary_weight(w_c[:, :nqk], (DIL_GROUPS + 1) * DIL_HEADS), w_c[:, nqk:]], axis=1)
    dilations = [dilation for _, dilation in DIL_PAIRS]
    assert all(span // dilation == DIL_WINDOW for span, dilation in DIL_PAIRS)
    srcs = _dilated_proj(x2d, w_c, tabs, dilations, batch=batch, seq_len=seq_len)
    outs, lses = [], []
    for src in srcs:
        o, lse = _dilated_group(src, (0, 1, 2), seq_len=seq_len)
        outs.append(o)
        lses.append(lse)
    return _merge_outproj_mlp(outs, lses, dilations, w_out.astype(BF16), x2d, ln1, mlp, ln2,
                              batch=batch, seq_len=seq_len)


def kernel(x, ab_w_in, ab_shift_mix, ab_w0, ab_w_up, ab_a0, ab_a_up, ab_g_up, ab_k_k, ab_k_a, ab_r_k,
           ab_lnx_g, ab_lnx_b, ab_w_out, c_w_in, c_w_out, ln1_g, ln1_b, mlp_w1, mlp_w2, ln2_g, ln2_b):
    batch, seq_len, dm = x.shape
    assert dm == D_MODEL
    h = x.reshape(batch * seq_len, dm)
    depth = ln1_g.shape[0]
    for layer in range(depth):
        j = layer // 2
        ln1 = (ln1_g[layer][None, :], ln1_b[layer][None, :])
        ln2 = (ln2_g[layer][None, :], ln2_b[layer][None, :])
        mlp = (mlp_w1[layer].astype(BF16), mlp_w2[layer].astype(BF16))
        if layer % 2 == 0:
            h = _rwkv_moba_layer(h, ab_w_in[j], ab_shift_mix[j], ab_w0[j], ab_w_up[j], ab_a0[j], ab_a_up[j],
                                 ab_g_up[j], ab_k_k[j], ab_k_a[j], ab_r_k[j], ab_lnx_g[j], ab_lnx_b[j],
                                 ab_w_out[j], ln1, mlp, ln2, batch=batch, seq_len=seq_len)
        else:
            h = _dilated_layer(h, c_w_in[j], c_w_out[j], ln1, mlp, ln2, batch=batch, seq_len=seq_len)
    return h.reshape(batch, seq_len, dm)
```

```python
import functools

import jax
import jax.numpy as jnp
from jax import lax
from jax.experimental import pallas as pl
from jax.experimental.pallas import tpu as pltpu

F32 = jnp.float32
BF16 = jnp.bfloat16

D_MODEL = 1024
DEPTH = 2
DEEPNORM_ALPHA = (2 * DEPTH) ** 0.25
LN_EPS = 1e-5
ROPE_THETA = 500000.0
ROPE_FRACTION = 4
RWKV_HEAD_DIM = 64
RWKV_DIM = D_MODEL // 2
RWKV_HEADS = RWKV_DIM // RWKV_HEAD_DIM
DECAY_LORA = 64
ICLR_LORA = 64
GATE_LORA = 160
RWKV_GN_EPS = 64e-5
RWKV_PROJ_COLS = 3 * RWKV_DIM + DECAY_LORA + ICLR_LORA + GATE_LORA
MOBA_HEAD_DIM = 128
MOBA_WIDTH = D_MODEL - RWKV_DIM
MOBA_HEADS = MOBA_WIDTH // MOBA_HEAD_DIM
MOBA_BLOCK = 256
MOBA_TOPK = 3
DIL_PAIRS = ((128, 1), (512, 4), (2048, 16))
DIL_GROUPS = len(DIL_PAIRS)
DIL_HEAD_DIM = 128
DIL_HEADS = D_MODEL // DIL_HEAD_DIM
C_WIDTH = DIL_HEADS * DIL_HEAD_DIM
C_IN_COLS = DIL_GROUPS * C_WIDTH + 2 * C_WIDTH
MLP_HIDDEN = 4 * D_MODEL
NEG_INF = -1e30

LANES = 128
SUBLANES = 8
VMEM_LIMIT_BYTES = 56 * 1024 * 1024

LORA_COLS = DECAY_LORA + ICLR_LORA + GATE_LORA
LORA_PAD = -(-LORA_COLS // LANES) * LANES
RWKV_Z_COLS = 3 * RWKV_DIM + LORA_PAD
RWKV_CHUNK = 64
RWKV_SCAN_CHUNKS = 8
DIL_WINDOW = 128
DIL_ROWS = 512
DIL_PERM_ROWS = 256
PROJ_ROWS = 1024
PROJ_ROW_CHUNK = 256
LAYER_ROWS = 512
MLP_HIDDEN_CHUNK = 1024

_NN = (((1,), (0,)), ((), ()))
_NT = (((1,), (1,)), ((), ()))
_TN = (((0,), (0,)), ((), ()))


def _bdot(a, b, dims=_NN):
    return lax.dot_general(a.astype(BF16), b.astype(BF16), dims, preferred_element_type=F32)


def _layer_norm(y, g, b):
    mu = jnp.mean(y, axis=-1, keepdims=True)
    var = jnp.mean(jnp.square(y - mu), axis=-1, keepdims=True)
    return (y - mu) * lax.rsqrt(var + LN_EPS) * g + b


def _params(*sem):
    return pltpu.CompilerParams(dimension_semantics=sem, vmem_limit_bytes=VMEM_LIMIT_BYTES)


ROT_DIMS = MOBA_HEAD_DIM // ROPE_FRACTION
ROT_HALF = ROT_DIMS // 2


def _rotary_weight(w, n_heads):
    split = ROT_DIMS + LANES // 2 - ROT_HALF
    w3 = w.reshape(w.shape[0], n_heads, LANES)
    w3 = jnp.concatenate([w3[..., :ROT_HALF], w3[..., ROT_DIMS:split], w3[..., ROT_HALF:ROT_DIMS],
                          w3[..., split:]], axis=-1)
    return w3.reshape(w.shape[0], n_heads * LANES)


def _rotary_tables(seq_len, q_scale):
    inv_freq = ROPE_THETA ** (-jnp.arange(ROT_HALF, dtype=F32) / ROT_HALF)
    ang = jnp.arange(seq_len).astype(F32)[:, None] * inv_freq[None, :]
    cos, sin = jnp.cos(ang), jnp.sin(ang)
    pad = lambda a, fill: jnp.concatenate([a, jnp.full((seq_len, LANES // 2 - ROT_HALF), fill, F32)], axis=1)
    c = jnp.concatenate([pad(cos, 1.0), pad(cos, 1.0)], axis=1)
    s = jnp.concatenate([pad(-sin, 0.0), pad(sin, 0.0)], axis=1)
    one = jnp.ones_like(c)
    return jnp.stack([jnp.stack([c * q_scale, s * q_scale]), jnp.stack([c, s]),
                      jnp.stack([one, jnp.zeros_like(c)])])


def _rotary_proj_kernel(x_ref, w_ref, tab_ref, o_ref, *, head_kinds):
    tm = x_ref.shape[0]
    chunk = min(PROJ_ROW_CHUNK, tm)
    for r0 in range(0, tm, chunk):
        rs = slice(r0, r0 + chunk)
        acc = jnp.dot(x_ref[rs, :].astype(BF16), w_ref[...], preferred_element_type=F32)
        for hb, kind in enumerate(head_kinds):
            z = acc[:, hb * LANES:(hb + 1) * LANES]
            if kind is not None:
                z = z * tab_ref[kind, 0, rs, :] + pltpu.roll(z, LANES // 2, 1) * tab_ref[kind, 1, rs, :]
            o_ref[rs, hb * LANES:(hb + 1) * LANES] = z.astype(o_ref.dtype)


def _rotary_proj(x2d, w, tabs, head_kinds, *, seq_len):
    m, k = x2d.shape
    n = w.shape[1]
    tm = min(PROJ_ROWS, seq_len)
    assert m % tm == 0 and seq_len % tm == 0 and n == len(head_kinds) * LANES
    bps = seq_len // tm
    return pl.pallas_call(
        functools.partial(_rotary_proj_kernel, head_kinds=tuple(head_kinds)),
        grid=(m // tm,),
        in_specs=[pl.BlockSpec((tm, k), lambda i: (i, 0)),
                  pl.BlockSpec((k, n), lambda i: (0, 0), pipeline_mode=pl.Buffered(1)),
                  pl.BlockSpec((tabs.shape[0], 2, tm, LANES), lambda i: (0, 0, i % bps, 0))],
        out_specs=pl.BlockSpec((tm, n), lambda i: (i, 0)),
        out_shape=jax.ShapeDtypeStruct((m, n), BF16),
        compiler_params=_params("parallel"),
        name="rotary_proj",
    )(x2d, w, tabs)


def _softplus(x):
    return jnp.maximum(x, 0.0) + jnp.log(1.0 + jnp.exp(-jnp.abs(x)))


def _split_dot(x, ones_bd):
    hi = x.astype(BF16)
    lo = (x - hi.astype(F32)).astype(BF16)
    return (jnp.dot(hi, ones_bd, preferred_element_type=F32)
            + jnp.dot(lo, ones_bd, preferred_element_type=F32))


def _rwkv_prep(z, prev, mix, wl, par, ones_bd):
    d = RWKV_DIM
    pw = 2 * RWKV_HEAD_DIM
    row = lax.broadcasted_iota(jnp.int32, z.shape, 0)
    shifted = jnp.where(row == 0, prev, pltpu.roll(z, 1, 0))
    zz = z + (shifted - z) * mix
    r, k, v, lo = zz[:, 0:d], zz[:, d:2 * d], zz[:, 2 * d:3 * d], zz[:, 3 * d:]
    lane = lax.broadcasted_iota(jnp.int32, lo.shape, 1)
    act = jnp.where(lane < DECAY_LORA, jnp.tanh(lo),
                    jnp.where(lane < DECAY_LORA + ICLR_LORA, lo, jax.nn.sigmoid(lo)))
    lora = _bdot(act, wl)
    w0, a0, k_k, k_a = par[0:1, :], par[1:2, :], par[2:3, :], par[3:4, :]
    w = -_softplus(-(w0 + lora[:, 0:d])) - 0.5
    a = jax.nn.sigmoid(a0 + lora[:, d:2 * d])
    kk = k * k_k
    sq = kk * kk
    ssq = jnp.concatenate([_split_dot(sq[:, p0:p0 + pw], ones_bd) for p0 in range(0, d, pw)], axis=1)
    kk = kk / jnp.maximum(jnp.sqrt(ssq), 1e-12)
    return r, -jnp.exp(w), k * (1.0 + (a - 1.0) * k_a), v, -kk, kk * a, lora[:, 2 * d:3 * d]


def _rwkv_chunk_kernel(x_ref, wr_ref, mix_ref, wl_ref, par_ref, o_ref, s_ref, prev_ref, *, chunks):
    c, n = RWKV_CHUNK, RWKV_HEAD_DIM
    pw = 2 * n
    assert c == n and pw == LANES
    pairs = RWKV_HEADS // 2
    units = [(ch, p) for ch in range(chunks) for p in range(pairs)]

    first = pl.program_id(1) == 0

    @pl.when(first)
    def _():
        s_ref[...] = jnp.zeros_like(s_ref)

    row = lax.broadcasted_iota(jnp.int32, (c, pw), 0)
    col = lax.broadcasted_iota(jnp.int32, (c, pw), 1) & (n - 1)
    tril_inc = col <= row
    tril_str = col < row
    eye = col == row
    level_masks = [((row >> (lvl + 1)) == (col >> (lvl + 1))) & (((row >> lvl) & 1) == 1) & (((col >> lvl) & 1) == 0)
                   for lvl in range(c.bit_length() - 1)]
    r2 = lax.broadcasted_iota(jnp.int32, (pw, pw), 0)
    c2 = lax.broadcasted_iota(jnp.int32, (pw, pw), 1)
    same_head = (r2 < n) == (c2 < n)
    eye2 = r2 == c2
    ones_bd = jnp.where(same_head, 1.0, 0.0).astype(BF16)

    def stack(x):
        xb = x.astype(BF16)
        return jnp.where(same_head, jnp.concatenate([xb, xb], axis=0), jnp.zeros((), BF16))

    rows = chunks * c
    z = jnp.dot(x_ref[...].astype(BF16), wr_ref[...], preferred_element_type=F32)
    prev = jnp.where(first, 0.0, prev_ref[SUBLANES - 1:SUBLANES, :])
    r_all, lw, kp_all, v_all, na_all, b_all, g_all = _rwkv_prep(
        z, prev, mix_ref[...], wl_ref[...], par_ref[...], ones_bd)
    prev_ref[...] = z[rows - SUBLANES:, :]

    tr = lax.broadcasted_iota(jnp.int32, (rows, rows), 0)
    tc = lax.broadcasted_iota(jnp.int32, (rows, rows), 1)
    chunk_tril = ((tr >> (c.bit_length() - 1)) == (tc >> (c.bit_length() - 1))) & (tc <= tr)
    chunk_tril = jnp.where(chunk_tril, 1.0, 0.0).astype(BF16)
    lw_hi = lw.astype(BF16)
    lw_r1 = lw - lw_hi.astype(F32)
    lw_mid = lw_r1.astype(BF16)
    lw_lo = (lw_r1 - lw_mid.astype(F32)).astype(BF16)
    cum = (jnp.dot(chunk_tril, lw_hi, preferred_element_type=F32)
           + jnp.dot(chunk_tril, lw_mid, preferred_element_type=F32)
           + jnp.dot(chunk_tril, lw_lo, preferred_element_type=F32))
    r_k, lnx_g, lnx_b = par_ref[4:5, :], par_ref[5:6, :], par_ref[6:7, :]

    at, rt, bt, kt, bh, kh, vv, p_end = {}, {}, {}, {}, {}, {}, {}, {}
    for ch in range(chunks):
        rs = slice(ch * c, (ch + 1) * c)
        cum_c = cum[rs]
        cum_end = cum_c[c - 1:c, :]
        e_inv = jnp.exp(-cum_c)
        e_end = jnp.exp(cum_end - cum_c)
        at_c = na_all[rs] * jnp.exp(cum_c - lw[rs])
        rt_c = r_all[rs] * jnp.exp(cum_c)
        b_c, kp_c = b_all[rs], kp_all[rs]
        bt_c, kt_c, bh_c, kh_c = b_c * e_inv, kp_c * e_inv, b_c * e_end, kp_c * e_end
        pe_c = jnp.exp(cum_end)
        for p in range(pairs):
            ls = slice(p * pw, (p + 1) * pw)
            u = (ch, p)
            at[u], rt[u], bt[u], kt[u], bh[u], kh[u] = at_c[:, ls], rt_c[:, ls], bt_c[:, ls], kt_c[:, ls], bh_c[:, ls], kh_c[:, ls]
            vv[u], p_end[u] = v_all[rs, ls], pe_c[:, ls]

    ar = {u: jnp.concatenate([at[u], rt[u]], axis=0) for u in units}
    g = {u: _bdot(ar[u], jnp.concatenate([stack(bt[u]), stack(kt[u])], axis=0), _NT)
         for u in units}
    a_ab = {u: jnp.where(tril_str, g[u][:c, :pw], 0.0) for u in units}
    a_rb = {u: jnp.where(tril_inc, g[u][c:, :pw], 0.0) for u in units}
    a_ak = {u: jnp.where(tril_str, g[u][:c, pw:], 0.0) for u in units}
    a_rk = {u: jnp.where(tril_inc, g[u][c:, pw:], 0.0) for u in units}
    av = {u: _bdot(jnp.concatenate([a_ak[u], a_rk[u]], axis=0), stack(vv[u])) for u in units}
    ak_v = {u: av[u][:c] for u in units}
    inv = {u: jnp.where(eye, 1.0, jnp.where(level_masks[0], a_ab[u], 0.0)) for u in units}
    for mask in level_masks[1:]:
        e = {u: _bdot(jnp.where(mask, a_ab[u], 0.0), stack(inv[u])) for u in units}
        inv = {u: inv[u] + _bdot(inv[u], stack(e[u])) for u in units}
    wu = {u: _bdot(inv[u], jnp.concatenate([stack(at[u]), stack(ak_v[u])], axis=1)) for u in units}
    w_mat = {u: wu[u][:, :pw] for u in units}
    u_loc = {u: wu[u][:, pw:] for u in units}
    wub = {u: _bdot(wu[u], bh[u], _TN) for u in units}
    m_c = {u: jnp.where(same_head, wub[u][:pw], 0.0) + jnp.where(eye2, p_end[u], 0.0) for u in units}
    n_c = {u: jnp.where(same_head, wub[u][pw:] + _bdot(vv[u], kh[u], _TN), 0.0) for u in units}
    qy = {u: _bdot(a_rb[u], jnp.concatenate([stack(w_mat[u]), stack(u_loc[u])], axis=1)) for u in units}
    q_c = {u: rt[u] + qy[u][:, :pw] for u in units}
    y_loc = {u: qy[u][:, pw:] + av[u][c:] for u in units}

    state = [s_ref[p] for p in range(pairs)]
    y = {}
    for ch in range(chunks):
        for p in range(pairs):
            u = (ch, p)
            y[u] = _bdot(q_c[u], state[p], _NT) + y_loc[u]
            state[p] = _bdot(state[p], m_c[u]) + n_c[u]
    for p in range(pairs):
        s_ref[p] = state[p]

    inv_n = 1.0 / n
    mu = {u: _bdot(y[u], ones_bd) * inv_n for u in units}
    var = {u: _bdot(jnp.square(y[u] - mu[u]), ones_bd) * inv_n for u in units}
    for (ch, p) in units:
        u = (ch, p)
        rs, ls = slice(ch * c, (ch + 1) * c), slice(p * pw, (p + 1) * pw)
        yn = (y[u] - mu[u]) * lax.rsqrt(var[u] + RWKV_GN_EPS) * lnx_g[:, ls] + lnx_b[:, ls]
        bonus = _bdot(r_all[rs, ls] * kp_all[rs, ls] * r_k[:, ls], ones_bd) * vv[u]
        o_ref[rs, ls] = ((yn + bonus) * g_all[rs, ls]).astype(o_ref.dtype)


def _rwkv_scan(x2d, w_r, mix, wl, par, *, batch, seq_len):
    c = RWKV_CHUNK
    chunks = RWKV_SCAN_CHUNKS
    rows = chunks * c
    dm, zc = w_r.shape
    assert seq_len % rows == 0
    const = lambda a: pl.BlockSpec(a.shape, lambda bi, ci: (0, 0), pipeline_mode=pl.Buffered(1))
    out = pl.pallas_call(
        functools.partial(_rwkv_chunk_kernel, chunks=chunks),
        grid=(batch, seq_len // rows),
        in_specs=[pl.BlockSpec((None, rows, dm), lambda bi, ci: (bi, ci, 0)),
                  const(w_r), const(mix), const(wl), const(par)],
        out_specs=pl.BlockSpec((None, rows, RWKV_DIM), lambda bi, ci: (bi, ci, 0)),
        out_shape=jax.ShapeDtypeStruct((batch, seq_len, RWKV_DIM), BF16),
        scratch_shapes=[pltpu.VMEM((RWKV_HEADS // 2, 2 * RWKV_HEAD_DIM, 2 * RWKV_HEAD_DIM), F32),
                        pltpu.VMEM((SUBLANES, zc), F32)],
        compiler_params=_params("parallel", "arbitrary"),
        name="rwkv_scan",
    )(x2d.reshape(batch, seq_len, dm), w_r, mix, wl, par)
    return out.reshape(batch * seq_len, RWKV_DIM)


def _moba_kernel(q_ref, k_ref, v_ref, pool_ref, o_ref, kext_ref, vext_ref, s_ref):
    blk, dh = MOBA_BLOCK, MOBA_HEAD_DIM
    t_len = k_ref.shape[0]
    nb = t_len // blk
    nbp = -(-nb // SUBLANES) * SUBLANES
    shift = blk.bit_length() - 1
    key_tile = 2 * blk

    lane = lax.broadcasted_iota(jnp.int32, (blk, LANES), 1)
    for j in range(nb):
        rs = slice(j * blk, (j + 1) * blk)
        kext_ref[rs, 0:dh] = k_ref[rs, :]
        kext_ref[rs, dh:dh + LANES] = jnp.where(lane == j, 1.0, 0.0).astype(BF16)
        vext_ref[rs, 0:dh] = v_ref[rs, :]
        vext_ref[rs, dh:dh + LANES] = jnp.ones((blk, LANES), BF16)

    km = jnp.dot(pool_ref[...], k_ref[...], preferred_element_type=F32)[0:nbp]
    km_hi = km.astype(BF16)
    km_lo = (km - km_hi.astype(F32)).astype(BF16)
    q_all = q_ref[...]
    gate = (lax.dot_general(km_hi, q_all, _NT, preferred_element_type=F32)
            + lax.dot_general(km_lo, q_all, _NT, preferred_element_type=F32))
    bidx = lax.broadcasted_iota(jnp.int32, gate.shape, 0)
    qblk = lax.broadcasted_iota(jnp.int32, gate.shape, 1) >> shift
    past = bidx < qblk
    gate = jnp.where(past, gate, NEG_INF)
    rank = jnp.zeros(gate.shape, jnp.int32)
    for jp in range(nb):
        other = gate[jp:jp + 1, :]
        rank = rank + ((other > gate) | ((other == gate) & (jp < bidx))).astype(jnp.int32)
    allowed = (past & (rank < MOBA_TOPK)) | (bidx == qblk)
    bias_t = jnp.where(allowed, 0.0, NEG_INF).astype(BF16)
    bias_t = jnp.concatenate([bias_t, jnp.zeros((LANES - nbp, t_len), BF16)], axis=0)
    r2 = lax.broadcasted_iota(jnp.int32, (LANES, LANES), 0)
    c2 = lax.broadcasted_iota(jnp.int32, (LANES, LANES), 1)
    eye = jnp.where(r2 == c2, 1.0, 0.0).astype(BF16)
    row = lax.broadcasted_iota(jnp.int32, (blk, blk), 0)
    col = lax.broadcasted_iota(jnp.int32, (blk, blk), 1)
    causal = col <= row
    colmax = lambda x: functools.reduce(jnp.maximum, [x[:, a:a + LANES] for a in range(0, x.shape[1], LANES)])

    for i in range(nb):
        qs = slice(i * blk, (i + 1) * blk)
        bias = lax.dot_general(bias_t[:, qs], eye, _TN, preferred_element_type=F32)
        q_ext = jnp.concatenate([q_ref[qs, :], bias.astype(BF16)], axis=1)
        nk = (i + 1) * blk
        tiles = [(a, min(a + key_tile, nk)) for a in range(0, nk, key_tile)]
        m_lane = None
        for a, e in tiles:
            s = lax.dot_general(q_ext, kext_ref[a:e, :], _NT, preferred_element_type=F32)
            if e == nk:
                s_own = jnp.where(causal, s[:, e - a - blk:], NEG_INF)
                s = s_own if e - a == blk else jnp.concatenate([s[:, :e - a - blk], s_own], axis=1)
            s_ref[:, a:e] = s
            part = colmax(s)
            m_lane = part if m_lane is None else jnp.maximum(m_lane, part)
        m = jnp.max(m_lane, axis=-1, keepdims=True)
        acc = None
        for a, e in tiles:
            p = jnp.exp((s_ref[:, a:e] - m).astype(BF16))
            pv = jnp.dot(p, vext_ref[a:e, :], preferred_element_type=F32)
            acc = pv if acc is None else acc + pv
        o_ref[qs, :] = (acc[:, :dh] / acc[:, dh:]).astype(o_ref.dtype)


def _moba(qkv, *, batch, seq_len):
    blk, dh, nh = MOBA_BLOCK, MOBA_HEAD_DIM, MOBA_HEADS
    assert seq_len % blk == 0 and seq_len // blk <= LANES and dh == LANES
    pool = (jnp.arange(LANES)[:, None] == (jnp.arange(seq_len) // blk)[None, :]).astype(BF16) / blk
    qkv3 = qkv.reshape(batch, seq_len, 3 * MOBA_WIDTH)
    head = lambda off: pl.BlockSpec((None, seq_len, dh), lambda b, h: (b, 0, off + h))
    out = pl.pallas_call(
        _moba_kernel,
        grid=(batch, nh),
        in_specs=[head(0), head(nh), head(2 * nh), pl.BlockSpec((LANES, seq_len), lambda b, h: (0, 0))],
        out_specs=head(0),
        out_shape=jax.ShapeDtypeStruct((batch, seq_len, MOBA_WIDTH), BF16),
        scratch_shapes=[pltpu.VMEM((seq_len, dh + LANES), BF16), pltpu.VMEM((seq_len, dh + LANES), BF16),
                        pltpu.VMEM((blk, seq_len), F32)],
        compiler_params=_params("parallel", "parallel"),
        name="moba",
    )(qkv3, qkv3, qkv3, pool)
    return out.reshape(batch * seq_len, MOBA_WIDTH)


def _residue_perm(d):
    p = DIL_PERM_ROWS
    idx = jnp.arange(p)
    nat = (idx % (p // d)) * d + idx // (p // d)
    return (nat[:, None] == idx[None, :]).astype(BF16)


def _dilated_proj_kernel(x_ref, w_ref, tab_ref, *rest, dilations):
    n = len(dilations)
    p_refs, o_refs, xb_ref = rest[:n], rest[n:2 * n], rest[2 * n]
    j = pl.program_id(1)

    @pl.when(j == 0)
    def _():
        xb_ref[...] = x_ref[...].astype(BF16)

    tm = x_ref.shape[0]
    grp = DIL_PERM_ROWS
    nat_ref = o_refs[0]
    for r0 in range(0, tm, grp):
        rs = slice(r0, r0 + grp)
        acc = jnp.dot(xb_ref[rs, :], w_ref[...], preferred_element_type=F32)
        t_z, t_partner = tab_ref[0, rs, :], tab_ref[1, rs, :]
        for hb in range(acc.shape[1] // LANES):
            z = acc[:, hb * LANES:(hb + 1) * LANES]
            nat_ref[0, rs, hb * LANES:(hb + 1) * LANES] = (
                z * t_z + pltpu.roll(z, LANES // 2, 1) * t_partner).astype(nat_ref.dtype)

    for g in range(1, n):
        d = dilations[g]
        w = grp // d

        @pl.when((j == g) | (j >= n))
        def _(g=g, d=d, w=w):
            for gi in range(tm // grp):
                y = jnp.dot(p_refs[g][...], nat_ref[0, gi * grp:(gi + 1) * grp, :],
                            preferred_element_type=F32).astype(o_refs[g].dtype)
                for r in range(d):
                    o_refs[g][r, gi * w:(gi + 1) * w, :] = y[r * w:(r + 1) * w, :]


def _dilated_proj(x2d, w, tabs, dilations, *, batch, seq_len):
    m, k = x2d.shape
    n = len(dilations)
    tm = min(PROJ_ROWS, seq_len)
    assert dilations[0] == 1 and w.shape[1] == (n + 2) * C_WIDTH
    assert m % tm == 0 and seq_len % tm == 0 and tm % DIL_PERM_ROWS == 0
    assert all(DIL_PERM_ROWS % (d * 2 * SUBLANES) == 0 for d in dilations[1:])
    bps = seq_len // tm
    perms = [_residue_perm(d) for d in dilations]

    def tab_map(i, j):
        return ((j >= n).astype(jnp.int32) + (j > n).astype(jnp.int32), 0, i % bps, 0)

    def out_map(g):
        return lambda i, j: (i // bps, 0, i % bps, (j > g).astype(jnp.int32) + (j > n).astype(jnp.int32))

    outs = pl.pallas_call(
        functools.partial(_dilated_proj_kernel, dilations=tuple(dilations)),
        grid=(m // tm, n + 2),
        in_specs=[pl.BlockSpec((tm, k), lambda i, j: (i, 0)),
                  pl.BlockSpec((k, C_WIDTH), lambda i, j: (0, j)),
                  pl.BlockSpec((None, 2, tm, LANES), tab_map)]
                 + [pl.BlockSpec(p.shape, lambda i, j: (0, 0)) for p in perms],
        out_specs=[pl.BlockSpec((None, d, tm // d, C_WIDTH), out_map(g)) for g, d in enumerate(dilations)],
        out_shape=[jax.ShapeDtypeStruct((batch, d, seq_len // d, 3 * C_WIDTH), BF16) for d in dilations],
        scratch_shapes=[pltpu.VMEM((tm, k), BF16)],
        compiler_params=_params("parallel", "arbitrary"),
        name="dilated_proj",
    )(x2d, w, tabs, *perms)
    return outs


def _dilated_kernel(q_ref, k_ref, kp_ref, v_ref, vp_ref, o_ref, lse_ref):
    w, dh = DIL_WINDOW, DIL_HEAD_DIM
    n_res, tiles = q_ref.shape[0], q_ref.shape[1] // w
    has_prev = pl.program_id(2) > 0
    row = lax.broadcasted_iota(jnp.int32, (w, 2 * w), 0)
    col = lax.broadcasted_iota(jnp.int32, (w, 2 * w), 1)
    band = (col >= row) & (col <= row + w)
    first_band = band & (has_prev | (col >= w))
    er = lax.broadcasted_iota(jnp.int32, (w, w), 0)
    ec = lax.broadcasted_iota(jnp.int32, (w, w), 1)
    eye = er == ec
    units = [(r, h, t) for r in range(n_res) for h in range(DIL_HEADS) for t in range(tiles)]

    def banded(ref, halo_ref, r, h, t):
        hs = slice(h * dh, (h + 1) * dh)
        if t == 0:
            return jnp.concatenate([halo_ref[r, :, hs], ref[r, 0:w, hs]], axis=0)
        return ref[r, (t - 1) * w:(t + 1) * w, hs]

    s = {}
    for r, h, t in units:
        q = q_ref[r, t * w:(t + 1) * w, h * dh:(h + 1) * dh]
        sc = lax.dot_general(q, banded(k_ref, kp_ref, r, h, t), _NT, preferred_element_type=F32)
        s[r, h, t] = jnp.where(first_band if t == 0 else band, sc, NEG_INF)
    m = {u: jnp.max(jnp.maximum(s[u][:, :w], s[u][:, w:]), axis=-1, keepdims=True) for u in units}
    p = {u: jnp.exp((s[u] - m[u]).astype(BF16)) for u in units}
    ones = jnp.ones((2 * w, dh), BF16)
    ol = {u: jnp.dot(p[u], jnp.concatenate([banded(v_ref, vp_ref, *u), ones], axis=1),
                     preferred_element_type=F32) for u in units}
    for r, h, t in units:
        o, l = ol[r, h, t][:, :dh], ol[r, h, t][:, dh:]
        o_ref[r, t * w:(t + 1) * w, h * dh:(h + 1) * dh] = (o / l).astype(o_ref.dtype)
        lse = m[r, h, t] + jnp.log(l)
        lse_ref[r, t, h:h + 1, :] = jnp.sum(jnp.where(eye, lse, 0.0), axis=0, keepdims=True)


def _dilated_group(src, cols, *, seq_len):
    batch, d, length, _ = src.shape
    w = DIL_WINDOW
    rows = min(DIL_ROWS, length)
    n_res = min(d, DIL_ROWS // rows)
    assert length % rows == 0 and rows % w == 0 and length * d == seq_len and d % n_res == 0
    tiles = rows // w
    qc, kc, vc = cols
    blk = lambda cb: pl.BlockSpec((None, n_res, rows, C_WIDTH), lambda b, r, i: (b, r, i, cb))
    halo = lambda cb: pl.BlockSpec(
        (None, n_res, w, C_WIDTH), lambda b, r, i: (b, r, jnp.maximum(i * tiles - 1, 0), cb))
    o, lse = pl.pallas_call(
        _dilated_kernel,
        grid=(batch, d // n_res, length // rows),
        in_specs=[blk(qc), blk(kc), halo(kc), blk(vc), halo(vc)],
        out_specs=[
            pl.BlockSpec((None, n_res, rows, C_WIDTH), lambda b, r, i: (b, r, i, 0)),
            pl.BlockSpec((None, n_res, tiles, DIL_HEADS, w), lambda b, r, i: (b, r, i, 0, 0)),
        ],
        out_shape=[
            jax.ShapeDtypeStruct((batch, d, length, C_WIDTH), BF16),
            jax.ShapeDtypeStruct((batch, d, length // w, DIL_HEADS, w), F32),
        ],
        compiler_params=_params("parallel", "parallel", "arbitrary"),
        name=f"dilated_d{d}",
    )(src, src, src, src, src)
    lse = lse.transpose(0, 2, 4, 1, 3).reshape(batch, seq_len, DIL_HEADS)
    return o, lse


def _mlp_ln(x1_ref, w1_ref, w2_ref, ln2_g, ln2_b):
    xb = x1_ref[...].astype(BF16)
    hs = []
    for c0 in range(0, w1_ref.shape[1], MLP_HIDDEN_CHUNK):
        h = jnp.dot(xb, w1_ref[:, c0:c0 + MLP_HIDDEN_CHUNK], preferred_element_type=F32)
        hs.append(jnp.square(jnp.maximum(h, 0.0)).astype(BF16))
    y = jnp.dot(jnp.concatenate(hs, axis=1), w2_ref[...], preferred_element_type=F32)
    return _layer_norm(DEEPNORM_ALPHA * x1_ref[...] + y, ln2_g, ln2_b)


def _outproj_mlp_kernel(*refs, n_act):
    acts = refs[:n_act]
    w_ref, x_ref, g1_ref, b1_ref, w1_ref, w2_ref, g2_ref, b2_ref, o_ref, x1_ref = refs[n_act:]

    @pl.when(pl.program_id(0) == 0)
    def _():
        x1_ref[...] = jnp.zeros_like(x1_ref)

    o_ref[...] = _mlp_ln(x1_ref, w1_ref, w2_ref, g2_ref[...], b2_ref[...])
    proj = None
    off = 0
    for a_ref in acts:
        kw = a_ref.shape[1]
        part = jnp.dot(a_ref[...], w_ref[off:off + kw, :], preferred_element_type=F32)
        proj = part if proj is None else proj + part
        off += kw
    x1_ref[...] = _layer_norm(DEEPNORM_ALPHA * x_ref[...] + proj, g1_ref[...], b1_ref[...])


def _outproj_mlp(acts, w, x2d, ln1, mlp, ln2):
    m, dm = x2d.shape
    tm = LAYER_ROWS
    assert m % tm == 0 and mlp[0].shape[1] % MLP_HIDDEN_CHUNK == 0
    last = m // tm - 1
    rowspec = lambda a: pl.BlockSpec((tm, a.shape[1]), lambda s: (jnp.minimum(s, last), 0))
    const = lambda a: pl.BlockSpec(a.shape, lambda s: (0, 0), pipeline_mode=pl.Buffered(1))
    return pl.pallas_call(
        functools.partial(_outproj_mlp_kernel, n_act=len(acts)),
        grid=(m // tm + 1,),
        in_specs=([rowspec(a) for a in acts] + [const(w), rowspec(x2d)]
                  + [const(a) for a in (*ln1, *mlp, *ln2)]),
        out_specs=pl.BlockSpec((tm, dm), lambda s: (jnp.maximum(s - 1, 0), 0)),
        out_shape=jax.ShapeDtypeStruct((m, dm), F32),
        scratch_shapes=[pltpu.VMEM((tm, dm), F32)],
        compiler_params=_params("arbitrary"),
        name="outproj_mlp",
    )(*acts, w, x2d, *ln1, *mlp, *ln2)


def _merge_outproj_mlp_kernel(*refs, dilations):
    n = len(dilations)
    o_refs, lse_refs, p_refs = refs[:n], refs[n:2 * n], refs[2 * n:3 * n]
    w_ref, x_ref, g1_ref, b1_ref, w1_ref, w2_ref, g2_ref, b2_ref, out_ref, x1_ref = refs[3 * n:]
    nh, dh, grp = DIL_HEADS, DIL_HEAD_DIM, DIL_PERM_ROWS
    tm = x_ref.shape[0]

    @pl.when(pl.program_id(0) == 0)
    def _():
        x1_ref[...] = jnp.zeros_like(x1_ref)

    out_ref[...] = _mlp_ln(x1_ref, w1_ref, w2_ref, g2_ref[...], b2_ref[...])
    branch = []
    for o_ref, p_ref, d in zip(o_refs, p_refs, dilations):
        if d == 1:
            branch.append(o_ref[0].astype(F32))
            continue
        w = grp // d
        nat = []
        for gi in range(tm // grp):
            perm = jnp.concatenate([o_ref[r, gi * w:(gi + 1) * w, :] for r in range(d)], axis=0)
            nat.append(lax.dot_general(p_ref[...], perm, _TN, preferred_element_type=F32))
        branch.append(jnp.concatenate(nat, axis=0))
    parts = [ref[...] for ref in lse_refs]
    m = functools.reduce(jnp.maximum, parts)
    es = [jnp.exp(p - m) for p in parts]
    den = functools.reduce(jnp.add, es)
    wts = [e / den for e in es]
    heads = []
    for h in range(nh):
        hs = slice(h * dh, (h + 1) * dh)
        heads.append(functools.reduce(jnp.add, [wts[g][:, h:h + 1] * branch[g][:, hs] for g in range(n)]))
    proj = _bdot(jnp.concatenate(heads, axis=1), w_ref[...])
    x1_ref[...] = _layer_norm(DEEPNORM_ALPHA * x_ref[...] + proj, g1_ref[...], b1_ref[...])


def _merge_outproj_mlp(outs, lses, dilations, w, x2d, ln1, mlp, ln2, *, batch, seq_len):
    dm = x2d.shape[1]
    tm = LAYER_ROWS
    assert seq_len % tm == 0 and tm % DIL_PERM_ROWS == 0 and mlp[0].shape[1] % MLP_HIDDEN_CHUNK == 0
    x3 = x2d.reshape(batch, seq_len, dm)
    perms = [_residue_perm(d) for d in dilations]
    bps = seq_len // tm
    n_tiles = batch * bps
    tile = lambda s: jnp.minimum(s, n_tiles - 1)
    done = lambda s: jnp.maximum(s - 1, 0)
    const = lambda a: pl.BlockSpec(a.shape, lambda s: (0,) * a.ndim, pipeline_mode=pl.Buffered(1))
    rows3 = lambda a: pl.BlockSpec((None, tm, a.shape[2]), lambda s: (tile(s) // bps, tile(s) % bps, 0))
    in_specs = ([pl.BlockSpec((None, d, tm // d, C_WIDTH), lambda s: (tile(s) // bps, 0, tile(s) % bps, 0))
                 for d in dilations]
                + [rows3(a) for a in lses] + [const(p) for p in perms] + [const(w), rows3(x3)]
                + [const(a) for a in (*ln1, *mlp, *ln2)])
    out = pl.pallas_call(
        functools.partial(_merge_outproj_mlp_kernel, dilations=tuple(dilations)),
        grid=(n_tiles + 1,),
        in_specs=in_specs,
        out_specs=pl.BlockSpec((None, tm, dm), lambda s: (done(s) // bps, done(s) % bps, 0)),
        out_shape=jax.ShapeDtypeStruct((batch, seq_len, dm), F32),
        scratch_shapes=[pltpu.VMEM((tm, dm), F32)],
        compiler_params=_params("arbitrary"),
        name="merge_outproj_mlp",
    )(*outs, *lses, *perms, w, x3, *ln1, *mlp, *ln2)
    return out.reshape(batch * seq_len, dm)


def _rwkv_moba_layer(x2d, w_in, shift_mix, w0, w_up, a0, a_up, g_up, k_k, k_a, r_k, lnx_g, lnx_b,
                     w_out, ln1, mlp, ln2, *, batch, seq_len):
    d = RWKV_DIM
    tabs = _rotary_tables(seq_len, MOBA_HEAD_DIM ** -0.5)
    w_m = w_in[:, RWKV_PROJ_COLS:].astype(BF16)
    w_m = jnp.concatenate([_rotary_weight(w_m[:, :2 * MOBA_WIDTH], 2 * MOBA_HEADS),
                           w_m[:, 2 * MOBA_WIDTH:]], axis=1)
    kinds = [0] * MOBA_HEADS + [1] * MOBA_HEADS + [None] * MOBA_HEADS
    qkv = _rotary_proj(x2d, w_m, tabs, kinds, seq_len=seq_len)
    y_b = _moba(qkv, batch=batch, seq_len=seq_len)
    pad = RWKV_Z_COLS - RWKV_PROJ_COLS
    w_r = jnp.pad(w_in[:, :RWKV_PROJ_COLS], ((0, 0), (0, pad))).astype(BF16)
    mix = jnp.pad(shift_mix, (0, pad))[None, :]
    wl = jnp.zeros((LORA_PAD, 3 * d), F32)
    wl = wl.at[0:DECAY_LORA, 0:d].set(w_up)
    wl = wl.at[DECAY_LORA:DECAY_LORA + ICLR_LORA, d:2 * d].set(a_up)
    wl = wl.at[DECAY_LORA + ICLR_LORA:LORA_COLS, 2 * d:3 * d].set(g_up).astype(BF16)
    par = jnp.stack([w0, a0, k_k, k_a, r_k.reshape(d), lnx_g, lnx_b, jnp.zeros((d,), F32)])
    y_a = _rwkv_scan(x2d, w_r, mix, wl, par, batch=batch, seq_len=seq_len)
    return _outproj_mlp([y_a, y_b], w_out.astype(BF16), x2d, ln1, mlp, ln2)


def _dilated_layer(x2d, w_in, w_out, ln1, mlp, ln2, *, batch, seq_len):
    tabs = _rotary_tables(seq_len, DIL_HEAD_DIM ** -0.5)
    nqk = (DIL_GROUPS + 1) * C_WIDTH
    w_c = w_in.astype(BF16)
    w_c = jnp.concatenate([_rotary_weight(w_c[:, :nqk], (DIL_GROUPS + 1) * DIL_HEADS), w_c[:, nqk:]], axis=1)
    dilations = [dilation for _, dilation in DIL_PAIRS]
    assert all(span // dilation == DIL_WINDOW for span, dilation in DIL_PAIRS)
    srcs = _dilated_proj(x2d, w_c, tabs, dilations, batch=batch, seq_len=seq_len)
    outs, lses = [], []
    for src in srcs:
        o, lse = _dilated_group(src, (0, 1, 2), seq_len=seq_len)
        outs.append(o)
        lses.append(lse)
    return _merge_outproj_mlp(outs, lses, dilations, w_out.astype(BF16), x2d, ln1, mlp, ln2,
                              batch=batch, seq_len=seq_len)


def kernel(x, ab_w_in, ab_shift_mix, ab_w0, ab_w_up, ab_a0, ab_a_up, ab_g_up, ab_k_k, ab_k_a, ab_r_k,
           ab_lnx_g, ab_lnx_b, ab_w_out, c_w_in, c_w_out, ln1_g, ln1_b, mlp_w1, mlp_w2, ln2_g, ln2_b):
    batch, seq_len, dm = x.shape
    assert dm == D_MODEL
    h = x.reshape(batch * seq_len, dm)
    depth = ln1_g.shape[0]
    for layer in range(depth):
        j = layer // 2
        ln1 = (ln1_g[layer][None, :], ln1_b[layer][None, :])
        ln2 = (ln2_g[layer][None, :], ln2_b[layer][None, :])
        mlp = (mlp_w1[layer].astype(BF16), mlp_w2[layer].astype(BF16))
        if layer % 2 == 0:
            h = _rwkv_moba_layer(h, ab_w_in[j], ab_shift_mix[j], ab_w0[j], ab_w_up[j], ab_a0[j], ab_a_up[j],
                                 ab_g_up[j], ab_k_k[j], ab_k_a[j], ab_r_k[j], ab_lnx_g[j], ab_lnx_b[j],
                                 ab_w_out[j], ln1, mlp, ln2, batch=batch, seq_len=seq_len)
        else:
            h = _dilated_layer(h, c_w_in[j], c_w_out[j], ln1, mlp, ln2, batch=batch, seq_len=seq_len)
    return h.reshape(batch, seq_len, dm)
```

```python
import functools

import jax
import jax.numpy as jnp
from jax import lax
from jax.experimental import pallas as pl
from jax.experimental.pallas import tpu as pltpu

F32 = jnp.float32
BF16 = jnp.bfloat16

D_MODEL = 1024
DEPTH = 2
DEEPNORM_ALPHA = (2 * DEPTH) ** 0.25
LN_EPS = 1e-5
ROPE_THETA = 500000.0
ROPE_FRACTION = 4
RWKV_HEAD_DIM = 64
RWKV_DIM = D_MODEL // 2
RWKV_HEADS = RWKV_DIM // RWKV_HEAD_DIM
DECAY_LORA = 64
ICLR_LORA = 64
GATE_LORA = 160
RWKV_GN_EPS = 64e-5
RWKV_PROJ_COLS = 3 * RWKV_DIM + DECAY_LORA + ICLR_LORA + GATE_LORA
MOBA_HEAD_DIM = 128
MOBA_WIDTH = D_MODEL - RWKV_DIM
MOBA_HEADS = MOBA_WIDTH // MOBA_HEAD_DIM
MOBA_BLOCK = 256
MOBA_TOPK = 3
DIL_PAIRS = ((128, 1), (512, 4), (2048, 16))
DIL_GROUPS = len(DIL_PAIRS)
DIL_HEAD_DIM = 128
DIL_HEADS = D_MODEL // DIL_HEAD_DIM
C_WIDTH = DIL_HEADS * DIL_HEAD_DIM
C_IN_COLS = DIL_GROUPS * C_WIDTH + 2 * C_WIDTH
MLP_HIDDEN = 4 * D_MODEL
NEG_INF = -1e30

LANES = 128
SUBLANES = 8
VMEM_LIMIT_BYTES = 56 * 1024 * 1024

LORA_COLS = DECAY_LORA + ICLR_LORA + GATE_LORA
LORA_PAD = -(-LORA_COLS // LANES) * LANES
RWKV_Z_COLS = 3 * RWKV_DIM + LORA_PAD
RWKV_CHUNK = 64
RWKV_SCAN_CHUNKS = 8
DIL_WINDOW = 128
DIL_ROWS = 1024
DIL_PERM_ROWS = 256
PROJ_ROWS = 1024
PROJ_ROW_CHUNK = 256
LAYER_ROWS = 512
MLP_HIDDEN_CHUNK = 1024

_NN = (((1,), (0,)), ((), ()))
_NT = (((1,), (1,)), ((), ()))
_TN = (((0,), (0,)), ((), ()))


def _bdot(a, b, dims=_NN):
    return lax.dot_general(a.astype(BF16), b.astype(BF16), dims, preferred_element_type=F32)


def _layer_norm(y, g, b):
    mu = jnp.mean(y, axis=-1, keepdims=True)
    var = jnp.mean(jnp.square(y - mu), axis=-1, keepdims=True)
    return (y - mu) * lax.rsqrt(var + LN_EPS) * g + b


def _params(*sem):
    return pltpu.CompilerParams(dimension_semantics=sem, vmem_limit_bytes=VMEM_LIMIT_BYTES)


ROT_DIMS = MOBA_HEAD_DIM // ROPE_FRACTION
ROT_HALF = ROT_DIMS // 2


def _rotary_weight(w, n_heads):
    split = ROT_DIMS + LANES // 2 - ROT_HALF
    w3 = w.reshape(w.shape[0], n_heads, LANES)
    w3 = jnp.concatenate([w3[..., :ROT_HALF], w3[..., ROT_DIMS:split], w3[..., ROT_HALF:ROT_DIMS],
                          w3[..., split:]], axis=-1)
    return w3.reshape(w.shape[0], n_heads * LANES)


def _rotary_tables(seq_len, q_scale):
    inv_freq = ROPE_THETA ** (-jnp.arange(ROT_HALF, dtype=F32) / ROT_HALF)
    ang = jnp.arange(seq_len).astype(F32)[:, None] * inv_freq[None, :]
    cos, sin = jnp.cos(ang), jnp.sin(ang)
    pad = lambda a, fill: jnp.concatenate([a, jnp.full((seq_len, LANES // 2 - ROT_HALF), fill, F32)], axis=1)
    c = jnp.concatenate([pad(cos, 1.0), pad(cos, 1.0)], axis=1)
    s = jnp.concatenate([pad(-sin, 0.0), pad(sin, 0.0)], axis=1)
    one = jnp.ones_like(c)
    return jnp.stack([jnp.stack([c * q_scale, s * q_scale]), jnp.stack([c, s]),
                      jnp.stack([one, jnp.zeros_like(c)])])


def _rotary_proj_kernel(x_ref, w_ref, tab_ref, o_ref, *, head_kinds):
    tm = x_ref.shape[0]
    chunk = min(PROJ_ROW_CHUNK, tm)
    for r0 in range(0, tm, chunk):
        rs = slice(r0, r0 + chunk)
        acc = jnp.dot(x_ref[rs, :].astype(BF16), w_ref[...], preferred_element_type=F32)
        for hb, kind in enumerate(head_kinds):
            z = acc[:, hb * LANES:(hb + 1) * LANES]
            if kind is not None:
                z = z * tab_ref[kind, 0, rs, :] + pltpu.roll(z, LANES // 2, 1) * tab_ref[kind, 1, rs, :]
            o_ref[rs, hb * LANES:(hb + 1) * LANES] = z.astype(o_ref.dtype)


def _rotary_proj(x2d, w, tabs, head_kinds, *, seq_len):
    m, k = x2d.shape
    n = w.shape[1]
    tm = min(PROJ_ROWS, seq_len)
    assert m % tm == 0 and seq_len % tm == 0 and n == len(head_kinds) * LANES
    bps = seq_len // tm
    return pl.pallas_call(
        functools.partial(_rotary_proj_kernel, head_kinds=tuple(head_kinds)),
        grid=(m // tm,),
        in_specs=[pl.BlockSpec((tm, k), lambda i: (i, 0)),
                  pl.BlockSpec((k, n), lambda i: (0, 0), pipeline_mode=pl.Buffered(1)),
                  pl.BlockSpec((tabs.shape[0], 2, tm, LANES), lambda i: (0, 0, i % bps, 0))],
        out_specs=pl.BlockSpec((tm, n), lambda i: (i, 0)),
        out_shape=jax.ShapeDtypeStruct((m, n), BF16),
        compiler_params=_params("parallel"),
        name="rotary_proj",
    )(x2d, w, tabs)


def _softplus(x):
    return jnp.maximum(x, 0.0) + jnp.log(1.0 + jnp.exp(-jnp.abs(x)))


def _split_dot(x, ones_bd):
    hi = x.astype(BF16)
    lo = (x - hi.astype(F32)).astype(BF16)
    return (jnp.dot(hi, ones_bd, preferred_element_type=F32)
            + jnp.dot(lo, ones_bd, preferred_element_type=F32))


def _rwkv_prep(z, prev, mix, wl, par, ones_bd):
    d = RWKV_DIM
    pw = 2 * RWKV_HEAD_DIM
    row = lax.broadcasted_iota(jnp.int32, z.shape, 0)
    shifted = jnp.where(row == 0, prev, pltpu.roll(z, 1, 0))
    zz = z + (shifted - z) * mix
    r, k, v, lo = zz[:, 0:d], zz[:, d:2 * d], zz[:, 2 * d:3 * d], zz[:, 3 * d:]
    lane = lax.broadcasted_iota(jnp.int32, lo.shape, 1)
    act = jnp.where(lane < DECAY_LORA, jnp.tanh(lo),
                    jnp.where(lane < DECAY_LORA + ICLR_LORA, lo, jax.nn.sigmoid(lo)))
    lora = _bdot(act, wl)
    w0, a0, k_k, k_a = par[0:1, :], par[1:2, :], par[2:3, :], par[3:4, :]
    w = -_softplus(-(w0 + lora[:, 0:d])) - 0.5
    a = jax.nn.sigmoid(a0 + lora[:, d:2 * d])
    kk = k * k_k
    sq = kk * kk
    ssq = jnp.concatenate([_split_dot(sq[:, p0:p0 + pw], ones_bd) for p0 in range(0, d, pw)], axis=1)
    kk = kk / jnp.maximum(jnp.sqrt(ssq), 1e-12)
    return r, -jnp.exp(w), k * (1.0 + (a - 1.0) * k_a), v, -kk, kk * a, lora[:, 2 * d:3 * d]


def _rwkv_chunk_kernel(x_ref, wr_ref, mix_ref, wl_ref, par_ref, o_ref, s_ref, prev_ref, *, chunks):
    c, n = RWKV_CHUNK, RWKV_HEAD_DIM
    pw = 2 * n
    assert c == n and pw == LANES
    pairs = RWKV_HEADS // 2
    units = [(ch, p) for ch in range(chunks) for p in range(pairs)]

    first = pl.program_id(1) == 0

    @pl.when(first)
    def _():
        s_ref[...] = jnp.zeros_like(s_ref)

    row = lax.broadcasted_iota(jnp.int32, (c, pw), 0)
    col = lax.broadcasted_iota(jnp.int32, (c, pw), 1) & (n - 1)
    tril_inc = col <= row
    tril_str = col < row
    eye = col == row
    level_masks = [((row >> (lvl + 1)) == (col >> (lvl + 1))) & (((row >> lvl) & 1) == 1) & (((col >> lvl) & 1) == 0)
                   for lvl in range(c.bit_length() - 1)]
    r2 = lax.broadcasted_iota(jnp.int32, (pw, pw), 0)
    c2 = lax.broadcasted_iota(jnp.int32, (pw, pw), 1)
    same_head = (r2 < n) == (c2 < n)
    eye2 = r2 == c2
    ones_bd = jnp.where(same_head, 1.0, 0.0).astype(BF16)

    def stack(x):
        xb = x.astype(BF16)
        return jnp.where(same_head, jnp.concatenate([xb, xb], axis=0), jnp.zeros((), BF16))

    rows = chunks * c
    z = jnp.dot(x_ref[...].astype(BF16), wr_ref[...], preferred_element_type=F32)
    prev = jnp.where(first, 0.0, prev_ref[SUBLANES - 1:SUBLANES, :])
    r_all, lw, kp_all, v_all, na_all, b_all, g_all = _rwkv_prep(
        z, prev, mix_ref[...], wl_ref[...], par_ref[...], ones_bd)
    prev_ref[...] = z[rows - SUBLANES:, :]

    tr = lax.broadcasted_iota(jnp.int32, (rows, rows), 0)
    tc = lax.broadcasted_iota(jnp.int32, (rows, rows), 1)
    chunk_tril = ((tr >> (c.bit_length() - 1)) == (tc >> (c.bit_length() - 1))) & (tc <= tr)
    chunk_tril = jnp.where(chunk_tril, 1.0, 0.0).astype(BF16)
    lw_hi = lw.astype(BF16)
    lw_r1 = lw - lw_hi.astype(F32)
    lw_mid = lw_r1.astype(BF16)
    lw_lo = (lw_r1 - lw_mid.astype(F32)).astype(BF16)
    cum = (jnp.dot(chunk_tril, lw_hi, preferred_element_type=F32)
           + jnp.dot(chunk_tril, lw_mid, preferred_element_type=F32)
           + jnp.dot(chunk_tril, lw_lo, preferred_element_type=F32))
    r_k, lnx_g, lnx_b = par_ref[4:5, :], par_ref[5:6, :], par_ref[6:7, :]

    at, rt, bt, kt, bh, kh, vv, p_end = {}, {}, {}, {}, {}, {}, {}, {}
    for ch in range(chunks):
        rs = slice(ch * c, (ch + 1) * c)
        cum_c = cum[rs]
        cum_end = cum_c[c - 1:c, :]
        e_inv = jnp.exp(-cum_c)
        e_end = jnp.exp(cum_end - cum_c)
        at_c = na_all[rs] * jnp.exp(cum_c - lw[rs])
        rt_c = r_all[rs] * jnp.exp(cum_c)
        b_c, kp_c = b_all[rs], kp_all[rs]
        bt_c, kt_c, bh_c, kh_c = b_c * e_inv, kp_c * e_inv, b_c * e_end, kp_c * e_end
        pe_c = jnp.exp(cum_end)
        for p in range(pairs):
            ls = slice(p * pw, (p + 1) * pw)
            u = (ch, p)
            at[u], rt[u], bt[u], kt[u], bh[u], kh[u] = at_c[:, ls], rt_c[:, ls], bt_c[:, ls], kt_c[:, ls], bh_c[:, ls], kh_c[:, ls]
            vv[u], p_end[u] = v_all[rs, ls], pe_c[:, ls]

    ar = {u: jnp.concatenate([at[u], rt[u]], axis=0) for u in units}
    g = {u: _bdot(ar[u], jnp.concatenate([stack(bt[u]), stack(kt[u])], axis=0), _NT)
         for u in units}
    a_ab = {u: jnp.where(tril_str, g[u][:c, :pw], 0.0) for u in units}
    a_rb = {u: jnp.where(tril_inc, g[u][c:, :pw], 0.0) for u in units}
    a_ak = {u: jnp.where(tril_str, g[u][:c, pw:], 0.0) for u in units}
    a_rk = {u: jnp.where(tril_inc, g[u][c:, pw:], 0.0) for u in units}
    av = {u: _bdot(jnp.concatenate([a_ak[u], a_rk[u]], axis=0), stack(vv[u])) for u in units}
    ak_v = {u: av[u][:c] for u in units}
    inv = {u: jnp.where(eye, 1.0, jnp.where(level_masks[0], a_ab[u], 0.0)) for u in units}
    for mask in level_masks[1:]:
        e = {u: _bdot(jnp.where(mask, a_ab[u], 0.0), stack(inv[u])) for u in units}
        inv = {u: inv[u] + _bdot(inv[u], stack(e[u])) for u in units}
    wu = {u: _bdot(inv[u], jnp.concatenate([stack(at[u]), stack(ak_v[u])], axis=1)) for u in units}
    w_mat = {u: wu[u][:, :pw] for u in units}
    u_loc = {u: wu[u][:, pw:] for u in units}
    wub = {u: _bdot(wu[u], bh[u], _TN) for u in units}
    m_c = {u: jnp.where(same_head, wub[u][:pw], 0.0) + jnp.where(eye2, p_end[u], 0.0) for u in units}
    n_c = {u: jnp.where(same_head, wub[u][pw:] + _bdot(vv[u], kh[u], _TN), 0.0) for u in units}
    qy = {u: _bdot(a_rb[u], jnp.concatenate([stack(w_mat[u]), stack(u_loc[u])], axis=1)) for u in units}
    q_c = {u: rt[u] + qy[u][:, :pw] for u in units}
    y_loc = {u: qy[u][:, pw:] + av[u][c:] for u in units}

    state = [s_ref[p] for p in range(pairs)]
    y = {}
    for ch in range(chunks):
        for p in range(pairs):
            u = (ch, p)
            y[u] = _bdot(q_c[u], state[p], _NT) + y_loc[u]
            state[p] = _bdot(state[p], m_c[u]) + n_c[u]
    for p in range(pairs):
        s_ref[p] = state[p]

    inv_n = 1.0 / n
    mu = {u: _bdot(y[u], ones_bd) * inv_n for u in units}
    var = {u: _bdot(jnp.square(y[u] - mu[u]), ones_bd) * inv_n for u in units}
    for (ch, p) in units:
        u = (ch, p)
        rs, ls = slice(ch * c, (ch + 1) * c), slice(p * pw, (p + 1) * pw)
        yn = (y[u] - mu[u]) * lax.rsqrt(var[u] + RWKV_GN_EPS) * lnx_g[:, ls] + lnx_b[:, ls]
        bonus = _bdot(r_all[rs, ls] * kp_all[rs, ls] * r_k[:, ls], ones_bd) * vv[u]
        o_ref[rs, ls] = ((yn + bonus) * g_all[rs, ls]).astype(o_ref.dtype)


def _rwkv_scan(x2d, w_r, mix, wl, par, *, batch, seq_len):
    c = RWKV_CHUNK
    chunks = RWKV_SCAN_CHUNKS
    rows = chunks * c
    dm, zc = w_r.shape
    assert seq_len % rows == 0
    const = lambda a: pl.BlockSpec(a.shape, lambda bi, ci: (0, 0), pipeline_mode=pl.Buffered(1))
    out = pl.pallas_call(
        functools.partial(_rwkv_chunk_kernel, chunks=chunks),
        grid=(batch, seq_len // rows),
        in_specs=[pl.BlockSpec((None, rows, dm), lambda bi, ci: (bi, ci, 0)),
                  const(w_r), const(mix), const(wl), const(par)],
        out_specs=pl.BlockSpec((None, rows, RWKV_DIM), lambda bi, ci: (bi, ci, 0)),
        out_shape=jax.ShapeDtypeStruct((batch, seq_len, RWKV_DIM), BF16),
        scratch_shapes=[pltpu.VMEM((RWKV_HEADS // 2, 2 * RWKV_HEAD_DIM, 2 * RWKV_HEAD_DIM), F32),
                        pltpu.VMEM((SUBLANES, zc), F32)],
        compiler_params=_params("parallel", "arbitrary"),
        name="rwkv_scan",
    )(x2d.reshape(batch, seq_len, dm), w_r, mix, wl, par)
    return out.reshape(batch * seq_len, RWKV_DIM)


def _moba_kernel(q_ref, k_ref, v_ref, pool_ref, o_ref, kext_ref, vext_ref, s_ref):
    blk, dh = MOBA_BLOCK, MOBA_HEAD_DIM
    t_len = k_ref.shape[0]
    nb = t_len // blk
    nbp = -(-nb // SUBLANES) * SUBLANES
    shift = blk.bit_length() - 1
    key_tile = 2 * blk

    lane = lax.broadcasted_iota(jnp.int32, (blk, LANES), 1)
    for j in range(nb):
        rs = slice(j * blk, (j + 1) * blk)
        kext_ref[rs, 0:dh] = k_ref[rs, :]
        kext_ref[rs, dh:dh + LANES] = jnp.where(lane == j, 1.0, 0.0).astype(BF16)
        vext_ref[rs, 0:dh] = v_ref[rs, :]
        vext_ref[rs, dh:dh + LANES] = jnp.ones((blk, LANES), BF16)

    km = jnp.dot(pool_ref[...], k_ref[...], preferred_element_type=F32)[0:nbp]
    km_hi = km.astype(BF16)
    km_lo = (km - km_hi.astype(F32)).astype(BF16)
    q_all = q_ref[...]
    gate = (lax.dot_general(km_hi, q_all, _NT, preferred_element_type=F32)
            + lax.dot_general(km_lo, q_all, _NT, preferred_element_type=F32))
    bidx = lax.broadcasted_iota(jnp.int32, gate.shape, 0)
    qblk = lax.broadcasted_iota(jnp.int32, gate.shape, 1) >> shift
    past = bidx < qblk
    gate = jnp.where(past, gate, NEG_INF)
    rank = jnp.zeros(gate.shape, jnp.int32)
    for jp in range(nb):
        other = gate[jp:jp + 1, :]
        rank = rank + ((other > gate) | ((other == gate) & (jp < bidx))).astype(jnp.int32)
    allowed = (past & (rank < MOBA_TOPK)) | (bidx == qblk)
    bias_t = jnp.where(allowed, 0.0, NEG_INF).astype(BF16)
    bias_t = jnp.concatenate([bias_t, jnp.zeros((LANES - nbp, t_len), BF16)], axis=0)
    r2 = lax.broadcasted_iota(jnp.int32, (LANES, LANES), 0)
    c2 = lax.broadcasted_iota(jnp.int32, (LANES, LANES), 1)
    eye = jnp.where(r2 == c2, 1.0, 0.0).astype(BF16)
    row = lax.broadcasted_iota(jnp.int32, (blk, blk), 0)
    col = lax.broadcasted_iota(jnp.int32, (blk, blk), 1)
    causal = col <= row
    colmax = lambda x: functools.reduce(jnp.maximum, [x[:, a:a + LANES] for a in range(0, x.shape[1], LANES)])

    for i in range(nb):
        qs = slice(i * blk, (i + 1) * blk)
        bias = lax.dot_general(bias_t[:, qs], eye, _TN, preferred_element_type=F32)
        q_ext = jnp.concatenate([q_ref[qs, :], bias.astype(BF16)], axis=1)
        nk = (i + 1) * blk
        tiles = [(a, min(a + key_tile, nk)) for a in range(0, nk, key_tile)]
        m_lane = None
        for a, e in tiles:
            s = lax.dot_general(q_ext, kext_ref[a:e, :], _NT, preferred_element_type=F32)
            if e == nk:
                s_own = jnp.where(causal, s[:, e - a - blk:], NEG_INF)
                s = s_own if e - a == blk else jnp.concatenate([s[:, :e - a - blk], s_own], axis=1)
            s_ref[:, a:e] = s
            part = colmax(s)
            m_lane = part if m_lane is None else jnp.maximum(m_lane, part)
        m = jnp.max(m_lane, axis=-1, keepdims=True)
        acc = None
        for a, e in tiles:
            p = jnp.exp((s_ref[:, a:e] - m).astype(BF16))
            pv = jnp.dot(p, vext_ref[a:e, :], preferred_element_type=F32)
            acc = pv if acc is None else acc + pv
        o_ref[qs, :] = (acc[:, :dh] / acc[:, dh:]).astype(o_ref.dtype)


def _moba(qkv, *, batch, seq_len):
    blk, dh, nh = MOBA_BLOCK, MOBA_HEAD_DIM, MOBA_HEADS
    assert seq_len % blk == 0 and seq_len // blk <= LANES and dh == LANES
    pool = (jnp.arange(LANES)[:, None] == (jnp.arange(seq_len) // blk)[None, :]).astype(BF16) / blk
    qkv3 = qkv.reshape(batch, seq_len, 3 * MOBA_WIDTH)
    head = lambda off: pl.BlockSpec((None, seq_len, dh), lambda b, h: (b, 0, off + h))
    out = pl.pallas_call(
        _moba_kernel,
        grid=(batch, nh),
        in_specs=[head(0), head(nh), head(2 * nh), pl.BlockSpec((LANES, seq_len), lambda b, h: (0, 0))],
        out_specs=head(0),
        out_shape=jax.ShapeDtypeStruct((batch, seq_len, MOBA_WIDTH), BF16),
        scratch_shapes=[pltpu.VMEM((seq_len, dh + LANES), BF16), pltpu.VMEM((seq_len, dh + LANES), BF16),
                        pltpu.VMEM((blk, seq_len), F32)],
        compiler_params=_params("parallel", "parallel"),
        name="moba",
    )(qkv3, qkv3, qkv3, pool)
    return out.reshape(batch * seq_len, MOBA_WIDTH)


def _residue_perm(d):
    p = DIL_PERM_ROWS
    idx = jnp.arange(p)
    nat = (idx % (p // d)) * d + idx // (p // d)
    return (nat[:, None] == idx[None, :]).astype(BF16)


def _dilated_proj_kernel(x_ref, w_ref, tab_ref, *rest, dilations):
    n = len(dilations)
    p_refs, o_refs, xb_ref = rest[:n], rest[n:2 * n], rest[2 * n]
    j = pl.program_id(1)

    @pl.when(j == 0)
    def _():
        xb_ref[...] = x_ref[...].astype(BF16)

    tm = x_ref.shape[0]
    grp = DIL_PERM_ROWS
    nat_ref = o_refs[0]
    for r0 in range(0, tm, grp):
        rs = slice(r0, r0 + grp)
        acc = jnp.dot(xb_ref[rs, :], w_ref[...], preferred_element_type=F32)
        t_z, t_partner = tab_ref[0, rs, :], tab_ref[1, rs, :]
        for hb in range(acc.shape[1] // LANES):
            z = acc[:, hb * LANES:(hb + 1) * LANES]
            nat_ref[0, rs, hb * LANES:(hb + 1) * LANES] = (
                z * t_z + pltpu.roll(z, LANES // 2, 1) * t_partner).astype(nat_ref.dtype)

    for g in range(1, n):
        d = dilations[g]
        w = grp // d

        @pl.when((j == g) | (j >= n))
        def _(g=g, d=d, w=w):
            for gi in range(tm // grp):
                y = jnp.dot(p_refs[g][...], nat_ref[0, gi * grp:(gi + 1) * grp, :],
                            preferred_element_type=F32).astype(o_refs[g].dtype)
                for r in range(d):
                    o_refs[g][r, gi * w:(gi + 1) * w, :] = y[r * w:(r + 1) * w, :]


def _dilated_proj(x2d, w, tabs, dilations, *, batch, seq_len):
    m, k = x2d.shape
    n = len(dilations)
    tm = min(PROJ_ROWS, seq_len)
    assert dilations[0] == 1 and w.shape[1] == (n + 2) * C_WIDTH
    assert m % tm == 0 and seq_len % tm == 0 and tm % DIL_PERM_ROWS == 0
    assert all(DIL_PERM_ROWS % (d * 2 * SUBLANES) == 0 for d in dilations[1:])
    bps = seq_len // tm
    perms = [_residue_perm(d) for d in dilations]

    def tab_map(i, j):
        return ((j >= n).astype(jnp.int32) + (j > n).astype(jnp.int32), 0, i % bps, 0)

    def out_map(g):
        return lambda i, j: (i // bps, 0, i % bps, (j > g).astype(jnp.int32) + (j > n).astype(jnp.int32))

    outs = pl.pallas_call(
        functools.partial(_dilated_proj_kernel, dilations=tuple(dilations)),
        grid=(m // tm, n + 2),
        in_specs=[pl.BlockSpec((tm, k), lambda i, j: (i, 0)),
                  pl.BlockSpec((k, C_WIDTH), lambda i, j: (0, j)),
                  pl.BlockSpec((None, 2, tm, LANES), tab_map)]
                 + [pl.BlockSpec(p.shape, lambda i, j: (0, 0)) for p in perms],
        out_specs=[pl.BlockSpec((None, d, tm // d, C_WIDTH), out_map(g)) for g, d in enumerate(dilations)],
        out_shape=[jax.ShapeDtypeStruct((batch, d, seq_len // d, 3 * C_WIDTH), BF16) for d in dilations],
        scratch_shapes=[pltpu.VMEM((tm, k), BF16)],
        compiler_params=_params("parallel", "arbitrary"),
        name="dilated_proj",
    )(x2d, w, tabs, *perms)
    return outs


def _dilated_kernel(q_ref, k_ref, kp_ref, v_ref, vp_ref, o_ref, lse_ref):
    w, dh = DIL_WINDOW, DIL_HEAD_DIM
    n_res, tiles = q_ref.shape[0], q_ref.shape[1] // w
    has_prev = pl.program_id(2) > 0
    row = lax.broadcasted_iota(jnp.int32, (w, 2 * w), 0)
    col = lax.broadcasted_iota(jnp.int32, (w, 2 * w), 1)
    band = (col >= row) & (col <= row + w)
    first_band = band & (has_prev | (col >= w))
    er = lax.broadcasted_iota(jnp.int32, (w, w), 0)
    ec = lax.broadcasted_iota(jnp.int32, (w, w), 1)
    eye = er == ec
    units = [(r, h, t) for r in range(n_res) for h in range(DIL_HEADS) for t in range(tiles)]

    def banded(ref, halo_ref, r, h, t):
        hs = slice(h * dh, (h + 1) * dh)
        if t == 0:
            return jnp.concatenate([halo_ref[r, :, hs], ref[r, 0:w, hs]], axis=0)
        return ref[r, (t - 1) * w:(t + 1) * w, hs]

    s = {}
    for r, h, t in units:
        q = q_ref[r, t * w:(t + 1) * w, h * dh:(h + 1) * dh]
        sc = lax.dot_general(q, banded(k_ref, kp_ref, r, h, t), _NT, preferred_element_type=F32)
        s[r, h, t] = jnp.where(first_band if t == 0 else band, sc, NEG_INF)
    m = {u: jnp.max(jnp.maximum(s[u][:, :w], s[u][:, w:]), axis=-1, keepdims=True) for u in units}
    p = {u: jnp.exp((s[u] - m[u]).astype(BF16)) for u in units}
    ones = jnp.ones((2 * w, dh), BF16)
    ol = {u: jnp.dot(p[u], jnp.concatenate([banded(v_ref, vp_ref, *u), ones], axis=1),
                     preferred_element_type=F32) for u in units}
    for r, h, t in units:
        o, l = ol[r, h, t][:, :dh], ol[r, h, t][:, dh:]
        o_ref[r, t * w:(t + 1) * w, h * dh:(h + 1) * dh] = (o / l).astype(o_ref.dtype)
        lse = m[r, h, t] + jnp.log(l)
        lse_ref[r, t, h:h + 1, :] = jnp.sum(jnp.where(eye, lse, 0.0), axis=0, keepdims=True)


def _dilated_group(src, cols, *, seq_len):
    batch, d, length, _ = src.shape
    w = DIL_WINDOW
    rows = min(DIL_ROWS, length)
    n_res = min(d, DIL_ROWS // rows)
    assert length % rows == 0 and rows % w == 0 and length * d == seq_len and d % n_res == 0
    tiles = rows // w
    qc, kc, vc = cols
    blk = lambda cb: pl.BlockSpec((None, n_res, rows, C_WIDTH), lambda b, r, i: (b, r, i, cb))
    halo = lambda cb: pl.BlockSpec(
        (None, n_res, w, C_WIDTH), lambda b, r, i: (b, r, jnp.maximum(i * tiles - 1, 0), cb))
    o, lse = pl.pallas_call(
        _dilated_kernel,
        grid=(batch, d // n_res, length // rows),
        in_specs=[blk(qc), blk(kc), halo(kc), blk(vc), halo(vc)],
        out_specs=[
            pl.BlockSpec((None, n_res, rows, C_WIDTH), lambda b, r, i: (b, r, i, 0)),
            pl.BlockSpec((None, n_res, tiles, DIL_HEADS, w), lambda b, r, i: (b, r, i, 0, 0)),
        ],
        out_shape=[
            jax.ShapeDtypeStruct((batch, d, length, C_WIDTH), BF16),
            jax.ShapeDtypeStruct((batch, d, length // w, DIL_HEADS, w), F32),
        ],
        compiler_params=_params("parallel", "parallel", "arbitrary"),
        name=f"dilated_d{d}",
    )(src, src, src, src, src)
    lse = lse.transpose(0, 2, 4, 1, 3).reshape(batch, seq_len, DIL_HEADS)
    return o, lse


def _mlp_ln(x1_ref, w1_ref, w2_ref, ln2_g, ln2_b):
    xb = x1_ref[...].astype(BF16)
    hs = []
    for c0 in range(0, w1_ref.shape[1], MLP_HIDDEN_CHUNK):
        h = jnp.dot(xb, w1_ref[:, c0:c0 + MLP_HIDDEN_CHUNK], preferred_element_type=F32)
        hs.append(jnp.square(jnp.maximum(h, 0.0)).astype(BF16))
    y = jnp.dot(jnp.concatenate(hs, axis=1), w2_ref[...], preferred_element_type=F32)
    return _layer_norm(DEEPNORM_ALPHA * x1_ref[...] + y, ln2_g, ln2_b)


def _outproj_mlp_kernel(*refs, n_act):
    acts = refs[:n_act]
    w_ref, x_ref, g1_ref, b1_ref, w1_ref, w2_ref, g2_ref, b2_ref, o_ref, x1_ref = refs[n_act:]

    @pl.when(pl.program_id(0) == 0)
    def _():
        x1_ref[...] = jnp.zeros_like(x1_ref)

    o_ref[...] = _mlp_ln(x1_ref, w1_ref, w2_ref, g2_ref[...], b2_ref[...])
    proj = None
    off = 0
    for a_ref in acts:
        kw = a_ref.shape[1]
        part = jnp.dot(a_ref[...], w_ref[off:off + kw, :], preferred_element_type=F32)
        proj = part if proj is None else proj + part
        off += kw
    x1_ref[...] = _layer_norm(DEEPNORM_ALPHA * x_ref[...] + proj, g1_ref[...], b1_ref[...])


def _outproj_mlp(acts, w, x2d, ln1, mlp, ln2):
    m, dm = x2d.shape
    tm = LAYER_ROWS
    assert m % tm == 0 and mlp[0].shape[1] % MLP_HIDDEN_CHUNK == 0
    last = m // tm - 1
    rowspec = lambda a: pl.BlockSpec((tm, a.shape[1]), lambda s: (jnp.minimum(s, last), 0))
    const = lambda a: pl.BlockSpec(a.shape, lambda s: (0, 0), pipeline_mode=pl.Buffered(1))
    return pl.pallas_call(
        functools.partial(_outproj_mlp_kernel, n_act=len(acts)),
        grid=(m // tm + 1,),
        in_specs=([rowspec(a) for a in acts] + [const(w), rowspec(x2d)]
                  + [const(a) for a in (*ln1, *mlp, *ln2)]),
        out_specs=pl.BlockSpec((tm, dm), lambda s: (jnp.maximum(s - 1, 0), 0)),
        out_shape=jax.ShapeDtypeStruct((m, dm), F32),
        scratch_shapes=[pltpu.VMEM((tm, dm), F32)],
        compiler_params=_params("arbitrary"),
        name="outproj_mlp",
    )(*acts, w, x2d, *ln1, *mlp, *ln2)


def _merge_outproj_mlp_kernel(*refs, dilations):
    n = len(dilations)
    o_refs, lse_refs, p_refs = refs[:n], refs[n:2 * n], refs[2 * n:3 * n]
    w_ref, x_ref, g1_ref, b1_ref, w1_ref, w2_ref, g2_ref, b2_ref, out_ref, x1_ref = refs[3 * n:]
    nh, dh, grp = DIL_HEADS, DIL_HEAD_DIM, DIL_PERM_ROWS
    tm = x_ref.shape[0]

    @pl.when(pl.program_id(0) == 0)
    def _():
        x1_ref[...] = jnp.zeros_like(x1_ref)

    out_ref[...] = _mlp_ln(x1_ref, w1_ref, w2_ref, g2_ref[...], b2_ref[...])
    branch = []
    for o_ref, p_ref, d in zip(o_refs, p_refs, dilations):
        if d == 1:
            branch.append(o_ref[0].astype(F32))
            continue
        w = grp // d
        nat = []
        for gi in range(tm // grp):
            perm = jnp.concatenate([o_ref[r, gi * w:(gi + 1) * w, :] for r in range(d)], axis=0)
            nat.append(lax.dot_general(p_ref[...], perm, _TN, preferred_element_type=F32))
        branch.append(jnp.concatenate(nat, axis=0))
    parts = [ref[...] for ref in lse_refs]
    m = functools.reduce(jnp.maximum, parts)
    es = [jnp.exp(p - m) for p in parts]
    den = functools.reduce(jnp.add, es)
    wts = [e / den for e in es]
    heads = []
    for h in range(nh):
        hs = slice(h * dh, (h + 1) * dh)
        heads.append(functools.reduce(jnp.add, [wts[g][:, h:h + 1] * branch[g][:, hs] for g in range(n)]))
    proj = _bdot(jnp.concatenate(heads, axis=1), w_ref[...])
    x1_ref[...] = _layer_norm(DEEPNORM_ALPHA * x_ref[...] + proj, g1_ref[...], b1_ref[...])


def _merge_outproj_mlp(outs, lses, dilations, w, x2d, ln1, mlp, ln2, *, batch, seq_len):
    dm = x2d.shape[1]
    tm = LAYER_ROWS
    assert seq_len % tm == 0 and tm % DIL_PERM_ROWS == 0 and mlp[0].shape[1] % MLP_HIDDEN_CHUNK == 0
    x3 = x2d.reshape(batch, seq_len, dm)
    perms = [_residue_perm(d) for d in dilations]
    bps = seq_len // tm
    n_tiles = batch * bps
    tile = lambda s: jnp.minimum(s, n_tiles - 1)
    done = lambda s: jnp.maximum(s - 1, 0)
    const = lambda a: pl.BlockSpec(a.shape, lambda s: (0,) * a.ndim, pipeline_mode=pl.Buffered(1))
    rows3 = lambda a: pl.BlockSpec((None, tm, a.shape[2]), lambda s: (tile(s) // bps, tile(s) % bps, 0))
    in_specs = ([pl.BlockSpec((None, d, tm // d, C_WIDTH), lambda s: (tile(s) // bps, 0, tile(s) % bps, 0))
                 for d in dilations]
                + [rows3(a) for a in lses] + [const(p) for p in perms] + [const(w), rows3(x3)]
                + [const(a) for a in (*ln1, *mlp, *ln2)])
    out = pl.pallas_call(
        functools.partial(_merge_outproj_mlp_kernel, dilations=tuple(dilations)),
        grid=(n_tiles + 1,),
        in_specs=in_specs,
        out_specs=pl.BlockSpec((None, tm, dm), lambda s: (done(s) // bps, done(s) % bps, 0)),
        out_shape=jax.ShapeDtypeStruct((batch, seq_len, dm), F32),
        scratch_shapes=[pltpu.VMEM((tm, dm), F32)],
        compiler_params=_params("arbitrary"),
        name="merge_outproj_mlp",
    )(*outs, *lses, *perms, w, x3, *ln1, *mlp, *ln2)
    return out.reshape(batch * seq_len, dm)


def _rwkv_moba_layer(x2d, w_in, shift_mix, w0, w_up, a0, a_up, g_up, k_k, k_a, r_k, lnx_g, lnx_b,
                     w_out, ln1, mlp, ln2, *, batch, seq_len):
    d = RWKV_DIM
    tabs = _rotary_tables(seq_len, MOBA_HEAD_DIM ** -0.5)
    w_m = w_in[:, RWKV_PROJ_COLS:].astype(BF16)
    w_m = jnp.concatenate([_rotary_weight(w_m[:, :2 * MOBA_WIDTH], 2 * MOBA_HEADS),
                           w_m[:, 2 * MOBA_WIDTH:]], axis=1)
    kinds = [0] * MOBA_HEADS + [1] * MOBA_HEADS + [None] * MOBA_HEADS
    qkv = _rotary_proj(x2d, w_m, tabs, kinds, seq_len=seq_len)
    y_b = _moba(qkv, batch=batch, seq_len=seq_len)
    pad = RWKV_Z_COLS - RWKV_PROJ_COLS
    w_r = jnp.pad(w_in[:, :RWKV_PROJ_COLS], ((0, 0), (0, pad))).astype(BF16)
    mix = jnp.pad(shift_mix, (0, pad))[None, :]
    wl = jnp.zeros((LORA_PAD, 3 * d), F32)
    wl = wl.at[0:DECAY_LORA, 0:d].set(w_up)
    wl = wl.at[DECAY_LORA:DECAY_LORA + ICLR_LORA, d:2 * d].set(a_up)
    wl = wl.at[DECAY_LORA + ICLR_LORA:LORA_COLS, 2 * d:3 * d].set(g_up).astype(BF16)
    par = jnp.stack([w0, a0, k_k, k_a, r_k.reshape(d), lnx_g, lnx_b, jnp.zeros((d,), F32)])
    y_a = _rwkv_scan(x2d, w_r, mix, wl, par, batch=batch, seq_len=seq_len)
    return _outproj_mlp([y_a, y_b], w_out.astype(BF16), x2d, ln1, mlp, ln2)


def _dilated_layer(x2d, w_in, w_out, ln1, mlp, ln2, *, batch, seq_len):
    tabs = _rotary_tables(seq_len, DIL_HEAD_DIM ** -0.5)
    nqk = (DIL_GROUPS + 1) * C_WIDTH
    w_c = w_in.astype(BF16)
    w_c = jnp.concatenate([_rotary_weight(w_c[:, :nqk], (DIL_GROUPS + 1) * DIL_HEADS), w_c[:, nqk:]], axis=1)
    dilations = [dilation for _, dilation in DIL_PAIRS]
    assert all(span // dilation == DIL_WINDOW for span, dilation in DIL_PAIRS)
    srcs = _dilated_proj(x2d, w_c, tabs, dilations, batch=batch, seq_len=seq_len)
    outs, lses = [], []
    for src in srcs:
        o, lse = _dilated_group(src, (0, 1, 2), seq_len=seq_len)
        outs.append(o)
        lses.append(lse)
    return _merge_outproj_mlp(outs, lses, dilations, w_out.astype(BF16), x2d, ln1, mlp, ln2,
                              batch=batch, seq_len=seq_len)


def kernel(x, ab_w_in, ab_shift_mix, ab_w0, ab_w_up, ab_a0, ab_a_up, ab_g_up, ab_k_k, ab_k_a, ab_r_k,
           ab_lnx_g, ab_lnx_b, ab_w_out, c_w_in, c_w_out, ln1_g, ln1_b, mlp_w1, mlp_w2, ln2_g, ln2_b):
    batch, seq_len, dm = x.shape
    assert dm == D_MODEL
    h = x.reshape(batch * seq_len, dm)
    depth = ln1_g.shape[0]
    for layer in range(depth):
        j = layer // 2
        ln1 = (ln1_g[layer][None, :], ln1_b[layer][None, :])
        ln2 = (ln2_g[layer][None, :], ln2_b[layer][None, :])
        mlp = (mlp_w1[layer].astype(BF16), mlp_w2[layer].astype(BF16))
        if layer % 2 == 0:
            h = _rwkv_moba_layer(h, ab_w_in[j], ab_shift_mix[j], ab_w0[j], ab_w_up[j], ab_a0[j], ab_a_up[j],
                                 ab_g_up[j], ab_k_k[j], ab_k_a[j], ab_r_k[j], ab_lnx_g[j], ab_lnx_b[j],
                                 ab_w_out[j], ln1, mlp, ln2, batch=batch, seq_len=seq_len)
        else:
            h = _dilated_layer(h, c_w_in[j], c_w_out[j], ln1, mlp, ln2, batch=batch, seq_len=seq_len)
    return h.reshape(batch, seq_len, dm)
```

```python
import functools

import jax
import jax.numpy as jnp
from jax import lax
from jax.experimental import pallas as pl
from jax.experimental.pallas import tpu as pltpu

F32 = jnp.float32
BF16 = jnp.bfloat16

D_MODEL = 1024
DEPTH = 2
DEEPNORM_ALPHA = (2 * DEPTH) ** 0.25
LN_EPS = 1e-5
ROPE_THETA = 500000.0
ROPE_FRACTION = 4
RWKV_HEAD_DIM = 64
RWKV_DIM = D_MODEL // 2
RWKV_HEADS = RWKV_DIM // RWKV_HEAD_DIM
DECAY_LORA = 64
ICLR_LORA = 64
GATE_LORA = 160
RWKV_GN_EPS = 64e-5
RWKV_PROJ_COLS = 3 * RWKV_DIM + DECAY_LORA + ICLR_LORA + GATE_LORA
MOBA_HEAD_DIM = 128
MOBA_WIDTH = D_MODEL - RWKV_DIM
MOBA_HEADS = MOBA_WIDTH // MOBA_HEAD_DIM
MOBA_BLOCK = 256
MOBA_TOPK = 3
DIL_PAIRS = ((128, 1), (512, 4), (2048, 16))
DIL_GROUPS = len(DIL_PAIRS)
DIL_HEAD_DIM = 128
DIL_HEADS = D_MODEL // DIL_HEAD_DIM
C_WIDTH = DIL_HEADS * DIL_HEAD_DIM
C_IN_COLS = DIL_GROUPS * C_WIDTH + 2 * C_WIDTH
MLP_HIDDEN = 4 * D_MODEL
NEG_INF = -1e30

LANES = 128
SUBLANES = 8
VMEM_LIMIT_BYTES = 56 * 1024 * 1024

LORA_COLS = DECAY_LORA + ICLR_LORA + GATE_LORA
LORA_PAD = -(-LORA_COLS // LANES) * LANES
RWKV_Z_COLS = 3 * RWKV_DIM + LORA_PAD
MOBA_Q_BLOCKS = 2
RWKV_CHUNK = 64
RWKV_SCAN_CHUNKS = 8
DIL_WINDOW = 128
DIL_ROWS = 1024
DIL_PERM_ROWS = 256
PROJ_ROWS = 1024
PROJ_ROW_CHUNK = 256
LAYER_ROWS = 512
MLP_HIDDEN_CHUNK = 1024

_NN = (((1,), (0,)), ((), ()))
_NT = (((1,), (1,)), ((), ()))
_TN = (((0,), (0,)), ((), ()))


def _bdot(a, b, dims=_NN):
    return lax.dot_general(a.astype(BF16), b.astype(BF16), dims, preferred_element_type=F32)


def _layer_norm(y, g, b):
    mu = jnp.mean(y, axis=-1, keepdims=True)
    var = jnp.mean(jnp.square(y - mu), axis=-1, keepdims=True)
    return (y - mu) * lax.rsqrt(var + LN_EPS) * g + b


def _params(*sem):
    return pltpu.CompilerParams(dimension_semantics=sem, vmem_limit_bytes=VMEM_LIMIT_BYTES)


ROT_DIMS = MOBA_HEAD_DIM // ROPE_FRACTION
ROT_HALF = ROT_DIMS // 2


def _rotary_weight(w, n_heads):
    split = ROT_DIMS + LANES // 2 - ROT_HALF
    w3 = w.reshape(w.shape[0], n_heads, LANES)
    w3 = jnp.concatenate([w3[..., :ROT_HALF], w3[..., ROT_DIMS:split], w3[..., ROT_HALF:ROT_DIMS],
                          w3[..., split:]], axis=-1)
    return w3.reshape(w.shape[0], n_heads * LANES)


def _rotary_tables(seq_len, q_scale):
    inv_freq = ROPE_THETA ** (-jnp.arange(ROT_HALF, dtype=F32) / ROT_HALF)
    ang = jnp.arange(seq_len).astype(F32)[:, None] * inv_freq[None, :]
    cos, sin = jnp.cos(ang), jnp.sin(ang)
    pad = lambda a, fill: jnp.concatenate([a, jnp.full((seq_len, LANES // 2 - ROT_HALF), fill, F32)], axis=1)
    c = jnp.concatenate([pad(cos, 1.0), pad(cos, 1.0)], axis=1)
    s = jnp.concatenate([pad(-sin, 0.0), pad(sin, 0.0)], axis=1)
    one = jnp.ones_like(c)
    return jnp.stack([jnp.stack([c * q_scale, s * q_scale]), jnp.stack([c, s]),
                      jnp.stack([one, jnp.zeros_like(c)])])


def _rotary_proj_kernel(x_ref, w_ref, tab_ref, o_ref, *, head_kinds):
    tm = x_ref.shape[0]
    chunk = min(PROJ_ROW_CHUNK, tm)
    for r0 in range(0, tm, chunk):
        rs = slice(r0, r0 + chunk)
        acc = jnp.dot(x_ref[rs, :].astype(BF16), w_ref[...], preferred_element_type=F32)
        for hb, kind in enumerate(head_kinds):
            z = acc[:, hb * LANES:(hb + 1) * LANES]
            if kind is not None:
                z = z * tab_ref[kind, 0, rs, :] + pltpu.roll(z, LANES // 2, 1) * tab_ref[kind, 1, rs, :]
            o_ref[rs, hb * LANES:(hb + 1) * LANES] = z.astype(o_ref.dtype)


def _rotary_proj(x2d, w, tabs, head_kinds, *, seq_len):
    m, k = x2d.shape
    n = w.shape[1]
    tm = min(PROJ_ROWS, seq_len)
    assert m % tm == 0 and seq_len % tm == 0 and n == len(head_kinds) * LANES
    bps = seq_len // tm
    return pl.pallas_call(
        functools.partial(_rotary_proj_kernel, head_kinds=tuple(head_kinds)),
        grid=(m // tm,),
        in_specs=[pl.BlockSpec((tm, k), lambda i: (i, 0)),
                  pl.BlockSpec((k, n), lambda i: (0, 0), pipeline_mode=pl.Buffered(1)),
                  pl.BlockSpec((tabs.shape[0], 2, tm, LANES), lambda i: (0, 0, i % bps, 0))],
        out_specs=pl.BlockSpec((tm, n), lambda i: (i, 0)),
        out_shape=jax.ShapeDtypeStruct((m, n), BF16),
        compiler_params=_params("parallel"),
        name="rotary_proj",
    )(x2d, w, tabs)


def _softplus(x):
    return jnp.maximum(x, 0.0) + jnp.log(1.0 + jnp.exp(-jnp.abs(x)))


def _split_dot(x, ones_bd):
    hi = x.astype(BF16)
    lo = (x - hi.astype(F32)).astype(BF16)
    return (jnp.dot(hi, ones_bd, preferred_element_type=F32)
            + jnp.dot(lo, ones_bd, preferred_element_type=F32))


def _rwkv_prep(z, prev, mix, wl, par, ones_bd):
    d = RWKV_DIM
    pw = 2 * RWKV_HEAD_DIM
    row = lax.broadcasted_iota(jnp.int32, z.shape, 0)
    shifted = jnp.where(row == 0, prev, pltpu.roll(z, 1, 0))
    zz = z + (shifted - z) * mix
    r, k, v, lo = zz[:, 0:d], zz[:, d:2 * d], zz[:, 2 * d:3 * d], zz[:, 3 * d:]
    lane = lax.broadcasted_iota(jnp.int32, lo.shape, 1)
    act = jnp.where(lane < DECAY_LORA, jnp.tanh(lo),
                    jnp.where(lane < DECAY_LORA + ICLR_LORA, lo, jax.nn.sigmoid(lo)))
    lora = _bdot(act, wl)
    w0, a0, k_k, k_a = par[0:1, :], par[1:2, :], par[2:3, :], par[3:4, :]
    w = -_softplus(-(w0 + lora[:, 0:d])) - 0.5
    a = jax.nn.sigmoid(a0 + lora[:, d:2 * d])
    kk = k * k_k
    sq = kk * kk
    ssq = jnp.concatenate([_split_dot(sq[:, p0:p0 + pw], ones_bd) for p0 in range(0, d, pw)], axis=1)
    kk = kk / jnp.maximum(jnp.sqrt(ssq), 1e-12)
    return r, -jnp.exp(w), k * (1.0 + (a - 1.0) * k_a), v, -kk, kk * a, lora[:, 2 * d:3 * d]


def _rwkv_chunk_kernel(x_ref, wr_ref, mix_ref, wl_ref, par_ref, o_ref, s_ref, prev_ref, *, chunks):
    c, n = RWKV_CHUNK, RWKV_HEAD_DIM
    pw = 2 * n
    assert c == n and pw == LANES
    pairs = RWKV_HEADS // 2
    units = [(ch, p) for ch in range(chunks) for p in range(pairs)]

    first = pl.program_id(1) == 0

    @pl.when(first)
    def _():
        s_ref[...] = jnp.zeros_like(s_ref)

    row = lax.broadcasted_iota(jnp.int32, (c, pw), 0)
    col = lax.broadcasted_iota(jnp.int32, (c, pw), 1) & (n - 1)
    tril_inc = col <= row
    tril_str = col < row
    eye = col == row
    level_masks = [((row >> (lvl + 1)) == (col >> (lvl + 1))) & (((row >> lvl) & 1) == 1) & (((col >> lvl) & 1) == 0)
                   for lvl in range(c.bit_length() - 1)]
    r2 = lax.broadcasted_iota(jnp.int32, (pw, pw), 0)
    c2 = lax.broadcasted_iota(jnp.int32, (pw, pw), 1)
    same_head = (r2 < n) == (c2 < n)
    eye2 = r2 == c2
    ones_bd = jnp.where(same_head, 1.0, 0.0).astype(BF16)

    def stack(x):
        xb = x.astype(BF16)
        return jnp.where(same_head, jnp.concatenate([xb, xb], axis=0), jnp.zeros((), BF16))

    rows = chunks * c
    z = jnp.dot(x_ref[...].astype(BF16), wr_ref[...], preferred_element_type=F32)
    prev = jnp.where(first, 0.0, prev_ref[SUBLANES - 1:SUBLANES, :])
    r_all, lw, kp_all, v_all, na_all, b_all, g_all = _rwkv_prep(
        z, prev, mix_ref[...], wl_ref[...], par_ref[...], ones_bd)
    prev_ref[...] = z[rows - SUBLANES:, :]

    tr = lax.broadcasted_iota(jnp.int32, (rows, rows), 0)
    tc = lax.broadcasted_iota(jnp.int32, (rows, rows), 1)
    chunk_tril = ((tr >> (c.bit_length() - 1)) == (tc >> (c.bit_length() - 1))) & (tc <= tr)
    chunk_tril = jnp.where(chunk_tril, 1.0, 0.0).astype(BF16)
    lw_hi = lw.astype(BF16)
    lw_r1 = lw - lw_hi.astype(F32)
    lw_mid = lw_r1.astype(BF16)
    lw_lo = (lw_r1 - lw_mid.astype(F32)).astype(BF16)
    cum = (jnp.dot(chunk_tril, lw_hi, preferred_element_type=F32)
           + jnp.dot(chunk_tril, lw_mid, preferred_element_type=F32)
           + jnp.dot(chunk_tril, lw_lo, preferred_element_type=F32))
    r_k, lnx_g, lnx_b = par_ref[4:5, :], par_ref[5:6, :], par_ref[6:7, :]

    at, rt, bt, kt, bh, kh, vv, p_end = {}, {}, {}, {}, {}, {}, {}, {}
    for ch in range(chunks):
        rs = slice(ch * c, (ch + 1) * c)
        cum_c = cum[rs]
        cum_end = cum_c[c - 1:c, :]
        e_inv = jnp.exp(-cum_c)
        e_end = jnp.exp(cum_end - cum_c)
        at_c = na_all[rs] * jnp.exp(cum_c - lw[rs])
        rt_c = r_all[rs] * jnp.exp(cum_c)
        b_c, kp_c = b_all[rs], kp_all[rs]
        bt_c, kt_c, bh_c, kh_c = b_c * e_inv, kp_c * e_inv, b_c * e_end, kp_c * e_end
        pe_c = jnp.exp(cum_end)
        for p in range(pairs):
            ls = slice(p * pw, (p + 1) * pw)
            u = (ch, p)
            at[u], rt[u], bt[u], kt[u], bh[u], kh[u] = at_c[:, ls], rt_c[:, ls], bt_c[:, ls], kt_c[:, ls], bh_c[:, ls], kh_c[:, ls]
            vv[u], p_end[u] = v_all[rs, ls], pe_c[:, ls]

    ar = {u: jnp.concatenate([at[u], rt[u]], axis=0) for u in units}
    g = {u: _bdot(ar[u], jnp.concatenate([stack(bt[u]), stack(kt[u])], axis=0), _NT)
         for u in units}
    a_ab = {u: jnp.where(tril_str, g[u][:c, :pw], 0.0) for u in units}
    a_rb = {u: jnp.where(tril_inc, g[u][c:, :pw], 0.0) for u in units}
    a_ak = {u: jnp.where(tril_str, g[u][:c, pw:], 0.0) for u in units}
    a_rk = {u: jnp.where(tril_inc, g[u][c:, pw:], 0.0) for u in units}
    av = {u: _bdot(jnp.concatenate([a_ak[u], a_rk[u]], axis=0), stack(vv[u])) for u in units}
    ak_v = {u: av[u][:c] for u in units}
    inv = {u: jnp.where(eye, 1.0, jnp.where(level_masks[0], a_ab[u], 0.0)) for u in units}
    for mask in level_masks[1:]:
        e = {u: _bdot(jnp.where(mask, a_ab[u], 0.0), stack(inv[u])) for u in units}
        inv = {u: inv[u] + _bdot(inv[u], stack(e[u])) for u in units}
    wu = {u: _bdot(inv[u], jnp.concatenate([stack(at[u]), stack(ak_v[u])], axis=1)) for u in units}
    w_mat = {u: wu[u][:, :pw] for u in units}
    u_loc = {u: wu[u][:, pw:] for u in units}
    wub = {u: _bdot(wu[u], bh[u], _TN) for u in units}
    m_c = {u: jnp.where(same_head, wub[u][:pw], 0.0) + jnp.where(eye2, p_end[u], 0.0) for u in units}
    n_c = {u: jnp.where(same_head, wub[u][pw:] + _bdot(vv[u], kh[u], _TN), 0.0) for u in units}
    qy = {u: _bdot(a_rb[u], jnp.concatenate([stack(w_mat[u]), stack(u_loc[u])], axis=1)) for u in units}
    q_c = {u: rt[u] + qy[u][:, :pw] for u in units}
    y_loc = {u: qy[u][:, pw:] + av[u][c:] for u in units}

    state = [s_ref[p] for p in range(pairs)]
    y = {}
    for ch in range(chunks):
        for p in range(pairs):
            u = (ch, p)
            y[u] = _bdot(q_c[u], state[p], _NT) + y_loc[u]
            state[p] = _bdot(state[p], m_c[u]) + n_c[u]
    for p in range(pairs):
        s_ref[p] = state[p]

    inv_n = 1.0 / n
    mu = {u: _bdot(y[u], ones_bd) * inv_n for u in units}
    var = {u: _bdot(jnp.square(y[u] - mu[u]), ones_bd) * inv_n for u in units}
    for (ch, p) in units:
        u = (ch, p)
        rs, ls = slice(ch * c, (ch + 1) * c), slice(p * pw, (p + 1) * pw)
        yn = (y[u] - mu[u]) * lax.rsqrt(var[u] + RWKV_GN_EPS) * lnx_g[:, ls] + lnx_b[:, ls]
        bonus = _bdot(r_all[rs, ls] * kp_all[rs, ls] * r_k[:, ls], ones_bd) * vv[u]
        o_ref[rs, ls] = ((yn + bonus) * g_all[rs, ls]).astype(o_ref.dtype)


def _rwkv_scan(x2d, w_r, mix, wl, par, *, batch, seq_len):
    c = RWKV_CHUNK
    chunks = RWKV_SCAN_CHUNKS
    rows = chunks * c
    dm, zc = w_r.shape
    assert seq_len % rows == 0
    const = lambda a: pl.BlockSpec(a.shape, lambda bi, ci: (0, 0), pipeline_mode=pl.Buffered(1))
    out = pl.pallas_call(
        functools.partial(_rwkv_chunk_kernel, chunks=chunks),
        grid=(batch, seq_len // rows),
        in_specs=[pl.BlockSpec((None, rows, dm), lambda bi, ci: (bi, ci, 0)),
                  const(w_r), const(mix), const(wl), const(par)],
        out_specs=pl.BlockSpec((None, rows, RWKV_DIM), lambda bi, ci: (bi, ci, 0)),
        out_shape=jax.ShapeDtypeStruct((batch, seq_len, RWKV_DIM), BF16),
        scratch_shapes=[pltpu.VMEM((RWKV_HEADS // 2, 2 * RWKV_HEAD_DIM, 2 * RWKV_HEAD_DIM), F32),
                        pltpu.VMEM((SUBLANES, zc), F32)],
        compiler_params=_params("parallel", "arbitrary"),
        name="rwkv_scan",
    )(x2d.reshape(batch, seq_len, dm), w_r, mix, wl, par)
    return out.reshape(batch * seq_len, RWKV_DIM)


def _moba_kernel(q_ref, k_ref, v_ref, pool_ref, o_ref, kext_ref, vext_ref, s_ref):
    blk, dh = MOBA_BLOCK, MOBA_HEAD_DIM
    t_len = k_ref.shape[0]
    nb = t_len // blk
    nbp = -(-nb // SUBLANES) * SUBLANES
    shift = blk.bit_length() - 1
    key_tile = MOBA_Q_BLOCKS * blk

    lane = lax.broadcasted_iota(jnp.int32, (blk, LANES), 1)
    for j in range(nb):
        rs = slice(j * blk, (j + 1) * blk)
        kext_ref[rs, 0:dh] = k_ref[rs, :]
        kext_ref[rs, dh:dh + LANES] = jnp.where(lane == j, 1.0, 0.0).astype(BF16)
        vext_ref[rs, 0:dh] = v_ref[rs, :]
        vext_ref[rs, dh:dh + LANES] = jnp.ones((blk, LANES), BF16)

    km = jnp.dot(pool_ref[...], k_ref[...], preferred_element_type=F32)[0:nbp]
    km_hi = km.astype(BF16)
    km_lo = (km - km_hi.astype(F32)).astype(BF16)
    q_all = q_ref[...]
    gate = (lax.dot_general(km_hi, q_all, _NT, preferred_element_type=F32)
            + lax.dot_general(km_lo, q_all, _NT, preferred_element_type=F32))
    bidx = lax.broadcasted_iota(jnp.int32, gate.shape, 0)
    qblk = lax.broadcasted_iota(jnp.int32, gate.shape, 1) >> shift
    past = bidx < qblk
    gate = jnp.where(past, gate, NEG_INF)
    rank = jnp.zeros(gate.shape, jnp.int32)
    for jp in range(nb):
        other = gate[jp:jp + 1, :]
        rank = rank + ((other > gate) | ((other == gate) & (jp < bidx))).astype(jnp.int32)
    allowed = (past & (rank < MOBA_TOPK)) | (bidx == qblk)
    bias_t = jnp.where(allowed, 0.0, NEG_INF).astype(BF16)
    bias_t = jnp.concatenate([bias_t, jnp.zeros((LANES - nbp, t_len), BF16)], axis=0)
    r2 = lax.broadcasted_iota(jnp.int32, (LANES, LANES), 0)
    c2 = lax.broadcasted_iota(jnp.int32, (LANES, LANES), 1)
    eye = jnp.where(r2 == c2, 1.0, 0.0).astype(BF16)
    q_rows = s_ref.shape[0]
    assert q_rows == key_tile and t_len % q_rows == 0
    row = lax.broadcasted_iota(jnp.int32, (q_rows, key_tile), 0)
    col = lax.broadcasted_iota(jnp.int32, (q_rows, key_tile), 1)
    causal = col <= row
    colmax = lambda x: functools.reduce(jnp.maximum, [x[:, a:a + LANES] for a in range(0, x.shape[1], LANES)])

    for i in range(t_len // q_rows):
        qs = slice(i * q_rows, (i + 1) * q_rows)
        bias = lax.dot_general(bias_t[:, qs], eye, _TN, preferred_element_type=F32)
        q_ext = jnp.concatenate([q_ref[qs, :], bias.astype(BF16)], axis=1)
        nk = (i + 1) * q_rows
        tiles = [(a, a + key_tile) for a in range(0, nk, key_tile)]
        m_lane = None
        for a, e in tiles:
            s = lax.dot_general(q_ext, kext_ref[a:e, :], _NT, preferred_element_type=F32)
            if e == nk:
                s = jnp.where(causal, s, NEG_INF)
            s_ref[:, a:e] = s
            part = colmax(s)
            m_lane = part if m_lane is None else jnp.maximum(m_lane, part)
        m = jnp.max(m_lane, axis=-1, keepdims=True)
        acc = None
        for a, e in tiles:
            p = jnp.exp((s_ref[:, a:e] - m).astype(BF16))
            pv = jnp.dot(p, vext_ref[a:e, :], preferred_element_type=F32)
            acc = pv if acc is None else acc + pv
        o_ref[qs, :] = (acc[:, :dh] / acc[:, dh:]).astype(o_ref.dtype)


def _moba(qkv, *, batch, seq_len):
    blk, dh, nh = MOBA_BLOCK, MOBA_HEAD_DIM, MOBA_HEADS
    assert seq_len % blk == 0 and seq_len // blk <= LANES and dh == LANES
    pool = (jnp.arange(LANES)[:, None] == (jnp.arange(seq_len) // blk)[None, :]).astype(BF16) / blk
    qkv3 = qkv.reshape(batch, seq_len, 3 * MOBA_WIDTH)
    head = lambda off: pl.BlockSpec((None, seq_len, dh), lambda b, h: (b, 0, off + h))
    out = pl.pallas_call(
        _moba_kernel,
        grid=(batch, nh),
        in_specs=[head(0), head(nh), head(2 * nh), pl.BlockSpec((LANES, seq_len), lambda b, h: (0, 0))],
        out_specs=head(0),
        out_shape=jax.ShapeDtypeStruct((batch, seq_len, MOBA_WIDTH), BF16),
        scratch_shapes=[pltpu.VMEM((seq_len, dh + LANES), BF16), pltpu.VMEM((seq_len, dh + LANES), BF16),
                        pltpu.VMEM((MOBA_Q_BLOCKS * blk, seq_len), F32)],
        compiler_params=_params("parallel", "parallel"),
        name="moba",
    )(qkv3, qkv3, qkv3, pool)
    return out.reshape(batch * seq_len, MOBA_WIDTH)


def _residue_perm(d):
    p = DIL_PERM_ROWS
    idx = jnp.arange(p)
    nat = (idx % (p // d)) * d + idx // (p // d)
    return (nat[:, None] == idx[None, :]).astype(BF16)


def _dilated_proj_kernel(x_ref, w_ref, tab_ref, *rest, dilations):
    n = len(dilations)
    p_refs, o_refs, xb_ref = rest[:n], rest[n:2 * n], rest[2 * n]
    j = pl.program_id(1)

    @pl.when(j == 0)
    def _():
        xb_ref[...] = x_ref[...].astype(BF16)

    tm = x_ref.shape[0]
    grp = DIL_PERM_ROWS
    nat_ref = o_refs[0]
    for r0 in range(0, tm, grp):
        rs = slice(r0, r0 + grp)
        acc = jnp.dot(xb_ref[rs, :], w_ref[...], preferred_element_type=F32)
        t_z, t_partner = tab_ref[0, rs, :], tab_ref[1, rs, :]
        for hb in range(acc.shape[1] // LANES):
            z = acc[:, hb * LANES:(hb + 1) * LANES]
            nat_ref[0, rs, hb * LANES:(hb + 1) * LANES] = (
                z * t_z + pltpu.roll(z, LANES // 2, 1) * t_partner).astype(nat_ref.dtype)

    for g in range(1, n):
        d = dilations[g]
        w = grp // d

        @pl.when((j == g) | (j >= n))
        def _(g=g, d=d, w=w):
            for gi in range(tm // grp):
                y = jnp.dot(p_refs[g][...], nat_ref[0, gi * grp:(gi + 1) * grp, :],
                            preferred_element_type=F32).astype(o_refs[g].dtype)
                for r in range(d):
                    o_refs[g][r, gi * w:(gi + 1) * w, :] = y[r * w:(r + 1) * w, :]


def _dilated_proj(x2d, w, tabs, dilations, *, batch, seq_len):
    m, k = x2d.shape
    n = len(dilations)
    tm = min(PROJ_ROWS, seq_len)
    assert dilations[0] == 1 and w.shape[1] == (n + 2) * C_WIDTH
    assert m % tm == 0 and seq_len % tm == 0 and tm % DIL_PERM_ROWS == 0
    assert all(DIL_PERM_ROWS % (d * 2 * SUBLANES) == 0 for d in dilations[1:])
    bps = seq_len // tm
    perms = [_residue_perm(d) for d in dilations]

    def tab_map(i, j):
        return ((j >= n).astype(jnp.int32) + (j > n).astype(jnp.int32), 0, i % bps, 0)

    def out_map(g):
        return lambda i, j: (i // bps, 0, i % bps, (j > g).astype(jnp.int32) + (j > n).astype(jnp.int32))

    outs = pl.pallas_call(
        functools.partial(_dilated_proj_kernel, dilations=tuple(dilations)),
        grid=(m // tm, n + 2),
        in_specs=[pl.BlockSpec((tm, k), lambda i, j: (i, 0)),
                  pl.BlockSpec((k, C_WIDTH), lambda i, j: (0, j)),
                  pl.BlockSpec((None, 2, tm, LANES), tab_map)]
                 + [pl.BlockSpec(p.shape, lambda i, j: (0, 0)) for p in perms],
        out_specs=[pl.BlockSpec((None, d, tm // d, C_WIDTH), out_map(g)) for g, d in enumerate(dilations)],
        out_shape=[jax.ShapeDtypeStruct((batch, d, seq_len // d, 3 * C_WIDTH), BF16) for d in dilations],
        scratch_shapes=[pltpu.VMEM((tm, k), BF16)],
        compiler_params=_params("parallel", "arbitrary"),
        name="dilated_proj",
    )(x2d, w, tabs, *perms)
    return outs


def _dilated_kernel(q_ref, k_ref, kp_ref, v_ref, vp_ref, o_ref, lse_ref):
    w, dh = DIL_WINDOW, DIL_HEAD_DIM
    n_res, tiles = q_ref.shape[0], q_ref.shape[1] // w
    has_prev = pl.program_id(2) > 0
    row = lax.broadcasted_iota(jnp.int32, (w, 2 * w), 0)
    col = lax.broadcasted_iota(jnp.int32, (w, 2 * w), 1)
    band = (col >= row) & (col <= row + w)
    first_band = band & (has_prev | (col >= w))
    er = lax.broadcasted_iota(jnp.int32, (w, w), 0)
    ec = lax.broadcasted_iota(jnp.int32, (w, w), 1)
    eye = er == ec
    units = [(r, h, t) for r in range(n_res) for h in range(DIL_HEADS) for t in range(tiles)]

    def banded(ref, halo_ref, r, h, t):
        hs = slice(h * dh, (h + 1) * dh)
        if t == 0:
            return jnp.concatenate([halo_ref[r, :, hs], ref[r, 0:w, hs]], axis=0)
        return ref[r, (t - 1) * w:(t + 1) * w, hs]

    s = {}
    for r, h, t in units:
        q = q_ref[r, t * w:(t + 1) * w, h * dh:(h + 1) * dh]
        sc = lax.dot_general(q, banded(k_ref, kp_ref, r, h, t), _NT, preferred_element_type=F32)
        s[r, h, t] = jnp.where(first_band if t == 0 else band, sc, NEG_INF)
    m = {u: jnp.max(jnp.maximum(s[u][:, :w], s[u][:, w:]), axis=-1, keepdims=True) for u in units}
    p = {u: jnp.exp((s[u] - m[u]).astype(BF16)) for u in units}
    ones = jnp.ones((2 * w, dh), BF16)
    ol = {u: jnp.dot(p[u], jnp.concatenate([banded(v_ref, vp_ref, *u), ones], axis=1),
                     preferred_element_type=F32) for u in units}
    for r, h, t in units:
        o, l = ol[r, h, t][:, :dh], ol[r, h, t][:, dh:]
        o_ref[r, t * w:(t + 1) * w, h * dh:(h + 1) * dh] = (o / l).astype(o_ref.dtype)
        lse = m[r, h, t] + jnp.log(l)
        lse_ref[r, t, h:h + 1, :] = jnp.sum(jnp.where(eye, lse, 0.0), axis=0, keepdims=True)


def _dilated_group(src, cols, *, seq_len):
    batch, d, length, _ = src.shape
    w = DIL_WINDOW
    rows = min(DIL_ROWS, length)
    n_res = min(d, DIL_ROWS // rows)
    assert length % rows == 0 and rows % w == 0 and length * d == seq_len and d % n_res == 0
    tiles = rows // w
    qc, kc, vc = cols
    blk = lambda cb: pl.BlockSpec((None, n_res, rows, C_WIDTH), lambda b, r, i: (b, r, i, cb))
    halo = lambda cb: pl.BlockSpec(
        (None, n_res, w, C_WIDTH), lambda b, r, i: (b, r, jnp.maximum(i * tiles - 1, 0), cb))
    o, lse = pl.pallas_call(
        _dilated_kernel,
        grid=(batch, d // n_res, length // rows),
        in_specs=[blk(qc), blk(kc), halo(kc), blk(vc), halo(vc)],
        out_specs=[
            pl.BlockSpec((None, n_res, rows, C_WIDTH), lambda b, r, i: (b, r, i, 0)),
            pl.BlockSpec((None, n_res, tiles, DIL_HEADS, w), lambda b, r, i: (b, r, i, 0, 0)),
        ],
        out_shape=[
            jax.ShapeDtypeStruct((batch, d, length, C_WIDTH), BF16),
            jax.ShapeDtypeStruct((batch, d, length // w, DIL_HEADS, w), F32),
        ],
        compiler_params=_params("parallel", "parallel", "arbitrary"),
        name=f"dilated_d{d}",
    )(src, src, src, src, src)
    lse = lse.transpose(0, 2, 4, 1, 3).reshape(batch, seq_len, DIL_HEADS)
    return o, lse


def _mlp_ln(x1_ref, w1_ref, w2_ref, ln2_g, ln2_b):
    xb = x1_ref[...].astype(BF16)
    hs = []
    for c0 in range(0, w1_ref.shape[1], MLP_HIDDEN_CHUNK):
        h = jnp.dot(xb, w1_ref[:, c0:c0 + MLP_HIDDEN_CHUNK], preferred_element_type=F32)
        hs.append(jnp.square(jnp.maximum(h, 0.0)).astype(BF16))
    y = jnp.dot(jnp.concatenate(hs, axis=1), w2_ref[...], preferred_element_type=F32)
    return _layer_norm(DEEPNORM_ALPHA * x1_ref[...] + y, ln2_g, ln2_b)


def _outproj_mlp_kernel(*refs, n_act):
    acts = refs[:n_act]
    w_ref, x_ref, g1_ref, b1_ref, w1_ref, w2_ref, g2_ref, b2_ref, o_ref, x1_ref = refs[n_act:]

    @pl.when(pl.program_id(0) == 0)
    def _():
        x1_ref[...] = jnp.zeros_like(x1_ref)

    o_ref[...] = _mlp_ln(x1_ref, w1_ref, w2_ref, g2_ref[...], b2_ref[...])
    proj = None
    off = 0
    for a_ref in acts:
        kw = a_ref.shape[1]
        part = jnp.dot(a_ref[...], w_ref[off:off + kw, :], preferred_element_type=F32)
        proj = part if proj is None else proj + part
        off += kw
    x1_ref[...] = _layer_norm(DEEPNORM_ALPHA * x_ref[...] + proj, g1_ref[...], b1_ref[...])


def _outproj_mlp(acts, w, x2d, ln1, mlp, ln2):
    m, dm = x2d.shape
    tm = LAYER_ROWS
    assert m % tm == 0 and mlp[0].shape[1] % MLP_HIDDEN_CHUNK == 0
    last = m // tm - 1
    rowspec = lambda a: pl.BlockSpec((tm, a.shape[1]), lambda s: (jnp.minimum(s, last), 0))
    const = lambda a: pl.BlockSpec(a.shape, lambda s: (0, 0), pipeline_mode=pl.Buffered(1))
    return pl.pallas_call(
        functools.partial(_outproj_mlp_kernel, n_act=len(acts)),
        grid=(m // tm + 1,),
        in_specs=([rowspec(a) for a in acts] + [const(w), rowspec(x2d)]
                  + [const(a) for a in (*ln1, *mlp, *ln2)]),
        out_specs=pl.BlockSpec((tm, dm), lambda s: (jnp.maximum(s - 1, 0), 0)),
        out_shape=jax.ShapeDtypeStruct((m, dm), F32),
        scratch_shapes=[pltpu.VMEM((tm, dm), F32)],
        compiler_params=_params("arbitrary"),
        name="outproj_mlp",
    )(*acts, w, x2d, *ln1, *mlp, *ln2)


def _merge_outproj_mlp_kernel(*refs, dilations):
    n = len(dilations)
    o_refs, lse_refs, p_refs = refs[:n], refs[n:2 * n], refs[2 * n:3 * n]
    w_ref, x_ref, g1_ref, b1_ref, w1_ref, w2_ref, g2_ref, b2_ref, out_ref, x1_ref = refs[3 * n:]
    nh, dh, grp = DIL_HEADS, DIL_HEAD_DIM, DIL_PERM_ROWS
    tm = x_ref.shape[0]

    @pl.when(pl.program_id(0) == 0)
    def _():
        x1_ref[...] = jnp.zeros_like(x1_ref)

    out_ref[...] = _mlp_ln(x1_ref, w1_ref, w2_ref, g2_ref[...], b2_ref[...])
    branch = []
    for o_ref, p_ref, d in zip(o_refs, p_refs, dilations):
        if d == 1:
            branch.append(o_ref[0].astype(F32))
            continue
        w = grp // d
        nat = []
        for gi in range(tm // grp):
            perm = jnp.concatenate([o_ref[r, gi * w:(gi + 1) * w, :] for r in range(d)], axis=0)
            nat.append(lax.dot_general(p_ref[...], perm, _TN, preferred_element_type=F32))
        branch.append(jnp.concatenate(nat, axis=0))
    parts = [ref[...] for ref in lse_refs]
    m = functools.reduce(jnp.maximum, parts)
    es = [jnp.exp(p - m) for p in parts]
    den = functools.reduce(jnp.add, es)
    wts = [e / den for e in es]
    heads = []
    for h in range(nh):
        hs = slice(h * dh, (h + 1) * dh)
        heads.append(functools.reduce(jnp.add, [wts[g][:, h:h + 1] * branch[g][:, hs] for g in range(n)]))
    proj = _bdot(jnp.concatenate(heads, axis=1), w_ref[...])
    x1_ref[...] = _layer_norm(DEEPNORM_ALPHA * x_ref[...] + proj, g1_ref[...], b1_ref[...])


def _merge_outproj_mlp(outs, lses, dilations, w, x2d, ln1, mlp, ln2, *, batch, seq_len):
    dm = x2d.shape[1]
    tm = LAYER_ROWS
    assert seq_len % tm == 0 and tm % DIL_PERM_ROWS == 0 and mlp[0].shape[1] % MLP_HIDDEN_CHUNK == 0
    x3 = x2d.reshape(batch, seq_len, dm)
    perms = [_residue_perm(d) for d in dilations]
    bps = seq_len // tm
    n_tiles = batch * bps
    tile = lambda s: jnp.minimum(s, n_tiles - 1)
    done = lambda s: jnp.maximum(s - 1, 0)
    const = lambda a: pl.BlockSpec(a.shape, lambda s: (0,) * a.ndim, pipeline_mode=pl.Buffered(1))
    rows3 = lambda a: pl.BlockSpec((None, tm, a.shape[2]), lambda s: (tile(s) // bps, tile(s) % bps, 0))
    in_specs = ([pl.BlockSpec((None, d, tm // d, C_WIDTH), lambda s: (tile(s) // bps, 0, tile(s) % bps, 0))
                 for d in dilations]
                + [rows3(a) for a in lses] + [const(p) for p in perms] + [const(w), rows3(x3)]
                + [const(a) for a in (*ln1, *mlp, *ln2)])
    out = pl.pallas_call(
        functools.partial(_merge_outproj_mlp_kernel, dilations=tuple(dilations)),
        grid=(n_tiles + 1,),
        in_specs=in_specs,
        out_specs=pl.BlockSpec((None, tm, dm), lambda s: (done(s) // bps, done(s) % bps, 0)),
        out_shape=jax.ShapeDtypeStruct((batch, seq_len, dm), F32),
        scratch_shapes=[pltpu.VMEM((tm, dm), F32)],
        compiler_params=_params("arbitrary"),
        name="merge_outproj_mlp",
    )(*outs, *lses, *perms, w, x3, *ln1, *mlp, *ln2)
    return out.reshape(batch * seq_len, dm)


def _rwkv_moba_layer(x2d, w_in, shift_mix, w0, w_up, a0, a_up, g_up, k_k, k_a, r_k, lnx_g, lnx_b,
                     w_out, ln1, mlp, ln2, *, batch, seq_len):
    d = RWKV_DIM
    tabs = _rotary_tables(seq_len, MOBA_HEAD_DIM ** -0.5)
    w_m = w_in[:, RWKV_PROJ_COLS:].astype(BF16)
    w_m = jnp.concatenate([_rotary_weight(w_m[:, :2 * MOBA_WIDTH], 2 * MOBA_HEADS),
                           w_m[:, 2 * MOBA_WIDTH:]], axis=1)
    kinds = [0] * MOBA_HEADS + [1] * MOBA_HEADS + [None] * MOBA_HEADS
    qkv = _rotary_proj(x2d, w_m, tabs, kinds, seq_len=seq_len)
    y_b = _moba(qkv, batch=batch, seq_len=seq_len)
    pad = RWKV_Z_COLS - RWKV_PROJ_COLS
    w_r = jnp.pad(w_in[:, :RWKV_PROJ_COLS], ((0, 0), (0, pad))).astype(BF16)
    mix = jnp.pad(shift_mix, (0, pad))[None, :]
    wl = jnp.zeros((LORA_PAD, 3 * d), F32)
    wl = wl.at[0:DECAY_LORA, 0:d].set(w_up)
    wl = wl.at[DECAY_LORA:DECAY_LORA + ICLR_LORA, d:2 * d].set(a_up)
    wl = wl.at[DECAY_LORA + ICLR_LORA:LORA_COLS, 2 * d:3 * d].set(g_up).astype(BF16)
    par = jnp.stack([w0, a0, k_k, k_a, r_k.reshape(d), lnx_g, lnx_b, jnp.zeros((d,), F32)])
    y_a = _rwkv_scan(x2d, w_r, mix, wl, par, batch=batch, seq_len=seq_len)
    return _outproj_mlp([y_a, y_b], w_out.astype(BF16), x2d, ln1, mlp, ln2)


def _dilated_layer(x2d, w_in, w_out, ln1, mlp, ln2, *, batch, seq_len):
    tabs = _rotary_tables(seq_len, DIL_HEAD_DIM ** -0.5)
    nqk = (DIL_GROUPS + 1) * C_WIDTH
    w_c = w_in.astype(BF16)
    w_c = jnp.concatenate([_rotary_weight(w_c[:, :nqk], (DIL_GROUPS + 1) * DIL_HEADS), w_c[:, nqk:]], axis=1)
    dilations = [dilation for _, dilation in DIL_PAIRS]
    assert all(span // dilation == DIL_WINDOW for span, dilation in DIL_PAIRS)
    srcs = _dilated_proj(x2d, w_c, tabs, dilations, batch=batch, seq_len=seq_len)
    outs, lses = [], []
    for src in srcs:
        o, lse = _dilated_group(src, (0, 1, 2), seq_len=seq_len)
        outs.append(o)
        lses.append(lse)
    return _merge_outproj_mlp(outs, lses, dilations, w_out.astype(BF16), x2d, ln1, mlp, ln2,
                              batch=batch, seq_len=seq_len)


def kernel(x, ab_w_in, ab_shift_mix, ab_w0, ab_w_up, ab_a0, ab_a_up, ab_g_up, ab_k_k, ab_k_a, ab_r_k,
           ab_lnx_g, ab_lnx_b, ab_w_out, c_w_in, c_w_out, ln1_g, ln1_b, mlp_w1, mlp_w2, ln2_g, ln2_b):
    batch, seq_len, dm = x.shape
    assert dm == D_MODEL
    h = x.reshape(batch * seq_len, dm)
    depth = ln1_g.shape[0]
    for layer in range(depth):
        j = layer // 2
        ln1 = (ln1_g[layer][None, :], ln1_b[layer][None, :])
        ln2 = (ln2_g[layer][None, :], ln2_b[layer][None, :])
        mlp = (mlp_w1[layer].astype(BF16), mlp_w2[layer].astype(BF16))
        if layer % 2 == 0:
            h = _rwkv_moba_layer(h, ab_w_in[j], ab_shift_mix[j], ab_w0[j], ab_w_up[j], ab_a0[j], ab_a_up[j],
                                 ab_g_up[j], ab_k_k[j], ab_k_a[j], ab_r_k[j], ab_lnx_g[j], ab_lnx_b[j],
                                 ab_w_out[j], ln1, mlp, ln2, batch=batch, seq_len=seq_len)
        else:
            h = _dilated_layer(h, c_w_in[j], c_w_out[j], ln1, mlp, ln2, batch=batch, seq_len=seq_len)
    return h.reshape(batch, seq_len, dm)
```

```python
import functools

import jax
import jax.numpy as jnp
from jax import lax
from jax.experimental import pallas as pl
from jax.experimental.pallas import tpu as pltpu

F32 = jnp.float32
BF16 = jnp.bfloat16

D_MODEL = 1024
DEPTH = 2
DEEPNORM_ALPHA = (2 * DEPTH) ** 0.25
LN_EPS = 1e-5
ROPE_THETA = 500000.0
ROPE_FRACTION = 4
RWKV_HEAD_DIM = 64
RWKV_DIM = D_MODEL // 2
RWKV_HEADS = RWKV_DIM // RWKV_HEAD_DIM
DECAY_LORA = 64
ICLR_LORA = 64
GATE_LORA = 160
RWKV_GN_EPS = 64e-5
RWKV_PROJ_COLS = 3 * RWKV_DIM + DECAY_LORA + ICLR_LORA + GATE_LORA
MOBA_HEAD_DIM = 128
MOBA_WIDTH = D_MODEL - RWKV_DIM
MOBA_HEADS = MOBA_WIDTH // MOBA_HEAD_DIM
MOBA_BLOCK = 256
MOBA_TOPK = 3
DIL_PAIRS = ((128, 1), (512, 4), (2048, 16))
DIL_GROUPS = len(DIL_PAIRS)
DIL_HEAD_DIM = 128
DIL_HEADS = D_MODEL // DIL_HEAD_DIM
C_WIDTH = DIL_HEADS * DIL_HEAD_DIM
C_IN_COLS = DIL_GROUPS * C_WIDTH + 2 * C_WIDTH
MLP_HIDDEN = 4 * D_MODEL
NEG_INF = -1e30

LANES = 128
SUBLANES = 8
VMEM_LIMIT_BYTES = 56 * 1024 * 1024

LORA_COLS = DECAY_LORA + ICLR_LORA + GATE_LORA
LORA_PAD = -(-LORA_COLS // LANES) * LANES
RWKV_Z_COLS = 3 * RWKV_DIM + LORA_PAD
MOBA_Q_BLOCKS = 2
RWKV_CHUNK = 64
RWKV_SCAN_CHUNKS = 8
DIL_WINDOW = 128
DIL_ROWS = 1024
DIL_PERM_ROWS = 256
PROJ_ROWS = 1024
PROJ_ROW_CHUNK = 256
LAYER_ROWS = 512
MLP_HIDDEN_CHUNK = 1024

_NN = (((1,), (0,)), ((), ()))
_NT = (((1,), (1,)), ((), ()))
_TN = (((0,), (0,)), ((), ()))


def _bdot(a, b, dims=_NN):
    return lax.dot_general(a.astype(BF16), b.astype(BF16), dims, preferred_element_type=F32)


def _layer_norm(y, g, b):
    mu = jnp.mean(y, axis=-1, keepdims=True)
    var = jnp.mean(jnp.square(y - mu), axis=-1, keepdims=True)
    return (y - mu) * lax.rsqrt(var + LN_EPS) * g + b


def _params(*sem, fuse_inputs=None):
    return pltpu.CompilerParams(dimension_semantics=sem, vmem_limit_bytes=VMEM_LIMIT_BYTES,
                                allow_input_fusion=fuse_inputs)


ROT_DIMS = MOBA_HEAD_DIM // ROPE_FRACTION
ROT_HALF = ROT_DIMS // 2


def _rotary_weight(w, n_heads):
    split = ROT_DIMS + LANES // 2 - ROT_HALF
    w3 = w.reshape(w.shape[0], n_heads, LANES)
    w3 = jnp.concatenate([w3[..., :ROT_HALF], w3[..., ROT_DIMS:split], w3[..., ROT_HALF:ROT_DIMS],
                          w3[..., split:]], axis=-1)
    return w3.reshape(w.shape[0], n_heads * LANES)


def _rotary_tables(seq_len, q_scale):
    inv_freq = ROPE_THETA ** (-jnp.arange(ROT_HALF, dtype=F32) / ROT_HALF)
    ang = jnp.arange(seq_len).astype(F32)[:, None] * inv_freq[None, :]
    cos, sin = jnp.cos(ang), jnp.sin(ang)
    pad = lambda a, fill: jnp.concatenate([a, jnp.full((seq_len, LANES // 2 - ROT_HALF), fill, F32)], axis=1)
    c = jnp.concatenate([pad(cos, 1.0), pad(cos, 1.0)], axis=1)
    s = jnp.concatenate([pad(-sin, 0.0), pad(sin, 0.0)], axis=1)
    one = jnp.ones_like(c)
    return jnp.stack([jnp.stack([c * q_scale, s * q_scale]), jnp.stack([c, s]),
                      jnp.stack([one, jnp.zeros_like(c)])])


def _rotary_proj_kernel(x_ref, w_ref, tab_ref, o_ref, *, head_kinds):
    tm = x_ref.shape[0]
    chunk = min(PROJ_ROW_CHUNK, tm)
    for r0 in range(0, tm, chunk):
        rs = slice(r0, r0 + chunk)
        acc = jnp.dot(x_ref[rs, :].astype(BF16), w_ref[...], preferred_element_type=F32)
        for hb, kind in enumerate(head_kinds):
            z = acc[:, hb * LANES:(hb + 1) * LANES]
            if kind is not None:
                z = z * tab_ref[kind, 0, rs, :] + pltpu.roll(z, LANES // 2, 1) * tab_ref[kind, 1, rs, :]
            o_ref[rs, hb * LANES:(hb + 1) * LANES] = z.astype(o_ref.dtype)


def _rotary_proj(x2d, w, tabs, head_kinds, *, seq_len):
    m, k = x2d.shape
    n = w.shape[1]
    tm = min(PROJ_ROWS, seq_len)
    assert m % tm == 0 and seq_len % tm == 0 and n == len(head_kinds) * LANES
    bps = seq_len // tm
    return pl.pallas_call(
        functools.partial(_rotary_proj_kernel, head_kinds=tuple(head_kinds)),
        grid=(m // tm,),
        in_specs=[pl.BlockSpec((tm, k), lambda i: (i, 0)),
                  pl.BlockSpec((k, n), lambda i: (0, 0), pipeline_mode=pl.Buffered(1)),
                  pl.BlockSpec((tabs.shape[0], 2, tm, LANES), lambda i: (0, 0, i % bps, 0))],
        out_specs=pl.BlockSpec((tm, n), lambda i: (i, 0)),
        out_shape=jax.ShapeDtypeStruct((m, n), BF16),
        compiler_params=_params("parallel", fuse_inputs=[False, True, False]),
        name="rotary_proj",
    )(x2d, w, tabs)


def _softplus(x):
    return jnp.maximum(x, 0.0) + jnp.log(1.0 + jnp.exp(-jnp.abs(x)))


def _split_dot(x, ones_bd):
    hi = x.astype(BF16)
    lo = (x - hi.astype(F32)).astype(BF16)
    return (jnp.dot(hi, ones_bd, preferred_element_type=F32)
            + jnp.dot(lo, ones_bd, preferred_element_type=F32))


def _rwkv_prep(z, prev, mix, wl, par, ones_bd):
    d = RWKV_DIM
    pw = 2 * RWKV_HEAD_DIM
    row = lax.broadcasted_iota(jnp.int32, z.shape, 0)
    shifted = jnp.where(row == 0, prev, pltpu.roll(z, 1, 0))
    zz = z + (shifted - z) * mix
    r, k, v, lo = zz[:, 0:d], zz[:, d:2 * d], zz[:, 2 * d:3 * d], zz[:, 3 * d:]
    lane = lax.broadcasted_iota(jnp.int32, lo.shape, 1)
    act = jnp.where(lane < DECAY_LORA, jnp.tanh(lo),
                    jnp.where(lane < DECAY_LORA + ICLR_LORA, lo, jax.nn.sigmoid(lo)))
    lora = _bdot(act, wl)
    w0, a0, k_k, k_a = par[0:1, :], par[1:2, :], par[2:3, :], par[3:4, :]
    w = -_softplus(-(w0 + lora[:, 0:d])) - 0.5
    a = jax.nn.sigmoid(a0 + lora[:, d:2 * d])
    kk = k * k_k
    sq = kk * kk
    ssq = jnp.concatenate([_split_dot(sq[:, p0:p0 + pw], ones_bd) for p0 in range(0, d, pw)], axis=1)
    kk = kk / jnp.maximum(jnp.sqrt(ssq), 1e-12)
    return r, -jnp.exp(w), k * (1.0 + (a - 1.0) * k_a), v, -kk, kk * a, lora[:, 2 * d:3 * d]


def _rwkv_chunk_kernel(x_ref, wr_ref, mix_ref, wl_ref, par_ref, o_ref, s_ref, prev_ref, *, chunks):
    c, n = RWKV_CHUNK, RWKV_HEAD_DIM
    pw = 2 * n
    assert c == n and pw == LANES
    pairs = RWKV_HEADS // 2
    units = [(ch, p) for ch in range(chunks) for p in range(pairs)]

    first = pl.program_id(1) == 0

    @pl.when(first)
    def _():
        s_ref[...] = jnp.zeros_like(s_ref)

    row = lax.broadcasted_iota(jnp.int32, (c, pw), 0)
    col = lax.broadcasted_iota(jnp.int32, (c, pw), 1) & (n - 1)
    tril_inc = col <= row
    tril_str = col < row
    eye = col == row
    level_masks = [((row >> (lvl + 1)) == (col >> (lvl + 1))) & (((row >> lvl) & 1) == 1) & (((col >> lvl) & 1) == 0)
                   for lvl in range(c.bit_length() - 1)]
    r2 = lax.broadcasted_iota(jnp.int32, (pw, pw), 0)
    c2 = lax.broadcasted_iota(jnp.int32, (pw, pw), 1)
    same_head = (r2 < n) == (c2 < n)
    eye2 = r2 == c2
    ones_bd = jnp.where(same_head, 1.0, 0.0).astype(BF16)

    def stack(x):
        xb = x.astype(BF16)
        return jnp.where(same_head, jnp.concatenate([xb, xb], axis=0), jnp.zeros((), BF16))

    rows = chunks * c
    z = jnp.dot(x_ref[...].astype(BF16), wr_ref[...], preferred_element_type=F32)
    prev = jnp.where(first, 0.0, prev_ref[SUBLANES - 1:SUBLANES, :])
    r_all, lw, kp_all, v_all, na_all, b_all, g_all = _rwkv_prep(
        z, prev, mix_ref[...], wl_ref[...], par_ref[...], ones_bd)
    prev_ref[...] = z[rows - SUBLANES:, :]

    tr = lax.broadcasted_iota(jnp.int32, (rows, rows), 0)
    tc = lax.broadcasted_iota(jnp.int32, (rows, rows), 1)
    chunk_tril = ((tr >> (c.bit_length() - 1)) == (tc >> (c.bit_length() - 1))) & (tc <= tr)
    chunk_tril = jnp.where(chunk_tril, 1.0, 0.0).astype(BF16)
    lw_hi = lw.astype(BF16)
    lw_r1 = lw - lw_hi.astype(F32)
    lw_mid = lw_r1.astype(BF16)
    lw_lo = (lw_r1 - lw_mid.astype(F32)).astype(BF16)
    cum = (jnp.dot(chunk_tril, lw_hi, preferred_element_type=F32)
           + jnp.dot(chunk_tril, lw_mid, preferred_element_type=F32)
           + jnp.dot(chunk_tril, lw_lo, preferred_element_type=F32))
    r_k, lnx_g, lnx_b = par_ref[4:5, :], par_ref[5:6, :], par_ref[6:7, :]

    at, rt, bt, kt, bh, kh, vv, p_end = {}, {}, {}, {}, {}, {}, {}, {}
    for ch in range(chunks):
        rs = slice(ch * c, (ch + 1) * c)
        cum_c = cum[rs]
        cum_end = cum_c[c - 1:c, :]
        e_inv = jnp.exp(-cum_c)
        e_end = jnp.exp(cum_end - cum_c)
        at_c = na_all[rs] * jnp.exp(cum_c - lw[rs])
        rt_c = r_all[rs] * jnp.exp(cum_c)
        b_c, kp_c = b_all[rs], kp_all[rs]
        bt_c, kt_c, bh_c, kh_c = b_c * e_inv, kp_c * e_inv, b_c * e_end, kp_c * e_end
        pe_c = jnp.exp(cum_end)
        for p in range(pairs):
            ls = slice(p * pw, (p + 1) * pw)
            u = (ch, p)
            at[u], rt[u], bt[u], kt[u], bh[u], kh[u] = at_c[:, ls], rt_c[:, ls], bt_c[:, ls], kt_c[:, ls], bh_c[:, ls], kh_c[:, ls]
            vv[u], p_end[u] = v_all[rs, ls], pe_c[:, ls]

    ar = {u: jnp.concatenate([at[u], rt[u]], axis=0) for u in units}
    g = {u: _bdot(ar[u], jnp.concatenate([stack(bt[u]), stack(kt[u])], axis=0), _NT)
         for u in units}
    a_ab = {u: jnp.where(tril_str, g[u][:c, :pw], 0.0) for u in units}
    a_rb = {u: jnp.where(tril_inc, g[u][c:, :pw], 0.0) for u in units}
    a_ak = {u: jnp.where(tril_str, g[u][:c, pw:], 0.0) for u in units}
    a_rk = {u: jnp.where(tril_inc, g[u][c:, pw:], 0.0) for u in units}
    av = {u: _bdot(jnp.concatenate([a_ak[u], a_rk[u]], axis=0), stack(vv[u])) for u in units}
    ak_v = {u: av[u][:c] for u in units}
    inv = {u: jnp.where(eye, 1.0, jnp.where(level_masks[0], a_ab[u], 0.0)) for u in units}
    for mask in level_masks[1:]:
        e = {u: _bdot(jnp.where(mask, a_ab[u], 0.0), stack(inv[u])) for u in units}
        inv = {u: inv[u] + _bdot(inv[u], stack(e[u])) for u in units}
    wu = {u: _bdot(inv[u], jnp.concatenate([stack(at[u]), stack(ak_v[u])], axis=1)) for u in units}
    w_mat = {u: wu[u][:, :pw] for u in units}
    u_loc = {u: wu[u][:, pw:] for u in units}
    wub = {u: _bdot(wu[u], bh[u], _TN) for u in units}
    m_c = {u: jnp.where(same_head, wub[u][:pw], 0.0) + jnp.where(eye2, p_end[u], 0.0) for u in units}
    n_c = {u: jnp.where(same_head, wub[u][pw:] + _bdot(vv[u], kh[u], _TN), 0.0) for u in units}
    qy = {u: _bdot(a_rb[u], jnp.concatenate([stack(w_mat[u]), stack(u_loc[u])], axis=1)) for u in units}
    q_c = {u: rt[u] + qy[u][:, :pw] for u in units}
    y_loc = {u: qy[u][:, pw:] + av[u][c:] for u in units}

    state = [s_ref[p] for p in range(pairs)]
    y = {}
    for ch in range(chunks):
        for p in range(pairs):
            u = (ch, p)
            y[u] = _bdot(q_c[u], state[p], _NT) + y_loc[u]
            state[p] = _bdot(state[p], m_c[u]) + n_c[u]
    for p in range(pairs):
        s_ref[p] = state[p]

    inv_n = 1.0 / n
    mu = {u: _bdot(y[u], ones_bd) * inv_n for u in units}
    var = {u: _bdot(jnp.square(y[u] - mu[u]), ones_bd) * inv_n for u in units}
    for (ch, p) in units:
        u = (ch, p)
        rs, ls = slice(ch * c, (ch + 1) * c), slice(p * pw, (p + 1) * pw)
        yn = (y[u] - mu[u]) * lax.rsqrt(var[u] + RWKV_GN_EPS) * lnx_g[:, ls] + lnx_b[:, ls]
        bonus = _bdot(r_all[rs, ls] * kp_all[rs, ls] * r_k[:, ls], ones_bd) * vv[u]
        o_ref[rs, ls] = ((yn + bonus) * g_all[rs, ls]).astype(o_ref.dtype)


def _rwkv_scan(x2d, w_r, mix, wl, par, *, batch, seq_len):
    c = RWKV_CHUNK
    chunks = RWKV_SCAN_CHUNKS
    rows = chunks * c
    dm, zc = w_r.shape
    assert seq_len % rows == 0
    const = lambda a: pl.BlockSpec(a.shape, lambda bi, ci: (0, 0), pipeline_mode=pl.Buffered(1))
    out = pl.pallas_call(
        functools.partial(_rwkv_chunk_kernel, chunks=chunks),
        grid=(batch, seq_len // rows),
        in_specs=[pl.BlockSpec((None, rows, dm), lambda bi, ci: (bi, ci, 0)),
                  const(w_r), const(mix), const(wl), const(par)],
        out_specs=pl.BlockSpec((None, rows, RWKV_DIM), lambda bi, ci: (bi, ci, 0)),
        out_shape=jax.ShapeDtypeStruct((batch, seq_len, RWKV_DIM), BF16),
        scratch_shapes=[pltpu.VMEM((RWKV_HEADS // 2, 2 * RWKV_HEAD_DIM, 2 * RWKV_HEAD_DIM), F32),
                        pltpu.VMEM((SUBLANES, zc), F32)],
        compiler_params=_params("parallel", "arbitrary", fuse_inputs=[False, True, False, True, False]),
        name="rwkv_scan",
    )(x2d.reshape(batch, seq_len, dm), w_r, mix, wl, par)
    return out.reshape(batch * seq_len, RWKV_DIM)


def _moba_kernel(q_ref, k_ref, v_ref, pool_ref, o_ref, kext_ref, vext_ref, s_ref):
    blk, dh = MOBA_BLOCK, MOBA_HEAD_DIM
    t_len = k_ref.shape[0]
    nb = t_len // blk
    nbp = -(-nb // SUBLANES) * SUBLANES
    shift = blk.bit_length() - 1
    key_tile = MOBA_Q_BLOCKS * blk

    lane = lax.broadcasted_iota(jnp.int32, (blk, LANES), 1)
    for j in range(nb):
        rs = slice(j * blk, (j + 1) * blk)
        kext_ref[rs, 0:dh] = k_ref[rs, :]
        kext_ref[rs, dh:dh + LANES] = jnp.where(lane == j, 1.0, 0.0).astype(BF16)
        vext_ref[rs, 0:dh] = v_ref[rs, :]
        vext_ref[rs, dh:dh + LANES] = jnp.ones((blk, LANES), BF16)

    km = jnp.dot(pool_ref[...], k_ref[...], preferred_element_type=F32)[0:nbp]
    km_hi = km.astype(BF16)
    km_lo = (km - km_hi.astype(F32)).astype(BF16)
    q_all = q_ref[...]
    gate = (lax.dot_general(km_hi, q_all, _NT, preferred_element_type=F32)
            + lax.dot_general(km_lo, q_all, _NT, preferred_element_type=F32))
    bidx = lax.broadcasted_iota(jnp.int32, gate.shape, 0)
    qblk = lax.broadcasted_iota(jnp.int32, gate.shape, 1) >> shift
    past = bidx < qblk
    gate = jnp.where(past, gate, NEG_INF)
    rank = jnp.zeros(gate.shape, jnp.int32)
    for jp in range(nb):
        other = gate[jp:jp + 1, :]
        rank = rank + ((other > gate) | ((other == gate) & (jp < bidx))).astype(jnp.int32)
    allowed = (past & (rank < MOBA_TOPK)) | (bidx == qblk)
    bias_t = jnp.where(allowed, 0.0, NEG_INF).astype(BF16)
    bias_t = jnp.concatenate([bias_t, jnp.zeros((LANES - nbp, t_len), BF16)], axis=0)
    r2 = lax.broadcasted_iota(jnp.int32, (LANES, LANES), 0)
    c2 = lax.broadcasted_iota(jnp.int32, (LANES, LANES), 1)
    eye = jnp.where(r2 == c2, 1.0, 0.0).astype(BF16)
    q_rows = s_ref.shape[0]
    assert q_rows == key_tile and t_len % q_rows == 0
    row = lax.broadcasted_iota(jnp.int32, (q_rows, key_tile), 0)
    col = lax.broadcasted_iota(jnp.int32, (q_rows, key_tile), 1)
    causal = col <= row
    colmax = lambda x: functools.reduce(jnp.maximum, [x[:, a:a + LANES] for a in range(0, x.shape[1], LANES)])

    for i in range(t_len // q_rows):
        qs = slice(i * q_rows, (i + 1) * q_rows)
        bias = lax.dot_general(bias_t[:, qs], eye, _TN, preferred_element_type=F32)
        q_ext = jnp.concatenate([q_ref[qs, :], bias.astype(BF16)], axis=1)
        nk = (i + 1) * q_rows
        tiles = [(a, a + key_tile) for a in range(0, nk, key_tile)]
        m_lane = None
        for a, e in tiles:
            s = lax.dot_general(q_ext, kext_ref[a:e, :], _NT, preferred_element_type=F32)
            if e == nk:
                s = jnp.where(causal, s, NEG_INF)
            s_ref[:, a:e] = s
            part = colmax(s)
            m_lane = part if m_lane is None else jnp.maximum(m_lane, part)
        m = jnp.max(m_lane, axis=-1, keepdims=True)
        acc = None
        for a, e in tiles:
            p = jnp.exp((s_ref[:, a:e] - m).astype(BF16))
            pv = jnp.dot(p, vext_ref[a:e, :], preferred_element_type=F32)
            acc = pv if acc is None else acc + pv
        o_ref[qs, :] = (acc[:, :dh] / acc[:, dh:]).astype(o_ref.dtype)


def _moba(qkv, *, batch, seq_len):
    blk, dh, nh = MOBA_BLOCK, MOBA_HEAD_DIM, MOBA_HEADS
    assert seq_len % blk == 0 and seq_len // blk <= LANES and dh == LANES
    pool = (jnp.arange(LANES)[:, None] == (jnp.arange(seq_len) // blk)[None, :]).astype(BF16) / blk
    qkv3 = qkv.reshape(batch, seq_len, 3 * MOBA_WIDTH)
    head = lambda off: pl.BlockSpec((None, seq_len, dh), lambda b, h: (b, 0, off + h))
    out = pl.pallas_call(
        _moba_kernel,
        grid=(batch, nh),
        in_specs=[head(0), head(nh), head(2 * nh), pl.BlockSpec((LANES, seq_len), lambda b, h: (0, 0))],
        out_specs=head(0),
        out_shape=jax.ShapeDtypeStruct((batch, seq_len, MOBA_WIDTH), BF16),
        scratch_shapes=[pltpu.VMEM((seq_len, dh + LANES), BF16), pltpu.VMEM((seq_len, dh + LANES), BF16),
                        pltpu.VMEM((MOBA_Q_BLOCKS * blk, seq_len), F32)],
        compiler_params=_params("parallel", "parallel"),
        name="moba",
    )(qkv3, qkv3, qkv3, pool)
    return out.reshape(batch * seq_len, MOBA_WIDTH)


def _residue_perm(d):
    p = DIL_PERM_ROWS
    idx = jnp.arange(p)
    nat = (idx % (p // d)) * d + idx // (p // d)
    return (nat[:, None] == idx[None, :]).astype(BF16)


def _dilated_proj_kernel(x_ref, w_ref, tab_ref, *rest, dilations):
    n = len(dilations)
    p_refs, o_refs, xb_ref = rest[:n], rest[n:2 * n], rest[2 * n]
    j = pl.program_id(1)

    @pl.when(j == 0)
    def _():
        xb_ref[...] = x_ref[...].astype(BF16)

    tm = x_ref.shape[0]
    grp = DIL_PERM_ROWS
    nat_ref = o_refs[0]
    for r0 in range(0, tm, grp):
        rs = slice(r0, r0 + grp)
        acc = jnp.dot(xb_ref[rs, :], w_ref[...], preferred_element_type=F32)
        t_z, t_partner = tab_ref[0, rs, :], tab_ref[1, rs, :]
        for hb in range(acc.shape[1] // LANES):
            z = acc[:, hb * LANES:(hb + 1) * LANES]
            nat_ref[0, rs, hb * LANES:(hb + 1) * LANES] = (
                z * t_z + pltpu.roll(z, LANES // 2, 1) * t_partner).astype(nat_ref.dtype)

    for g in range(1, n):
        d = dilations[g]
        w = grp // d

        @pl.when((j == g) | (j >= n))
        def _(g=g, d=d, w=w):
            for gi in range(tm // grp):
                y = jnp.dot(p_refs[g][...], nat_ref[0, gi * grp:(gi + 1) * grp, :],
                            preferred_element_type=F32).astype(o_refs[g].dtype)
                for r in range(d):
                    o_refs[g][r, gi * w:(gi + 1) * w, :] = y[r * w:(r + 1) * w, :]


def _dilated_proj(x2d, w, tabs, dilations, *, batch, seq_len):
    m, k = x2d.shape
    n = len(dilations)
    tm = min(PROJ_ROWS, seq_len)
    assert dilations[0] == 1 and w.shape[1] == (n + 2) * C_WIDTH
    assert m % tm == 0 and seq_len % tm == 0 and tm % DIL_PERM_ROWS == 0
    assert all(DIL_PERM_ROWS % (d * 2 * SUBLANES) == 0 for d in dilations[1:])
    bps = seq_len // tm
    perms = [_residue_perm(d) for d in dilations]

    def tab_map(i, j):
        return ((j >= n).astype(jnp.int32) + (j > n).astype(jnp.int32), 0, i % bps, 0)

    def out_map(g):
        return lambda i, j: (i // bps, 0, i % bps, (j > g).astype(jnp.int32) + (j > n).astype(jnp.int32))

    outs = pl.pallas_call(
        functools.partial(_dilated_proj_kernel, dilations=tuple(dilations)),
        grid=(m // tm, n + 2),
        in_specs=[pl.BlockSpec((tm, k), lambda i, j: (i, 0)),
                  pl.BlockSpec((k, C_WIDTH), lambda i, j: (0, j)),
                  pl.BlockSpec((None, 2, tm, LANES), tab_map)]
                 + [pl.BlockSpec(p.shape, lambda i, j: (0, 0)) for p in perms],
        out_specs=[pl.BlockSpec((None, d, tm // d, C_WIDTH), out_map(g)) for g, d in enumerate(dilations)],
        out_shape=[jax.ShapeDtypeStruct((batch, d, seq_len // d, 3 * C_WIDTH), BF16) for d in dilations],
        scratch_shapes=[pltpu.VMEM((tm, k), BF16)],
        compiler_params=_params("parallel", "arbitrary"),
        name="dilated_proj",
    )(x2d, w, tabs, *perms)
    return outs


def _dilated_kernel(q_ref, k_ref, kp_ref, v_ref, vp_ref, o_ref, lse_ref):
    w, dh = DIL_WINDOW, DIL_HEAD_DIM
    n_res, tiles = q_ref.shape[0], q_ref.shape[1] // w
    has_prev = pl.program_id(2) > 0
    row = lax.broadcasted_iota(jnp.int32, (w, 2 * w), 0)
    col = lax.broadcasted_iota(jnp.int32, (w, 2 * w), 1)
    band = (col >= row) & (col <= row + w)
    first_band = band & (has_prev | (col >= w))
    er = lax.broadcasted_iota(jnp.int32, (w, w), 0)
    ec = lax.broadcasted_iota(jnp.int32, (w, w), 1)
    eye = er == ec
    units = [(r, h, t) for r in range(n_res) for h in range(DIL_HEADS) for t in range(tiles)]

    def banded(ref, halo_ref, r, h, t):
        hs = slice(h * dh, (h + 1) * dh)
        if t == 0:
            return jnp.concatenate([halo_ref[r, :, hs], ref[r, 0:w, hs]], axis=0)
        return ref[r, (t - 1) * w:(t + 1) * w, hs]

    s = {}
    for r, h, t in units:
        q = q_ref[r, t * w:(t + 1) * w, h * dh:(h + 1) * dh]
        sc = lax.dot_general(q, banded(k_ref, kp_ref, r, h, t), _NT, preferred_element_type=F32)
        s[r, h, t] = jnp.where(first_band if t == 0 else band, sc, NEG_INF)
    m = {u: jnp.max(jnp.maximum(s[u][:, :w], s[u][:, w:]), axis=-1, keepdims=True) for u in units}
    p = {u: jnp.exp((s[u] - m[u]).astype(BF16)) for u in units}
    ones = jnp.ones((2 * w, dh), BF16)
    ol = {u: jnp.dot(p[u], jnp.concatenate([banded(v_ref, vp_ref, *u), ones], axis=1),
                     preferred_element_type=F32) for u in units}
    for r, h, t in units:
        o, l = ol[r, h, t][:, :dh], ol[r, h, t][:, dh:]
        o_ref[r, t * w:(t + 1) * w, h * dh:(h + 1) * dh] = (o / l).astype(o_ref.dtype)
        lse = m[r, h, t] + jnp.log(l)
        lse_ref[r, t, h:h + 1, :] = jnp.sum(jnp.where(eye, lse, 0.0), axis=0, keepdims=True)


def _dilated_group(src, cols, *, seq_len):
    batch, d, length, _ = src.shape
    w = DIL_WINDOW
    rows = min(DIL_ROWS, length)
    n_res = min(d, DIL_ROWS // rows)
    assert length % rows == 0 and rows % w == 0 and length * d == seq_len and d % n_res == 0
    tiles = rows // w
    qc, kc, vc = cols
    blk = lambda cb: pl.BlockSpec((None, n_res, rows, C_WIDTH), lambda b, r, i: (b, r, i, cb))
    halo = lambda cb: pl.BlockSpec(
        (None, n_res, w, C_WIDTH), lambda b, r, i: (b, r, jnp.maximum(i * tiles - 1, 0), cb))
    o, lse = pl.pallas_call(
        _dilated_kernel,
        grid=(batch, d // n_res, length // rows),
        in_specs=[blk(qc), blk(kc), halo(kc), blk(vc), halo(vc)],
        out_specs=[
            pl.BlockSpec((None, n_res, rows, C_WIDTH), lambda b, r, i: (b, r, i, 0)),
            pl.BlockSpec((None, n_res, tiles, DIL_HEADS, w), lambda b, r, i: (b, r, i, 0, 0)),
        ],
        out_shape=[
            jax.ShapeDtypeStruct((batch, d, length, C_WIDTH), BF16),
            jax.ShapeDtypeStruct((batch, d, length // w, DIL_HEADS, w), F32),
        ],
        compiler_params=_params("parallel", "parallel", "arbitrary"),
        name=f"dilated_d{d}",
    )(src, src, src, src, src)
    lse = lse.transpose(0, 2, 4, 1, 3).reshape(batch, seq_len, DIL_HEADS)
    return o, lse


def _mlp_ln(x1_ref, w1_ref, w2_ref, ln2_g, ln2_b):
    xb = x1_ref[...].astype(BF16)
    hs = []
    for c0 in range(0, w1_ref.shape[1], MLP_HIDDEN_CHUNK):
        h = jnp.dot(xb, w1_ref[:, c0:c0 + MLP_HIDDEN_CHUNK], preferred_element_type=F32)
        hs.append(jnp.square(jnp.maximum(h, 0.0)).astype(BF16))
    y = jnp.dot(jnp.concatenate(hs, axis=1), w2_ref[...], preferred_element_type=F32)
    return _layer_norm(DEEPNORM_ALPHA * x1_ref[...] + y, ln2_g, ln2_b)


def _outproj_mlp_kernel(*refs, n_act):
    acts = refs[:n_act]
    w_ref, x_ref, g1_ref, b1_ref, w1_ref, w2_ref, g2_ref, b2_ref, o_ref, x1_ref = refs[n_act:]

    @pl.when(pl.program_id(0) == 0)
    def _():
        x1_ref[...] = jnp.zeros_like(x1_ref)

    o_ref[...] = _mlp_ln(x1_ref, w1_ref, w2_ref, g2_ref[...], b2_ref[...])
    proj = None
    off = 0
    for a_ref in acts:
        kw = a_ref.shape[1]
        part = jnp.dot(a_ref[...], w_ref[off:off + kw, :], preferred_element_type=F32)
        proj = part if proj is None else proj + part
        off += kw
    x1_ref[...] = _layer_norm(DEEPNORM_ALPHA * x_ref[...] + proj, g1_ref[...], b1_ref[...])


def _outproj_mlp(acts, w, x2d, ln1, mlp, ln2):
    m, dm = x2d.shape
    tm = LAYER_ROWS
    assert m % tm == 0 and mlp[0].shape[1] % MLP_HIDDEN_CHUNK == 0
    last = m // tm - 1
    rowspec = lambda a: pl.BlockSpec((tm, a.shape[1]), lambda s: (jnp.minimum(s, last), 0))
    const = lambda a: pl.BlockSpec(a.shape, lambda s: (0, 0), pipeline_mode=pl.Buffered(1))
    return pl.pallas_call(
        functools.partial(_outproj_mlp_kernel, n_act=len(acts)),
        grid=(m // tm + 1,),
        in_specs=([rowspec(a) for a in acts] + [const(w), rowspec(x2d)]
                  + [const(a) for a in (*ln1, *mlp, *ln2)]),
        out_specs=pl.BlockSpec((tm, dm), lambda s: (jnp.maximum(s - 1, 0), 0)),
        out_shape=jax.ShapeDtypeStruct((m, dm), F32),
        scratch_shapes=[pltpu.VMEM((tm, dm), F32)],
        compiler_params=_params("arbitrary", fuse_inputs=(
            [False] * len(acts) + [True, False, False, False, True, True, False, False])),
        name="outproj_mlp",
    )(*acts, w, x2d, *ln1, *mlp, *ln2)


def _merge_outproj_mlp_kernel(*refs, dilations):
    n = len(dilations)
    o_refs, lse_refs, p_refs = refs[:n], refs[n:2 * n], refs[2 * n:3 * n]
    w_ref, x_ref, g1_ref, b1_ref, w1_ref, w2_ref, g2_ref, b2_ref, out_ref, x1_ref = refs[3 * n:]
    nh, dh, grp = DIL_HEADS, DIL_HEAD_DIM, DIL_PERM_ROWS
    tm = x_ref.shape[0]

    @pl.when(pl.program_id(0) == 0)
    def _():
        x1_ref[...] = jnp.zeros_like(x1_ref)

    out_ref[...] = _mlp_ln(x1_ref, w1_ref, w2_ref, g2_ref[...], b2_ref[...])
    branch = []
    for o_ref, p_ref, d in zip(o_refs, p_refs, dilations):
        if d == 1:
            branch.append(o_ref[0].astype(F32))
            continue
        w = grp // d
        nat = []
        for gi in range(tm // grp):
            perm = jnp.concatenate([o_ref[r, gi * w:(gi + 1) * w, :] for r in range(d)], axis=0)
            nat.append(lax.dot_general(p_ref[...], perm, _TN, preferred_element_type=F32))
        branch.append(jnp.concatenate(nat, axis=0))
    parts = [ref[...] for ref in lse_refs]
    m = functools.reduce(jnp.maximum, parts)
    es = [jnp.exp(p - m) for p in parts]
    den = functools.reduce(jnp.add, es)
    wts = [e / den for e in es]
    heads = []
    for h in range(nh):
        hs = slice(h * dh, (h + 1) * dh)
        heads.append(functools.reduce(jnp.add, [wts[g][:, h:h + 1] * branch[g][:, hs] for g in range(n)]))
    proj = _bdot(jnp.concatenate(heads, axis=1), w_ref[...])
    x1_ref[...] = _layer_norm(DEEPNORM_ALPHA * x_ref[...] + proj, g1_ref[...], b1_ref[...])


def _merge_outproj_mlp(outs, lses, dilations, w, x2d, ln1, mlp, ln2, *, batch, seq_len):
    dm = x2d.shape[1]
    tm = LAYER_ROWS
    assert seq_len % tm == 0 and tm % DIL_PERM_ROWS == 0 and mlp[0].shape[1] % MLP_HIDDEN_CHUNK == 0
    x3 = x2d.reshape(batch, seq_len, dm)
    perms = [_residue_perm(d) for d in dilations]
    bps = seq_len // tm
    n_tiles = batch * bps
    tile = lambda s: jnp.minimum(s, n_tiles - 1)
    done = lambda s: jnp.maximum(s - 1, 0)
    const = lambda a: pl.BlockSpec(a.shape, lambda s: (0,) * a.ndim, pipeline_mode=pl.Buffered(1))
    rows3 = lambda a: pl.BlockSpec((None, tm, a.shape[2]), lambda s: (tile(s) // bps, tile(s) % bps, 0))
    in_specs = ([pl.BlockSpec((None, d, tm // d, C_WIDTH), lambda s: (tile(s) // bps, 0, tile(s) % bps, 0))
                 for d in dilations]
                + [rows3(a) for a in lses] + [const(p) for p in perms] + [const(w), rows3(x3)]
                + [const(a) for a in (*ln1, *mlp, *ln2)])
    out = pl.pallas_call(
        functools.partial(_merge_outproj_mlp_kernel, dilations=tuple(dilations)),
        grid=(n_tiles + 1,),
        in_specs=in_specs,
        out_specs=pl.BlockSpec((None, tm, dm), lambda s: (done(s) // bps, done(s) % bps, 0)),
        out_shape=jax.ShapeDtypeStruct((batch, seq_len, dm), F32),
        scratch_shapes=[pltpu.VMEM((tm, dm), F32)],
        compiler_params=_params("arbitrary", fuse_inputs=(
            [False] * (3 * len(dilations)) + [True, False, False, False, True, True, False, False])),
        name="merge_outproj_mlp",
    )(*outs, *lses, *perms, w, x3, *ln1, *mlp, *ln2)
    return out.reshape(batch * seq_len, dm)


def _rwkv_moba_layer(x2d, w_in, shift_mix, w0, w_up, a0, a_up, g_up, k_k, k_a, r_k, lnx_g, lnx_b,
                     w_out, ln1, mlp, ln2, *, batch, seq_len):
    d = RWKV_DIM
    tabs = _rotary_tables(seq_len, MOBA_HEAD_DIM ** -0.5)
    w_m = w_in[:, RWKV_PROJ_COLS:].astype(BF16)
    w_m = jnp.concatenate([_rotary_weight(w_m[:, :2 * MOBA_WIDTH], 2 * MOBA_HEADS),
                           w_m[:, 2 * MOBA_WIDTH:]], axis=1)
    kinds = [0] * MOBA_HEADS + [1] * MOBA_HEADS + [None] * MOBA_HEADS
    qkv = _rotary_proj(x2d, w_m, tabs, kinds, seq_len=seq_len)
    y_b = _moba(qkv, batch=batch, seq_len=seq_len)
    pad = RWKV_Z_COLS - RWKV_PROJ_COLS
    w_r = jnp.pad(w_in[:, :RWKV_PROJ_COLS], ((0, 0), (0, pad))).astype(BF16)
    mix = jnp.pad(shift_mix, (0, pad))[None, :]
    wl = jnp.zeros((LORA_PAD, 3 * d), F32)
    wl = wl.at[0:DECAY_LORA, 0:d].set(w_up)
    wl = wl.at[DECAY_LORA:DECAY_LORA + ICLR_LORA, d:2 * d].set(a_up)
    wl = wl.at[DECAY_LORA + ICLR_LORA:LORA_COLS, 2 * d:3 * d].set(g_up).astype(BF16)
    par = jnp.stack([w0, a0, k_k, k_a, r_k.reshape(d), lnx_g, lnx_b, jnp.zeros((d,), F32)])
    y_a = _rwkv_scan(x2d, w_r, mix, wl, par, batch=batch, seq_len=seq_len)
    return _outproj_mlp([y_a, y_b], w_out.astype(BF16), x2d, ln1, mlp, ln2)


def _dilated_layer(x2d, w_in, w_out, ln1, mlp, ln2, *, batch, seq_len):
    tabs = _rotary_tables(seq_len, DIL_HEAD_DIM ** -0.5)
    nqk = (DIL_GROUPS + 1) * C_WIDTH
    w_c = w_in.astype(BF16)
    w_c = jnp.concatenate([_rotary_weight(w_c[:, :nqk], (DIL_GROUPS + 1) * DIL_HEADS), w_c[:, nqk:]], axis=1)
    dilations = [dilation for _, dilation in DIL_PAIRS]
    assert all(span // dilation == DIL_WINDOW for span, dilation in DIL_PAIRS)
    srcs = _dilated_proj(x2d, w_c, tabs, dilations, batch=batch, seq_len=seq_len)
    outs, lses = [], []
    for src in srcs:
        o, lse = _dilated_group(src, (0, 1, 2), seq_len=seq_len)
        outs.append(o)
        lses.append(lse)
    return _merge_outproj_mlp(outs, lses, dilations, w_out.astype(BF16), x2d, ln1, mlp, ln2,
                              batch=batch, seq_len=seq_len)


def kernel(x, ab_w_in, ab_shift_mix, ab_w0, ab_w_up, ab_a0, ab_a_up, ab_g_up, ab_k_k, ab_k_a, ab_r_k,
           ab_lnx_g, ab_lnx_b, ab_w_out, c_w_in, c_w_out, ln1_g, ln1_b, mlp_w1, mlp_w2, ln2_g, ln2_b):
    batch, seq_len, dm = x.shape
    assert dm == D_MODEL
    h = x.reshape(batch * seq_len, dm)
    depth = ln1_g.shape[0]
    for layer in range(depth):
        j = layer // 2
        ln1 = (ln1_g[layer][None, :], ln1_b[layer][None, :])
        ln2 = (ln2_g[layer][None, :], ln2_b[layer][None, :])
        mlp = (mlp_w1[layer].astype(BF16), mlp_w2[layer].astype(BF16))
        if layer % 2 == 0:
            h = _rwkv_moba_layer(h, ab_w_in[j], ab_shift_mix[j], ab_w0[j], ab_w_up[j], ab_a0[j], ab_a_up[j],
                                 ab_g_up[j], ab_k_k[j], ab_k_a[j], ab_r_k[j], ab_lnx_g[j], ab_lnx_b[j],
                                 ab_w_out[j], ln1, mlp, ln2, batch=batch, seq_len=seq_len)
        else:
            h = _dilated_layer(h, c_w_in[j], c_w_out[j], ln1, mlp, ln2, batch=batch, seq_len=seq_len)
    return h.reshape(batch, seq_len, dm)
```
